```python
import jax
import jax.numpy as jnp
from jax import lax
import numpy as np

D_MODEL = 1024
BATCH = 4
SEQ = 8192
DEPTH = 2

CHUNK = 64
Q_BLOCK = 128
HEAD_DIM = 64
ROPE_THETA = 10000.0

DSA_HEADS = 8
IDX_HEADS = 4
IDX_DIM = 64
DSA_TOPK_MAX = 256

FOX_HEADS = 8

HGRN_HEADS = 8
HGRN_DK = 64
HGRN_DV = 64

N_BRANCH = 3

N_EXPERTS = 32
TOP_K = 4
D_EXPERT = D_MODEL
SWIGLU_LIMIT = 7.0
SWIGLU_ALPHA = 1.702
MOE_BLOCK = 256

LN_EPS = 1e-5
RMS_EPS = 1e-6
DEEPNORM_ALPHA = (2 * DEPTH) ** 0.25
DEEPNORM_BETA = (8 * DEPTH) ** -0.25

IN_SPLITS = (
    DSA_HEADS * HEAD_DIM, DSA_HEADS * HEAD_DIM, DSA_HEADS * HEAD_DIM,
    IDX_HEADS * IDX_DIM, IDX_DIM, IDX_HEADS,
    FOX_HEADS * HEAD_DIM, FOX_HEADS * HEAD_DIM, FOX_HEADS * HEAD_DIM,
    FOX_HEADS,
    HGRN_HEADS * HGRN_DK, HGRN_HEADS * HGRN_DK,
    HGRN_HEADS * HGRN_DV, HGRN_HEADS * HGRN_DV,
    N_BRANCH * D_MODEL,
)
D_IN = sum(IN_SPLITS)
SPLIT_POINTS = tuple(int(v) for v in np.cumsum(IN_SPLITS)[:-1])

kernel_name = 'hybrid_dsa_fox_hgrn2_moe_deepnorm'


def layer_norm(x, g, b):
    x32 = x.astype(jnp.float32)
    mu = jnp.mean(x32, axis=-1, keepdims=True)
    var = jnp.mean(jnp.square(x32 - mu), axis=-1, keepdims=True)
    return ((x32 - mu) * lax.rsqrt(var + LN_EPS) * g + b).astype(x.dtype)


def rope(t, pos):
    half = t.shape[-1] // 2
    inv = ROPE_THETA ** (-jnp.arange(half, dtype=jnp.float32) / half)
    ang = pos.astype(jnp.float32)[:, None] * inv[None, :]
    cos = jnp.cos(ang)[None, :, None, :]
    sin = jnp.sin(ang)[None, :, None, :]
    t32 = t.astype(jnp.float32)
    t1, t2 = t32[..., :half], t32[..., half:]
    return jnp.concatenate([t1 * cos - t2 * sin, t2 * cos + t1 * sin], axis=-1).astype(t.dtype)


def dsa_attention(q, k, v, iq, ik, iw):
    B, S, H, Dh = q.shape
    k_sel = min(DSA_TOPK_MAX, S // 4)
    n_blk = S // Q_BLOCK
    key_pos = jnp.arange(S)
    scale = Dh ** -0.5
    iscale = IDX_DIM ** -0.5
    ik32 = ik.astype(jnp.float32)
    gather = jax.vmap(lambda t, idx: t[idx])

    def block(i):
        t0 = i * Q_BLOCK
        qpos = t0 + jnp.arange(Q_BLOCK)
        iq_b = lax.dynamic_slice_in_dim(iq, t0, Q_BLOCK, axis=1).astype(jnp.float32)
        iw_b = lax.dynamic_slice_in_dim(iw, t0, Q_BLOCK, axis=1)
        rel = jax.nn.relu(jnp.einsum('bqhd,bsd->bqhs', iq_b, ik32) * iscale)
        score = jnp.einsum('bqhs,bqh->bqs', rel, iw_b)
        admissible = key_pos[None, :] < ((qpos // CHUNK + 1) * CHUNK)[:, None]
        score = jnp.where(admissible[None], score, -jnp.inf)
        sel_score, sel_idx = lax.top_k(score, k_sel)
        valid = jnp.isfinite(sel_score)
        kg = gather(k, sel_idx)
        vg = gather(v, sel_idx)
        q_b = lax.dynamic_slice_in_dim(q, t0, Q_BLOCK, axis=1)
        logits = jnp.einsum('bqhd,bqkhd->bhqk', q_b, kg,
                            preferred_element_type=jnp.float32) * scale
        logits = jnp.where(valid[:, None], logits, -jnp.inf)
        p = jax.nn.softmax(logits, axis=-1)
        return jnp.einsum('bhqk,bqkhd->bqhd', p.astype(v.dtype), vg)

    out = lax.map(block, jnp.arange(n_blk))
    return out.transpose(1, 0, 2, 3, 4).reshape(B, S, H * Dh)


def fox_attention(q, k, v, log_f):
    B, S, H, Dh = q.shape
    n_blk = S // Q_BLOCK
    key_pos = jnp.arange(S)
    scale = Dh ** -0.5
    c = jnp.cumsum(log_f, axis=1).transpose(0, 2, 1)

    def block(i):
        t0 = i * Q_BLOCK
        qpos = t0 + jnp.arange(Q_BLOCK)
        q_b = lax.dynamic_slice_in_dim(q, t0, Q_BLOCK, axis=1)
        c_b = lax.dynamic_slice_in_dim(c, t0, Q_BLOCK, axis=2)
        logits = (jnp.einsum('bqhd,bshd->bhqs', q_b, k,
                             preferred_element_type=jnp.float32) * scale
                  + c_b[..., None] - c[:, :, None, :])
        causal = key_pos[None, :] <= qpos[:, None]
        logits = jnp.where(causal, logits, -jnp.inf)
        p = jax.nn.softmax(logits, axis=-1)
        return jnp.einsum('bhqs,bshd->bqhd', p.astype(v.dtype), v)

    out = lax.map(block, jnp.arange(n_blk))
    return out.transpose(1, 0, 2, 3, 4).reshape(B, S, H * Dh)


def hgrn2_recurrence(q, k, v, log_f):
    B, S, H, DK = q.shape
    DV = v.shape[-1]
    n_c = S // CHUNK

    def to_chunks(t):
        return t.reshape(B, n_c, CHUNK, H, t.shape[-1]).transpose(1, 0, 3, 2, 4)

    tri = jnp.tril(jnp.ones((CHUNK, CHUNK), dtype=bool))

    def step(state, inp):
        qc, kc, vc, lfc = inp
        b = jnp.cumsum(lfc, axis=2)
        inter = jnp.einsum('bhtd,bhde->bhte', qc * jnp.exp(b), state)
        decay = jnp.exp(jnp.where(tri[:, :, None],
                                  b[:, :, :, None, :] - b[:, :, None, :, :], -jnp.inf))
        scores = jnp.einsum('bhtsd,bhsd->bhts', qc[:, :, :, None, :] * decay, kc)
        intra = jnp.einsum('bhts,bhse->bhte', scores, vc)
        b_last = b[:, :, -1, :]
        new_state = (jnp.exp(b_last)[..., None] * state
                     + jnp.einsum('bhsd,bhse->bhde', kc * jnp.exp(b_last[:, :, None, :] - b), vc))
        return new_state, inter + intra

    state0 = jnp.zeros((B, H, DK, DV), jnp.float32)
    _, out = lax.scan(step, state0, (to_chunks(q), to_chunks(k), to_chunks(v), to_chunks(log_f)))
    return out.transpose(1, 0, 3, 2, 4).reshape(B, S, H, DV)


def token_mixers(x, w_in, b_fox_f, lb, hgrn_norm_g, w_pa, w_pb, w_pc, w_out):
    B, S, _ = x.shape
    pos = jnp.arange(S)
    z = x @ w_in
    (aq, ak, av, iq, ik, iw, fq, fk, fv, ff, cq, cf, ci, cg, gt) = jnp.split(z, SPLIT_POINTS, axis=-1)

    def heads(t, h):
        return t.reshape(B, S, h, -1)

    aq = rope(heads(aq, DSA_HEADS), pos)
    ak = rope(heads(ak, DSA_HEADS), pos)
    av = heads(av, DSA_HEADS)
    iq = rope(heads(iq, IDX_HEADS), pos)
    ik = rope(ik[:, :, None, :], pos)[:, :, 0, :]
    iw = iw.astype(jnp.float32) * IDX_HEADS ** -0.5
    o_a = dsa_attention(aq, ak, av, iq, ik, iw)

    log_f_fox = jax.nn.log_sigmoid(ff.astype(jnp.float32) + b_fox_f.astype(jnp.float32))
    o_b = fox_attention(heads(fq, FOX_HEADS), heads(fk, FOX_HEADS), heads(fv, FOX_HEADS), log_f_fox)

    zf = cf.astype(jnp.float32)
    log_f_c = jnp.logaddexp(jnp.log(lb), jnp.log1p(-lb) + jax.nn.log_sigmoid(zf))
    k_c = (1.0 - lb) * jax.nn.sigmoid(-zf)
    o_c = hgrn2_recurrence(heads(cq.astype(jnp.float32), HGRN_HEADS), heads(k_c, HGRN_HEADS),
                           heads(ci.astype(jnp.float32), HGRN_HEADS), heads(log_f_c, HGRN_HEADS))
    o_c = (o_c * lax.rsqrt(jnp.mean(jnp.square(o_c), axis=-1, keepdims=True) + RMS_EPS)
           * hgrn_norm_g.astype(jnp.float32).reshape(HGRN_HEADS, HGRN_DV)
           * jax.nn.sigmoid(heads(cg.astype(jnp.float32), HGRN_HEADS)))
    o_c = o_c.reshape(B, S, HGRN_HEADS * HGRN_DV).astype(x.dtype)

    g = jax.nn.sigmoid(gt.reshape(B, S, N_BRANCH, D_MODEL))
    merged = g[:, :, 0] * (o_a @ w_pa) + g[:, :, 1] * (o_b @ w_pb) + g[:, :, 2] * (o_c @ w_pc)
    return merged @ w_out


def moe_ffn(h, w_router, b_router, w_gu, b_gu, w_down, b_down):
    B, S, D = h.shape
    xt = h.reshape(-1, D)
    n_tok = xt.shape[0]
    n_assign = n_tok * TOP_K
    logits = xt.astype(jnp.float32) @ w_router.astype(jnp.float32) + b_router.astype(jnp.float32)
    top_val, top_idx = lax.top_k(logits, TOP_K)
    gates = jax.nn.softmax(top_val, axis=-1)

    flat_e = top_idx.reshape(-1).astype(jnp.int32)
    order = jnp.argsort(flat_e).astype(jnp.int32)
    sorted_e = flat_e[order]
    counts = jnp.bincount(flat_e, length=N_EXPERTS).astype(jnp.int32)
    padded = (counts + MOE_BLOCK - 1) // MOE_BLOCK * MOE_BLOCK
    pad_end = jnp.cumsum(padded)
    pad_start = pad_end - padded
    start = jnp.cumsum(counts) - counts
    rank = jnp.arange(n_assign, dtype=jnp.int32) - start[sorted_e]
    dest = pad_start[sorted_e] + rank
    n_blocks = -(-(n_assign + N_EXPERTS * (MOE_BLOCK - 1)) // MOE_BLOCK)
    tok_sorted = order // TOP_K
    slot_tok = jnp.zeros((n_blocks * MOE_BLOCK,), jnp.int32).at[dest].set(tok_sorted)
    blk_e = jnp.searchsorted(pad_end, jnp.arange(n_blocks, dtype=jnp.int32) * MOE_BLOCK, side='right')
    blk_e = jnp.minimum(blk_e, N_EXPERTS - 1)

    def expert_block(args):
        tok, e = args
        xb = xt[tok]
        gu = xb @ w_gu[e] + b_gu[e]
        gate = jnp.minimum(gu[:, :D_EXPERT], SWIGLU_LIMIT)
        up = jnp.clip(gu[:, D_EXPERT:], -SWIGLU_LIMIT, SWIGLU_LIMIT)
        act = gate * jax.nn.sigmoid(SWIGLU_ALPHA * gate) * (up + 1.0)
        return act @ w_down[e] + b_down[e]

    y_slots = lax.map(expert_block, (slot_tok.reshape(n_blocks, MOE_BLOCK), blk_e))
    y_slots = y_slots.reshape(n_blocks * MOE_BLOCK, D)
    gate_sorted = gates.reshape(-1)[order].astype(y_slots.dtype)
    y_assign = y_slots[dest] * gate_sorted[:, None]
    out = jax.ops.segment_sum(y_assign, tok_sorted, num_segments=n_tok)
    return out.reshape(B, S, D).astype(h.dtype)


def setup_inputs(seed: int = 0) -> dict:
    key = jax.random.key(seed)
    ks = jax.random.split(key, 20)
    f32 = jnp.float32
    wa = DSA_HEADS * HEAD_DIM
    wb = FOX_HEADS * HEAD_DIM
    wc = HGRN_HEADS * HGRN_DV
    nrm = lambda k, shape: jax.random.normal(k, shape, f32)
    return {
        'x': nrm(ks[0], (BATCH, SEQ, D_MODEL)),
        'w_in': nrm(ks[1], (DEPTH, D_MODEL, D_IN)) * D_MODEL ** -0.5,
        'b_fox_f': 3.0 + 0.1 * nrm(ks[2], (DEPTH, FOX_HEADS)),
        'hgrn_lb_logits': 0.5 * nrm(ks[3], (DEPTH, HGRN_HEADS * HGRN_DK)),
        'hgrn_norm_g': 1.0 + 0.02 * nrm(ks[4], (DEPTH, HGRN_HEADS * HGRN_DV)),
        'w_branch_a': nrm(ks[5], (DEPTH, wa, D_MODEL)) * wa ** -0.5 * DEEPNORM_BETA,
        'w_branch_b': nrm(ks[6], (DEPTH, wb, D_MODEL)) * wb ** -0.5 * DEEPNORM_BETA,
        'w_branch_c': nrm(ks[7], (DEPTH, wc, D_MODEL)) * wc ** -0.5 * DEEPNORM_BETA,
        'w_out': nrm(ks[8], (DEPTH, D_MODEL, D_MODEL)) * D_MODEL ** -0.5 * DEEPNORM_BETA,
        'ln1_g': 1.0 + 0.02 * nrm(ks[9], (DEPTH, D_MODEL)),
        'ln1_b': 0.02 * nrm(ks[10], (DEPTH, D_MODEL)),
        'w_router': nrm(ks[11], (DEPTH, D_MODEL, N_EXPERTS)) * D_MODEL ** -0.5,
        'b_router': 0.01 * nrm(ks[12], (DEPTH, N_EXPERTS)),
        'w_gu': nrm(ks[13], (DEPTH, N_EXPERTS, D_MODEL, 2 * D_EXPERT)) * D_MODEL ** -0.5,
        'b_gu': 0.01 * nrm(ks[14], (DEPTH, N_EXPERTS, 2 * D_EXPERT)),
        'w_down': nrm(ks[15], (DEPTH, N_EXPERTS, D_EXPERT, D_MODEL)) * D_EXPERT ** -0.5 * DEEPNORM_BETA,
        'b_down': 0.01 * nrm(ks[16], (DEPTH, N_EXPERTS, D_MODEL)),
        'ln2_g': 1.0 + 0.02 * nrm(ks[17], (DEPTH, D_MODEL)),
        'ln2_b': 0.02 * nrm(ks[18], (DEPTH, D_MODEL)),
    }


def reference(x, w_in, b_fox_f, hgrn_lb_logits, hgrn_norm_g, w_branch_a, w_branch_b, w_branch_c,
              w_out, ln1_g, ln1_b, w_router, b_router, w_gu, b_gu, w_down, b_down, ln2_g, ln2_b):
    p = jax.nn.softmax(hgrn_lb_logits.astype(jnp.float32), axis=0)
    lbs = jnp.cumsum(p, axis=0)
    lbs = lbs - lbs[0]
    for l in range(DEPTH):
        mix = token_mixers(x, w_in[l], b_fox_f[l], lbs[l], hgrn_norm_g[l],
                           w_branch_a[l], w_branch_b[l], w_branch_c[l], w_out[l])
        x = layer_norm(DEEPNORM_ALPHA * x + mix, ln1_g[l], ln1_b[l])
        ffn = moe_ffn(x, w_router[l], b_router[l], w_gu[l], b_gu[l], w_down[l], b_down[l])
        x = layer_norm(DEEPNORM_ALPHA * x + ffn, ln2_g[l], ln2_b[l])
    return x
```

```python
import functools
import math

import jax
import jax.numpy as jnp
import numpy as np
from jax import lax
from jax.experimental import pallas as pl
from jax.experimental.pallas import tpu as pltpu

F32 = jnp.float32
BF16 = jnp.bfloat16
I32 = jnp.int32

CHUNK = 64
HEAD_DIM = 64
ROPE_THETA = 10000.0
DSA_TOPK_MAX = 256
IDX_HEADS = 4
N_EXPERTS = 32
TOP_K = 4
SWIGLU_LIMIT = 7.0
SWIGLU_ALPHA = 1.702
MOE_BLOCK = 256
LN_EPS = 1e-5
RMS_EPS = 1e-6

LOG2E = 1.4426950408889634
NEG = -1e30
INT_MIN = -2147483648
LANES = 128
HGRN_SUB = 16
VMEM_LIMIT = 56 * 1024 * 1024

IN_SPLITS = (512, 512, 512, 256, 64, 4, 512, 512, 512, 8, 512, 512, 512, 512, 3072)
_OFF = tuple(int(v) for v in np.cumsum((0,) + IN_SPLITS))


def _params(*sem):
    return pltpu.CompilerParams(dimension_semantics=sem, vmem_limit_bytes=VMEM_LIMIT)


def _dot_nt(a, b):
    return lax.dot_general(a, b, (((1,), (1,)), ((), ())), preferred_element_type=F32)


def _dot_tn(a, b):
    return lax.dot_general(a, b, (((0,), (0,)), ((), ())), preferred_element_type=F32)


def _sigmoid(t):
    return 1.0 / (1.0 + jnp.exp(-t))


def _rope_tile(t, cos, sin):
    w = t.shape[1]
    lane_d = lax.broadcasted_iota(I32, t.shape, 1) & (HEAD_DIM - 1)
    partner = jnp.where(lane_d < HEAD_DIM // 2,
                        pltpu.roll(t, w - HEAD_DIM // 2, axis=1),
                        pltpu.roll(t, HEAD_DIM // 2, axis=1))
    return t * cos + partner * sin


def _proj_dsa_kernel(x_ref, w_ref, cos_ref, sin_ref, q_ref, k_ref, v_ref):
    z = jnp.dot(x_ref[...].astype(BF16), w_ref[...], preferred_element_type=F32)
    cos = cos_ref[...]
    sin = sin_ref[...]
    q_ref[...] = (_rope_tile(z[:, 0:512], cos, sin) * (HEAD_DIM ** -0.5 * LOG2E)).astype(BF16)
    k_ref[...] = _rope_tile(z[:, 512:1024], cos, sin).astype(BF16)
    v_ref[...] = z[:, 1024:1536].astype(BF16)


def _proj_idx_kernel(x_ref, w_ref, cos_ref, sin_ref, iq_ref, ik_ref, sm_ref):
    z = jnp.dot(x_ref[...].astype(BF16), w_ref[...], preferred_element_type=F32)
    cos = cos_ref[...][:, 0:256]
    sin = sin_ref[...][:, 0:256]
    iq_ref[...] = (_rope_tile(z[:, 0:256], cos, sin) * (HEAD_DIM ** -0.5)).astype(BF16)
    ik_ref[...] = _rope_tile(z[:, 256:512], cos, sin).astype(BF16)
    sm_ref[...] = z[:, 512:640]


def _proj_fox_kernel(x_ref, w_ref, q_ref, k_ref, v_ref):
    z = jnp.dot(x_ref[...].astype(BF16), w_ref[...], preferred_element_type=F32)
    q_ref[...] = (z[:, 0:512] * (HEAD_DIM ** -0.5 * LOG2E)).astype(BF16)
    k_ref[...] = z[:, 512:1024].astype(BF16)
    v_ref[...] = z[:, 1024:1536].astype(BF16)


def _proj_plain_kernel(x_ref, w_ref, o_ref):
    o_ref[...] = jnp.dot(x_ref[...].astype(BF16), w_ref[...], preferred_element_type=F32)


def _proj_call(kernel_fn, x2, w, extra, outs, tm):
    n, d = x2.shape
    nout = w.shape[1]
    in_specs = [pl.BlockSpec((tm, d), lambda i: (i, 0)),
                pl.BlockSpec((d, nout), lambda i: (0, 0))]
    args = [x2, w]
    for arr, spec in extra:
        in_specs.append(spec)
        args.append(arr)
    out_shape = [jax.ShapeDtypeStruct((n, wdt), dt) for wdt, dt in outs]
    out_specs = [pl.BlockSpec((tm, wdt), lambda i: (i, 0)) for wdt, _ in outs]
    return pl.pallas_call(
        kernel_fn, grid=(n // tm,), in_specs=in_specs, out_specs=out_specs,
        out_shape=out_shape, compiler_params=_params("parallel"))(*args)


def _fox_bias_kernel(ff_ref, b_ref, nb_ref):
    s = ff_ref.shape[2]
    z = ff_ref[0] + b_ref[...]
    lf = jnp.minimum(z, 0.0) - jnp.log1p(jnp.exp(-jnp.abs(z)))
    r = lax.broadcasted_iota(I32, (LANES, LANES), 0)
    c = lax.broadcasted_iota(I32, (LANES, LANES), 1)
    upper = jnp.where(r <= c, 1.0, 0.0).astype(F32)
    carry = jnp.zeros((lf.shape[0], 1), F32)
    for t in range(s // LANES):
        blk = lf[:, t * LANES:(t + 1) * LANES]
        cs = jnp.dot(blk, upper, preferred_element_type=F32,
                     precision=lax.Precision.HIGHEST) + carry
        nb_ref[0, :, t * LANES:(t + 1) * LANES] = cs * (-LOG2E)
        carry = cs[:, LANES - 1:LANES]


def _fox_bias(ff_t, b_fox):
    bsz, h, s = ff_t.shape
    return pl.pallas_call(
        _fox_bias_kernel, grid=(bsz,),
        in_specs=[pl.BlockSpec((1, h, s), lambda b: (b, 0, 0)),
                  pl.BlockSpec((h, 1), lambda b: (0, 0))],
        out_specs=pl.BlockSpec((1, h, s), lambda b: (b, 0, 0)),
        out_shape=jax.ShapeDtypeStruct((bsz, h, s), F32),
        compiler_params=_params("parallel"))(ff_t, b_fox.reshape(h, 1).astype(F32))


def _dsa_index_kernel(iq_ref, ik_ref, sm_ref, bias_ref, keys_scr, j_scr, *, tq, tk, seq, ksel):
    i = pl.program_id(1)
    n_kt = ((i + 1) * tq + tk - 1) // tk
    row_pos = i * tq + lax.broadcasted_iota(I32, (tq, 1), 0)
    adm_end = (row_pos // CHUNK + 1) * CHUNK

    iq = iq_ref[...]
    lane = lax.broadcasted_iota(I32, iq.shape, 1)
    iqm = [jnp.where((lane >= HEAD_DIM * h) & (lane < HEAD_DIM * (h + 1)), iq, jnp.zeros_like(iq))
           for h in range(IDX_HEADS)]
    sm = sm_ref[...]
    wts = [sm[:, h:h + 1] * (IDX_HEADS ** -0.5) for h in range(IDX_HEADS)]

    def cols(k0):
        return k0 + lax.broadcasted_iota(I32, (tq, tk), 1)

    def score_tile(kt, carry):
        k0 = pl.multiple_of(kt * tk, tk)
        ikt = ik_ref[pl.ds(k0, tk), :]
        sc = jnp.zeros((tq, tk), F32)
        for h in range(IDX_HEADS):
            sc = sc + jnp.maximum(_dot_nt(iqm[h], ikt), 0.0) * wts[h]
        bits = pltpu.bitcast(sc, I32)
        sign = bits >> 31
        key = (bits ^ (sign & 0x7FFFFFFF)) - sign
        keys_scr[:, pl.ds(k0, tk)] = jnp.where(cols(k0) < adm_end, key, INT_MIN)
        return carry

    lax.fori_loop(0, n_kt, score_tile, 0)

    def count(pred):
        def body(kt, acc):
            k0 = pl.multiple_of(kt * tk, tk)
            hit = jnp.where(pred(keys_scr[:, pl.ds(k0, tk)], k0), 1.0, 0.0)
            part = hit[:, 0:LANES]
            for c in range(1, tk // LANES):
                part = part + hit[:, c * LANES:(c + 1) * LANES]
            return acc + part
        acc = lax.fori_loop(0, n_kt, body, jnp.zeros((tq, LANES), F32))
        return jnp.sum(acc, axis=1, keepdims=True)

    def count_ge(cand):
        return count(lambda kk, k0: kk >= cand)

    kf = float(ksel)
    zero = jnp.zeros((tq, 1), I32)
    thr = jnp.where(count_ge(zero) >= kf, zero, jnp.full((tq, 1), INT_MIN, I32))

    def bit_step(it, t):
        cand = t | jnp.left_shift(jnp.int32(1), 30 - it)
        return jnp.where(count_ge(cand) >= kf, cand, t)

    thr = lax.fori_loop(0, 31, bit_step, thr)
    c_ge = count_ge(thr)
    need = kf - count_ge(thr + 1)
    tie = (c_ge > kf) & (thr > INT_MIN)
    thr = jnp.maximum(thr, INT_MIN + 1)

    j_scr[...] = jnp.full((tq, 1), 2 * seq, I32)

    @pl.when(jnp.max(jnp.where(tie, 1.0, 0.0)) > 0.0)
    def _():
        def idx_step(it, x):
            cand = x | jnp.left_shift(jnp.int32(1), 13 - it)
            cnt = count(lambda kk, k0: (kk == thr) & (cols(k0) < cand))
            return jnp.where(cnt < need, cand, x)
        x = lax.fori_loop(0, 14, idx_step, jnp.zeros((tq, 1), I32))
        j_scr[...] = jnp.where(tie, x, 2 * seq)

    jmax = j_scr[...]

    def write_tile(kt, carry):
        k0 = pl.multiple_of(kt * tk, tk)
        kk = keys_scr[:, pl.ds(k0, tk)]
        sel = (kk > thr) | ((kk == thr) & (cols(k0) <= jmax))
        bias_ref[0, :, pl.ds(k0, tk)] = jnp.where(sel, 0.0, NEG).astype(BF16)
        return carry

    lax.fori_loop(0, n_kt, write_tile, 0)

    def fill_tile(kt, carry):
        k0 = pl.multiple_of(kt * tk, tk)
        bias_ref[0, :, pl.ds(k0, tk)] = jnp.full((tq, tk), NEG, BF16)
        return carry

    lax.fori_loop(n_kt, seq // tk, fill_tile, 0)


def _dsa_index(iq, ik4, small, bsz, seq, tq, tk):
    ksel = min(DSA_TOPK_MAX, seq // 4)
    nq = seq // tq
    kern = functools.partial(_dsa_index_kernel, tq=tq, tk=tk, seq=seq, ksel=ksel)
    return pl.pallas_call(
        kern, grid=(bsz, nq),
        in_specs=[pl.BlockSpec((tq, 256), lambda b, i: (b * nq + i, 0)),
                  pl.BlockSpec((seq, 256), lambda b, i: (b, 0)),
                  pl.BlockSpec((tq, LANES), lambda b, i: (b * nq + i, 0))],
        out_specs=pl.BlockSpec((1, tq, seq), lambda b, i: (b, i, 0)),
        out_shape=jax.ShapeDtypeStruct((bsz, seq, seq), BF16),
        scratch_shapes=[pltpu.VMEM((tq, seq), I32), pltpu.VMEM((tq, 1), I32)],
        compiler_params=_params("parallel", "parallel"))(iq, ik4, small)


def _flash_kernel(*refs, t, use_mask):
    if use_mask:
        q_ref, k_ref, v_ref, mask_ref, o_ref, m_scr, l_scr, acc_scr = refs
        nb_ref = None
    else:
        q_ref, k_ref, v_ref, nb_ref, o_ref, m_scr, l_scr, acc_scr = refs
        mask_ref = None
    i = pl.program_id(2)
    q = q_ref[...]
    lane = lax.broadcasted_iota(I32, q.shape, 1)
    qm = [jnp.where(lane < HEAD_DIM, q, jnp.zeros_like(q)),
          jnp.where(lane >= HEAD_DIM, q, jnp.zeros_like(q))]
    m_scr[...] = jnp.full(m_scr.shape, NEG, F32)
    l_scr[...] = jnp.zeros(l_scr.shape, F32)
    acc_scr[...] = jnp.zeros(acc_scr.shape, F32)

    def tile(kt, diagonal):
        k0 = pl.multiple_of(kt * t, t)
        kt_ = k_ref[pl.ds(k0, t), :]
        vt = v_ref[pl.ds(k0, t), :]
        if use_mask:
            extra = mask_ref[0, :, pl.ds(k0, t)].astype(F32)
        for j in range(2):
            s = _dot_nt(qm[j], kt_)
            if use_mask:
                s = s + extra
            else:
                s = s + nb_ref[0, 0, j:j + 1, pl.ds(k0, t)]
                if diagonal:
                    r = lax.broadcasted_iota(I32, (t, t), 0)
                    c = lax.broadcasted_iota(I32, (t, t), 1)
                    s = jnp.where(c <= r, s, NEG)
            m_old = m_scr[j]
            m_new = jnp.maximum(m_old, jnp.max(s, axis=1, keepdims=True))
            alpha = jnp.exp2(m_old - m_new)
            p = jnp.exp2(s - m_new)
            l_scr[j] = alpha * l_scr[j] + jnp.sum(p, axis=1, keepdims=True)
            acc_scr[j] = alpha * acc_scr[j] + jnp.dot(p.astype(BF16), vt,
                                                      preferred_element_type=F32)
            m_scr[j] = m_new

    def body(kt, carry):
        tile(kt, False)
        return carry

    if use_mask:
        lax.fori_loop(0, i + 1, body, 0)
    else:
        lax.fori_loop(0, i, body, 0)
        tile(i, True)

    o0 = acc_scr[0] / l_scr[0]
    o1 = acc_scr[1] / l_scr[1]
    o_ref[...] = jnp.where(lane < HEAD_DIM, o0, o1).astype(o_ref.dtype)


def _flash(q, k, v, bsz, seq, t, nb=None, mask=None):
    nq = seq // t
    n_hp = q.shape[1] // LANES
    use_mask = mask is not None
    in_specs = [pl.BlockSpec((t, LANES), lambda b, h, i: (b * nq + i, h)),
                pl.BlockSpec((seq, LANES), lambda b, h, i: (b, h)),
                pl.BlockSpec((seq, LANES), lambda b, h, i: (b, h))]
    if use_mask:
        in_specs.append(pl.BlockSpec((1, t, seq), lambda b, h, i: (b, i, 0)))
        extra = mask
    else:
        in_specs.append(pl.BlockSpec((1, 1, 2, seq), lambda b, h, i: (b, h, 0, 0)))
        extra = nb
    kern = functools.partial(_flash_kernel, t=t, use_mask=use_mask)
    return pl.pallas_call(
        kern, grid=(bsz, n_hp, nq), in_specs=in_specs,
        out_specs=pl.BlockSpec((t, LANES), lambda b, h, i: (b * nq + i, h)),
        out_shape=jax.ShapeDtypeStruct(q.shape, BF16),
        scratch_shapes=[pltpu.VMEM((2, t, 1), F32), pltpu.VMEM((2, t, 1), F32),
                        pltpu.VMEM((2, t, LANES), F32)],
        compiler_params=_params("parallel", "parallel", "parallel"))(q, k, v, extra)


def _hgrn_kernel(q_ref, zf_ref, v_ref, g_ref, lb_ref, ng_ref, o_ref, st_scr, oi_scr, *, tt):
    r = HGRN_SUB
    half = 256

    @pl.when(pl.program_id(1) == 0)
    def _():
        st_scr[...] = jnp.zeros(st_scr.shape, F32)

    q = q_ref[...]
    zf = zf_ref[...]
    v = v_ref[...]
    ls = jnp.minimum(zf, 0.0) - jnp.log1p(jnp.exp(-jnp.abs(zf)))
    a = lb_ref[0:1, :]
    y = lb_ref[1:2, :] + ls
    lf = jnp.maximum(a, y) + jnp.log1p(jnp.exp(-jnp.abs(a - y)))
    kk = lb_ref[2:3, :] * (1.0 / (1.0 + jnp.exp(zf)))

    rin = lax.broadcasted_iota(I32, (tt, 1), 0) & (r - 1)
    b = lf
    sh = 1
    while sh < r:
        b = b + jnp.where(rin >= sh, pltpu.roll(b, sh, axis=0), 0.0)
        sh *= 2

    rr = lax.broadcasted_iota(I32, (half, half), 0) // HEAD_DIM
    cc = lax.broadcasted_iota(I32, (half, half), 1) // HEAD_DIM
    same_head = rr == cc
    ones_bd = jnp.where(same_head, 1.0, 0.0).astype(BF16)
    bd_mask = jnp.where(same_head, 1.0, 0.0).astype(F32)

    o = jnp.zeros((tt, 2 * half), F32)
    for off in range(r):
        if off == 0:
            e = q * kk
            vs = v
        else:
            e = q * pltpu.roll(kk, off, axis=0) * jnp.exp(b - pltpu.roll(b, off, axis=0))
            e = jnp.where(rin >= off, e, 0.0)
            vs = pltpu.roll(v, off, axis=0)
        eb = e.astype(BF16)
        sc = jnp.concatenate(
            [jnp.dot(eb[:, 0:half], ones_bd, preferred_element_type=F32),
             jnp.dot(eb[:, half:], ones_bd, preferred_element_type=F32)], axis=1)
        o = o + sc * vs

    qe = (q * jnp.exp(b)).astype(BF16)
    vb = v.astype(BF16)
    for c in range(tt // r):
        r0 = c * r
        bl = b[r0 + r - 1:r0 + r, :]
        kd = (kk[r0:r0 + r, :] * jnp.exp(bl - b[r0:r0 + r, :])).astype(BF16)
        dec = jnp.exp(bl)
        for h2 in range(2):
            lo = h2 * half
            st = st_scr[h2]
            oi_scr[r0:r0 + r, lo:lo + half] = _dot_nt(qe[r0:r0 + r, lo:lo + half], st.astype(BF16))
            upd = _dot_tn(vb[r0:r0 + r, lo:lo + half], kd[:, lo:lo + half])
            st_scr[h2] = st * dec[:, lo:lo + half] + upd * bd_mask

    o = o + oi_scr[...]
    o2 = o * o
    ones_f = bd_mask
    ms = jnp.concatenate(
        [jnp.dot(o2[:, 0:half], ones_f, preferred_element_type=F32, precision=lax.Precision.HIGHEST),
         jnp.dot(o2[:, half:], ones_f, preferred_element_type=F32, precision=lax.Precision.HIGHEST)],
        axis=1) * (1.0 / HEAD_DIM)
    on = o * lax.rsqrt(ms + RMS_EPS) * ng_ref[...] * _sigmoid(g_ref[...])
    o_ref[...] = on.astype(o_ref.dtype)


def _hgrn(zc, lbp, ng, bsz, seq, tt):
    n = zc.shape[0]
    nt = seq // tt
    col = lambda cidx: pl.BlockSpec((tt, 512), lambda b, j: (b * nt + j, cidx))
    kern = functools.partial(_hgrn_kernel, tt=tt)
    return pl.pallas_call(
        kern, grid=(bsz, nt),
        in_specs=[col(0), col(1), col(2), col(3),
                  pl.BlockSpec((3, 512), lambda b, j: (0, 0)),
                  pl.BlockSpec((1, 512), lambda b, j: (0, 0))],
        out_specs=pl.BlockSpec((tt, 512), lambda b, j: (b * nt + j, 0)),
        out_shape=jax.ShapeDtypeStruct((n, 512), BF16),
        scratch_shapes=[pltpu.VMEM((2, 256, 256), F32), pltpu.VMEM((tt, 512), F32)],
        compiler_params=_params("parallel", "arbitrary"))(zc, zc, zc, zc, lbp, ng)


def _layer_norm(y, g, b):
    mu = jnp.mean(y, axis=-1, keepdims=True)
    d = y - mu
    var = jnp.mean(d * d, axis=-1, keepdims=True)
    return d * lax.rsqrt(var + LN_EPS) * g + b


def _merge_kernel(oa_ref, ob_ref, oc_ref, gt_ref, x_ref, wpa_ref, wpb_ref, wpc_ref, wout_ref,
                  g_ref, b_ref, wr_ref, br_ref, x1_ref, xb_ref, ti_ref, gate_ref, *, alpha):
    d = x_ref.shape[1]
    pa = jnp.dot(oa_ref[...], wpa_ref[...], preferred_element_type=F32)
    pb = jnp.dot(ob_ref[...], wpb_ref[...], preferred_element_type=F32)
    pc = jnp.dot(oc_ref[...], wpc_ref[...], preferred_element_type=F32)
    merged = (_sigmoid(gt_ref[:, 0:d]) * pa + _sigmoid(gt_ref[:, d:2 * d]) * pb
              + _sigmoid(gt_ref[:, 2 * d:3 * d]) * pc)
    mix = jnp.dot(merged.astype(BF16), wout_ref[...], preferred_element_type=F32)
    x1 = _layer_norm(alpha * x_ref[...] + mix, g_ref[...], b_ref[...])
    x1_ref[...] = x1
    xb_ref[...] = x1.astype(BF16)

    logits = jnp.dot(x1, wr_ref[...], preferred_element_type=F32,
                     precision=lax.Precision.HIGHEST) + br_ref[...]
    lane = lax.broadcasted_iota(I32, logits.shape, 1)
    topi = jnp.zeros(logits.shape, I32)
    topv = jnp.full(logits.shape, NEG, F32)
    for k in range(TOP_K):
        m = jnp.max(logits, axis=1, keepdims=True)
        idx = jnp.min(jnp.where(logits == m, lane, LANES), axis=1, keepdims=True)
        topi = jnp.where(lane == k, idx, topi)
        topv = jnp.where(lane == k, m, topv)
        logits = jnp.where(lane == idx, -jnp.inf, logits)
    e = jnp.where(lane < TOP_K, jnp.exp(topv - jnp.max(topv, axis=1, keepdims=True)), 0.0)
    ti_ref[...] = topi
    gate_ref[...] = e / jnp.sum(e, axis=1, keepdims=True)


def _merge(oa, ob, oc, gt, x2, wpa, wpb, wpc, wout, g, b, wr, br, alpha, tm):
    n, d = x2.shape
    row = lambda w: pl.BlockSpec((tm, w), lambda i: (i, 0))
    full = lambda a: pl.BlockSpec(a.shape, lambda i: (0,) * a.ndim)
    kern = functools.partial(_merge_kernel, alpha=alpha)
    return pl.pallas_call(
        kern, grid=(n // tm,),
        in_specs=[row(512), row(512), row(512), row(3 * d), row(d),
                  full(wpa), full(wpb), full(wpc), full(wout), full(g), full(b), full(wr), full(br)],
        out_specs=[row(d), row(d), row(LANES), row(LANES)],
        out_shape=[jax.ShapeDtypeStruct((n, d), F32), jax.ShapeDtypeStruct((n, d), BF16),
                   jax.ShapeDtypeStruct((n, LANES), I32), jax.ShapeDtypeStruct((n, LANES), F32)],
        compiler_params=_params("parallel"))(oa, ob, oc, gt, x2, wpa, wpb, wpc, wout, g, b, wr, br)


def _moe_kernel(be_ref, nu_ref, xs_ref, wgu_ref, bgu_ref, wd_ref, bd_ref, y_ref):
    i = pl.program_id(0)
    de = wd_ref.shape[1]

    @pl.when(i < nu_ref[0])
    def _():
        gu = jnp.dot(xs_ref[...], wgu_ref[0], preferred_element_type=F32) + bgu_ref[0]
        gate = jnp.minimum(gu[:, 0:de], SWIGLU_LIMIT)
        up = jnp.clip(gu[:, de:], -SWIGLU_LIMIT, SWIGLU_LIMIT)
        act = gate * _sigmoid(SWIGLU_ALPHA * gate) * (up + 1.0)
        y_ref[...] = jnp.dot(act.astype(BF16), wd_ref[0], preferred_element_type=F32) + bd_ref[0]

    @pl.when(i >= nu_ref[0])
    def _():
        y_ref[...] = jnp.zeros(y_ref.shape, F32)


def _moe_experts(blk_e, n_used, xs, wgu, bgu, wd, bd):
    n_slots, d = xs.shape
    n_blocks = n_slots // MOE_BLOCK
    e, _, n2 = wgu.shape
    de = wd.shape[1]
    grid_spec = pltpu.PrefetchScalarGridSpec(
        num_scalar_prefetch=2, grid=(n_blocks,),
        in_specs=[pl.BlockSpec((MOE_BLOCK, d), lambda i, be, nu: (i, 0)),
                  pl.BlockSpec((1, d, n2), lambda i, be, nu: (be[i], 0, 0)),
                  pl.BlockSpec((1, 1, n2), lambda i, be, nu: (be[i], 0, 0)),
                  pl.BlockSpec((1, de, d), lambda i, be, nu: (be[i], 0, 0)),
                  pl.BlockSpec((1, 1, d), lambda i, be, nu: (be[i], 0, 0))],
        out_specs=pl.BlockSpec((MOE_BLOCK, d), lambda i, be, nu: (i, 0)))
    return pl.pallas_call(
        _moe_kernel, grid_spec=grid_spec,
        out_shape=jax.ShapeDtypeStruct((n_slots, d), F32),
        compiler_params=_params("arbitrary"))(
            blk_e, n_used, xs, wgu, bgu.reshape(e, 1, n2), wd, bd.reshape(e, 1, d))


def _combine_kernel(yg_ref, gate_ref, x1_ref, g_ref, b_ref, o_ref, *, alpha):
    gates = gate_ref[...]
    acc = alpha * x1_ref[...]
    for k in range(TOP_K):
        acc = acc + gates[:, k:k + 1] * yg_ref[k]
    o_ref[...] = _layer_norm(acc, g_ref[...], b_ref[...])


def _combine(yg, gates, x1, g, b, alpha, tm):
    n, d = x1.shape
    kern = functools.partial(_combine_kernel, alpha=alpha)
    return pl.pallas_call(
        kern, grid=(n // tm,),
        in_specs=[pl.BlockSpec((TOP_K, tm, d), lambda i: (0, i, 0)),
                  pl.BlockSpec((tm, LANES), lambda i: (i, 0)),
                  pl.BlockSpec((tm, d), lambda i: (i, 0)),
                  pl.BlockSpec((1, d), lambda i: (0, 0)),
                  pl.BlockSpec((1, d), lambda i: (0, 0))],
        out_specs=pl.BlockSpec((tm, d), lambda i: (i, 0)),
        out_shape=jax.ShapeDtypeStruct((n, d), F32),
        compiler_params=_params("parallel"))(yg, gates, x1, g, b)


def _rope_tables(seq, width):
    half = HEAD_DIM // 2
    inv = ROPE_THETA ** (-jnp.arange(half, dtype=F32) / half)
    ang = jnp.arange(seq, dtype=F32)[:, None] * inv[None, :]
    cos = jnp.concatenate([jnp.cos(ang), jnp.cos(ang)], axis=1)
    sin = jnp.concatenate([-jnp.sin(ang), jnp.sin(ang)], axis=1)
    reps = width // HEAD_DIM
    return jnp.tile(cos, (1, reps)), jnp.tile(sin, (1, reps))


def _split_w_in(w):
    seg = lambda a, b: w[:, _OFF[a]:_OFF[b]]
    w_dsa = seg(0, 3)
    small = jnp.concatenate([seg(5, 6), seg(9, 10), jnp.zeros((w.shape[0], LANES - 12), w.dtype)], axis=1)
    w_idx = jnp.concatenate([seg(3, 4)] + [seg(4, 5)] * IDX_HEADS + [small], axis=1)
    w_fox = seg(6, 9)
    w_hgrn = seg(10, 14)
    w_gate = seg(14, 15)
    return [t.astype(BF16) for t in (w_dsa, w_idx, w_fox, w_hgrn, w_gate)]


def _route(top_idx, n_tok):
    n_assign = n_tok * TOP_K
    flat_e = top_idx.reshape(-1)
    order = jnp.argsort(flat_e).astype(I32)
    sorted_e = flat_e[order]
    counts = jnp.bincount(flat_e, length=N_EXPERTS).astype(I32)
    padded = (counts + MOE_BLOCK - 1) // MOE_BLOCK * MOE_BLOCK
    pad_end = jnp.cumsum(padded)
    pad_start = pad_end - padded
    start = jnp.cumsum(counts) - counts
    rank = jnp.arange(n_assign, dtype=I32) - start[sorted_e]
    dest = pad_start[sorted_e] + rank
    n_blocks = -(-(n_assign + N_EXPERTS * (MOE_BLOCK - 1)) // MOE_BLOCK)
    slot_tok = jnp.zeros((n_blocks * MOE_BLOCK,), I32).at[dest].set(order // TOP_K)
    blk_e = jnp.searchsorted(pad_end, jnp.arange(n_blocks, dtype=I32) * MOE_BLOCK, side='right')
    blk_e = jnp.minimum(blk_e, N_EXPERTS - 1).astype(I32)
    n_used = (pad_end[-1] // MOE_BLOCK).astype(I32).reshape(1)
    slot_of = jnp.zeros((n_assign,), I32).at[order].set(dest)
    return slot_tok, blk_e, n_used, slot_of.reshape(n_tok, TOP_K).T


def _layer(x2, bsz, seq, alpha, w_in, b_fox, lb, ng, wpa, wpb, wpc, wout, g1, b1,
           wr, br, wgu, bgu, wd, bd, g2, b2, cos, sin):
    n, d = x2.shape
    tm = min(512, seq)
    w_dsa, w_idx, w_fox, w_hgrn, w_gate = _split_w_in(w_in)
    nst = seq // tm
    tab = lambda arr: (arr, pl.BlockSpec((tm, 512), lambda i: (i % nst, 0)))

    aq, ak, av = _proj_call(_proj_dsa_kernel, x2, w_dsa, [tab(cos), tab(sin)],
                            [(512, BF16)] * 3, tm)
    iq, ik4, small = _proj_call(_proj_idx_kernel, x2, w_idx, [tab(cos), tab(sin)],
                                [(256, BF16), (256, BF16), (LANES, F32)], tm)
    fq, fk, fv = _proj_call(_proj_fox_kernel, x2, w_fox, [], [(512, BF16)] * 3, tm)
    (zc,) = _proj_call(_proj_plain_kernel, x2, w_hgrn, [], [(2048, F32)], tm)
    (gt,) = _proj_call(_proj_plain_kernel, x2, w_gate, [], [(3 * d, F32)], tm)

    t_att = min(512, seq)
    mask = _dsa_index(iq, ik4, small, bsz, seq, min(256, seq), min(512, seq))
    o_a = _flash(aq, ak, av, bsz, seq, t_att, mask=mask)

    ff_t = small[:, 4:12].reshape(bsz, seq, 8).transpose(0, 2, 1)
    nb = _fox_bias(ff_t, b_fox).reshape(bsz, 4, 2, seq)
    o_b = _flash(fq, fk, fv, bsz, seq, t_att, nb=nb)

    lbp = jnp.stack([jnp.log(lb), jnp.log1p(-lb), 1.0 - lb]).astype(F32)
    o_c = _hgrn(zc, lbp, ng.reshape(1, -1).astype(F32), bsz, seq, min(256, seq))

    wr_p = jnp.zeros((d, LANES), F32).at[:, :N_EXPERTS].set(wr.astype(F32))
    br_p = jnp.full((1, LANES), NEG, F32).at[0, :N_EXPERTS].set(br.astype(F32))
    x1, x1b, topi, gates = _merge(
        o_a, o_b, o_c, gt, x2, wpa.astype(BF16), wpb.astype(BF16), wpc.astype(BF16),
        wout.astype(BF16), g1.reshape(1, d), b1.reshape(1, d), wr_p, br_p, alpha, min(256, n))

    slot_tok, blk_e, n_used, slot_of = _route(topi[:, :TOP_K], n)
    xs = x1b[slot_tok]
    y_slots = _moe_experts(blk_e, n_used, xs, wgu.astype(BF16), bgu, wd.astype(BF16), bd)
    yg = y_slots[slot_of]
    return _combine(yg, gates, x1, g2.reshape(1, d), b2.reshape(1, d), alpha, min(256, n))


def kernel(x, w_in, b_fox_f, hgrn_lb_logits, hgrn_norm_g, w_branch_a, w_branch_b, w_branch_c, w_out, ln1_g, ln1_b, w_router, b_router, w_gu, b_gu, w_down, b_down, ln2_g, ln2_b):
    bsz, seq, d = x.shape
    depth = w_in.shape[0]
    alpha = (2 * depth) ** 0.25
    p = jax.nn.softmax(hgrn_lb_logits.astype(F32), axis=0)
    lbs = jnp.cumsum(p, axis=0)
    lbs = lbs - lbs[0]
    cos, sin = _rope_tables(seq, 512)
    x2 = x.reshape(bsz * seq, d)
    for l in range(depth):
        x2 = _layer(x2, bsz, seq, alpha, w_in[l], b_fox_f[l], lbs[l], hgrn_norm_g[l],
                    w_branch_a[l], w_branch_b[l], w_branch_c[l], w_out[l], ln1_g[l], ln1_b[l],
                    w_router[l], b_router[l], w_gu[l], b_gu[l], w_down[l], b_down[l],
                    ln2_g[l], ln2_b[l], cos, sin)
    return x2.reshape(bsz, seq, d)
```

```python
import functools
import math

import jax
import jax.numpy as jnp
import numpy as np
from jax import lax
from jax.experimental import pallas as pl
from jax.experimental.pallas import tpu as pltpu

F32 = jnp.float32
BF16 = jnp.bfloat16
I32 = jnp.int32

CHUNK = 64
HEAD_DIM = 64
ROPE_THETA = 10000.0
DSA_TOPK_MAX = 256
IDX_HEADS = 4
N_EXPERTS = 32
TOP_K = 4
SWIGLU_LIMIT = 7.0
SWIGLU_ALPHA = 1.702
MOE_BLOCK = 256
LN_EPS = 1e-5
RMS_EPS = 1e-6

LOG2E = 1.4426950408889634
NEG = -1e30
INT_MIN = -2147483648
LANES = 128
DSA_SEL_ROWS = 128
FLASH_TQ = 2048
FLASH_TK = 512
HGRN_SUB = 16
VMEM_LIMIT = 56 * 1024 * 1024

IN_SPLITS = (512, 512, 512, 256, 64, 4, 512, 512, 512, 8, 512, 512, 512, 512, 3072)
_OFF = tuple(int(v) for v in np.cumsum((0,) + IN_SPLITS))


def _params(*sem):
    return pltpu.CompilerParams(dimension_semantics=sem, vmem_limit_bytes=VMEM_LIMIT)


def _dot_nt(a, b):
    return lax.dot_general(a, b, (((1,), (1,)), ((), ())), preferred_element_type=F32)


def _dot_tn(a, b):
    return lax.dot_general(a, b, (((0,), (0,)), ((), ())), preferred_element_type=F32)


def _sigmoid(t):
    return 1.0 / (1.0 + jnp.exp(-t))


def _rope_tile(t, cos, sin):
    w = t.shape[1]
    lane_d = lax.broadcasted_iota(I32, t.shape, 1) & (HEAD_DIM - 1)
    partner = jnp.where(lane_d < HEAD_DIM // 2,
                        pltpu.roll(t, w - HEAD_DIM // 2, axis=1),
                        pltpu.roll(t, HEAD_DIM // 2, axis=1))
    return t * cos + partner * sin


def _proj_dsa_kernel(x_ref, w_ref, cos_ref, sin_ref, q_ref, k_ref, v_ref):
    z = jnp.dot(x_ref[...].astype(BF16), w_ref[...], preferred_element_type=F32)
    cos = cos_ref[...]
    sin = sin_ref[...]
    q_ref[...] = (_rope_tile(z[:, 0:512], cos, sin) * (HEAD_DIM ** -0.5 * LOG2E)).astype(BF16)
    k_ref[...] = _rope_tile(z[:, 512:1024], cos, sin).astype(BF16)
    v_ref[...] = z[:, 1024:1536].astype(BF16)


def _proj_idx_kernel(x_ref, w_ref, cos_ref, sin_ref, iq_ref, ik_ref, sm_ref):
    z = jnp.dot(x_ref[...].astype(BF16), w_ref[...], preferred_element_type=F32)
    cos = cos_ref[...][:, 0:256]
    sin = sin_ref[...][:, 0:256]
    iq_ref[...] = (_rope_tile(z[:, 0:256], cos, sin) * (HEAD_DIM ** -0.5)).astype(BF16)
    ik_ref[...] = _rope_tile(z[:, 256:512], cos, sin).astype(BF16)
    sm_ref[...] = z[:, 512:640]


def _proj_fox_kernel(x_ref, w_ref, q_ref, k_ref, v_ref):
    z = jnp.dot(x_ref[...].astype(BF16), w_ref[...], preferred_element_type=F32)
    q_ref[...] = (z[:, 0:512] * (HEAD_DIM ** -0.5 * LOG2E)).astype(BF16)
    k_ref[...] = z[:, 512:1024].astype(BF16)
    v_ref[...] = z[:, 1024:1536].astype(BF16)


def _proj_plain_kernel(x_ref, w_ref, o_ref):
    o_ref[...] = jnp.dot(x_ref[...].astype(BF16), w_ref[...], preferred_element_type=F32)


def _proj_call(kernel_fn, x2, w, extra, outs, tm):
    n, d = x2.shape
    nout = w.shape[1]
    in_specs = [pl.BlockSpec((tm, d), lambda i: (i, 0)),
                pl.BlockSpec((d, nout), lambda i: (0, 0))]
    args = [x2, w]
    for arr, spec in extra:
        in_specs.append(spec)
        args.append(arr)
    out_shape = [jax.ShapeDtypeStruct((n, wdt), dt) for wdt, dt in outs]
    out_specs = [pl.BlockSpec((tm, wdt), lambda i: (i, 0)) for wdt, _ in outs]
    return pl.pallas_call(
        kernel_fn, grid=(n // tm,), in_specs=in_specs, out_specs=out_specs,
        out_shape=out_shape, compiler_params=_params("parallel"))(*args)


def _fox_bias_kernel(ff_ref, b_ref, nb_ref):
    s = ff_ref.shape[2]
    z = ff_ref[0] + b_ref[...]
    lf = jnp.minimum(z, 0.0) - jnp.log1p(jnp.exp(-jnp.abs(z)))
    r = lax.broadcasted_iota(I32, (LANES, LANES), 0)
    c = lax.broadcasted_iota(I32, (LANES, LANES), 1)
    upper = jnp.where(r <= c, 1.0, 0.0).astype(F32)
    carry = jnp.zeros((lf.shape[0], 1), F32)
    for t in range(s // LANES):
        blk = lf[:, t * LANES:(t + 1) * LANES]
        cs = jnp.dot(blk, upper, preferred_element_type=F32,
                     precision=lax.Precision.HIGHEST) + carry
        nb_ref[0, :, t * LANES:(t + 1) * LANES] = cs * (-LOG2E)
        carry = cs[:, LANES - 1:LANES]


def _fox_bias(ff_t, b_fox):
    bsz, h, s = ff_t.shape
    return pl.pallas_call(
        _fox_bias_kernel, grid=(bsz,),
        in_specs=[pl.BlockSpec((1, h, s), lambda b: (b, 0, 0)),
                  pl.BlockSpec((h, 1), lambda b: (0, 0))],
        out_specs=pl.BlockSpec((1, h, s), lambda b: (b, 0, 0)),
        out_shape=jax.ShapeDtypeStruct((bsz, h, s), F32),
        compiler_params=_params("parallel"))(ff_t, b_fox.reshape(h, 1).astype(F32))


def _dsa_index_kernel(iq_ref, ik_ref, sm_ref, bias_ref, keys_scr, thr_scr, j_scr, *, tq, tk, seq, ksel):
    i = pl.program_id(1)
    n_kt = ((i + 1) * tq + tk - 1) // tk
    row_pos = i * tq + lax.broadcasted_iota(I32, (tq, 1), 0)
    adm_end = (row_pos // CHUNK + 1) * CHUNK

    iq = iq_ref[...]
    lane = lax.broadcasted_iota(I32, iq.shape, 1)
    iqm = [jnp.where((lane >= HEAD_DIM * h) & (lane < HEAD_DIM * (h + 1)), iq, jnp.zeros_like(iq))
           for h in range(IDX_HEADS)]
    sm = sm_ref[...]
    wts = [sm[:, h:h + 1] * (IDX_HEADS ** -0.5) for h in range(IDX_HEADS)]

    def cols(k0):
        return k0 + lax.broadcasted_iota(I32, (tq, tk), 1)

    def score_tile(kt, carry):
        k0 = pl.multiple_of(kt * tk, tk)
        ikt = ik_ref[pl.ds(k0, tk), :]
        sc = jnp.zeros((tq, tk), F32)
        for h in range(IDX_HEADS):
            sc = sc + jnp.maximum(_dot_nt(iqm[h], ikt), 0.0) * wts[h]
        bits = pltpu.bitcast(sc, I32)
        sign = bits >> 31
        key = (bits ^ (sign & 0x7FFFFFFF)) - sign
        keys_scr[:, pl.ds(k0, tk)] = jnp.where(cols(k0) < adm_end, key, INT_MIN)
        return carry

    lax.fori_loop(0, n_kt, score_tile, 0)

    kf = float(ksel)
    n_col = tk // LANES

    def select_rows(r0):
        rs = DSA_SEL_ROWS
        rows = pl.ds(r0, rs)

        def count(pred):
            def body(kt, acc):
                k0 = pl.multiple_of(kt * tk, tk)
                for c in range(n_col):
                    kk = keys_scr[rows, pl.ds(k0 + c * LANES, LANES)]
                    acc = acc + jnp.where(pred(kk, k0 + c * LANES), 1.0, 0.0)
                return acc
            acc = lax.fori_loop(0, n_kt, body, jnp.zeros((rs, LANES), F32))
            return jnp.sum(acc, axis=1, keepdims=True)

        def count_ge(cand):
            cb = jnp.broadcast_to(cand, (rs, LANES))
            return count(lambda kk, c0: kk >= cb)

        imin = jnp.full((rs, 1), INT_MIN, I32)
        done0 = jnp.where(count_ge(imin + 1) <= kf, 1.0, 0.0)

        def cond(st):
            it, _, _, done = st
            return (it < 32) & (jnp.min(done) == 0.0)

        def step(st):
            it, t, thr, done = st
            cand = t + jnp.left_shift(jnp.int32(1), 31 - it)
            c = count_ge(cand)
            t = jnp.where(c >= kf, cand, t)
            hit = (c == kf) & (done == 0.0)
            thr = jnp.where(hit, cand - 1, thr)
            done = jnp.where(hit, 1.0, done)
            return it + 1, t, thr, done

        _, t, thr, done = lax.while_loop(cond, step, (jnp.int32(0), imin, imin, done0))
        thr_scr[rows, :] = thr
        j_scr[rows, :] = jnp.full((rs, 1), -1, I32)

        @pl.when(jnp.min(done) == 0.0)
        def _():
            need = kf - count_ge(t + 1)
            tb = jnp.broadcast_to(t, (rs, LANES))
            lane = lax.broadcasted_iota(I32, (rs, LANES), 1)

            def idx_step(it, x):
                cand = x | jnp.left_shift(jnp.int32(1), 13 - it)
                cb = jnp.broadcast_to(cand, (rs, LANES))
                cnt = count(lambda kk, c0: (kk == tb) & (c0 + lane < cb))
                return jnp.where(cnt < need, cand, x)

            x = lax.fori_loop(0, 14, idx_step, jnp.zeros((rs, 1), I32))
            thr_scr[rows, :] = jnp.where(done == 0.0, t, thr)
            j_scr[rows, :] = jnp.where(done == 0.0, x, -1)

    for r in range(tq // DSA_SEL_ROWS):
        select_rows(r * DSA_SEL_ROWS)

    thr = thr_scr[...]
    jmax = j_scr[...]

    def write_tile(kt, carry):
        k0 = pl.multiple_of(kt * tk, tk)
        kk = keys_scr[:, pl.ds(k0, tk)]
        sel = (kk > thr) | ((kk == thr) & (cols(k0) <= jmax))
        bias_ref[0, :, pl.ds(k0, tk)] = jnp.where(sel, 0.0, NEG).astype(BF16)
        return carry

    lax.fori_loop(0, n_kt, write_tile, 0)

    def fill_tile(kt, carry):
        k0 = pl.multiple_of(kt * tk, tk)
        bias_ref[0, :, pl.ds(k0, tk)] = jnp.full((tq, tk), NEG, BF16)
        return carry

    lax.fori_loop(n_kt, seq // tk, fill_tile, 0)


def _dsa_index(iq, ik4, small, bsz, seq, tq, tk):
    ksel = min(DSA_TOPK_MAX, seq // 4)
    nq = seq // tq
    kern = functools.partial(_dsa_index_kernel, tq=tq, tk=tk, seq=seq, ksel=ksel)
    return pl.pallas_call(
        kern, grid=(bsz, nq),
        in_specs=[pl.BlockSpec((tq, 256), lambda b, i: (b * nq + i, 0)),
                  pl.BlockSpec((seq, 256), lambda b, i: (b, 0)),
                  pl.BlockSpec((tq, LANES), lambda b, i: (b * nq + i, 0))],
        out_specs=pl.BlockSpec((1, tq, seq), lambda b, i: (b, i, 0)),
        out_shape=jax.ShapeDtypeStruct((bsz, seq, seq), BF16),
        scratch_shapes=[pltpu.VMEM((tq, seq), I32), pltpu.VMEM((tq, 1), I32), pltpu.VMEM((tq, 1), I32)],
        compiler_params=_params("parallel", "parallel"))(iq, ik4, small)


def _flash_kernel(*refs, tq, tk, use_mask):
    if use_mask:
        q_ref, k_ref, v_ref, mask_ref, o_ref, m_scr, acc_scr, mbuf, sem = refs
        nb_ref = None
    else:
        q_ref, k_ref, v_ref, nb_ref, o_ref, m_scr, acc_scr = refs
        mask_ref = None
    b = pl.program_id(0)
    i = pl.program_id(2)
    q = q_ref[...]
    head0 = lax.broadcasted_iota(I32, (tq, LANES), 1) < HEAD_DIM
    head0k = lax.broadcasted_iota(I32, (tk, LANES), 1) < HEAD_DIM
    qm = [jnp.where(head0, q, jnp.zeros_like(q)), jnp.where(head0, jnp.zeros_like(q), q)]
    m_scr[...] = jnp.full(m_scr.shape, NEG, F32)
    acc_scr[...] = jnp.zeros(acc_scr.shape, F32)
    n_col = tk // LANES
    n_sub = tq // tk
    n_off = i * n_sub

    def mask_dma(kt, slot, r0):
        return pltpu.make_async_copy(
            mask_ref.at[b, pl.ds(i * tq + r0, tq - r0), pl.ds(pl.multiple_of(kt * tk, tk), tk)],
            mbuf.at[slot, pl.ds(r0, tq - r0), :], sem.at[slot])

    def tile(kt, diag):
        r0 = 0 if diag is None else diag * tk
        nr = tq - r0
        k0 = pl.multiple_of(kt * tk, tk)
        kt_ = k_ref[pl.ds(k0, tk), :]
        vt = v_ref[pl.ds(k0, tk), :]
        one = jnp.ones_like(vt)
        vx = [jnp.where(head0k, vt, one), jnp.where(head0k, one, vt)]
        if use_mask:
            slot = kt & 1
            if diag is None:
                mask_dma(kt + 1, 1 - slot, 0).start()
            elif diag + 1 < n_sub:
                mask_dma(kt + 1, 1 - slot, (diag + 1) * tk).start()
            mask_dma(kt, slot, r0).wait()
            extra = mbuf[slot, pl.ds(r0, nr), :].astype(F32)
        ss = []
        for j in range(2):
            s = _dot_nt(qm[j][r0:, :], kt_)
            if use_mask:
                s = s + extra
            else:
                s = s + nb_ref[0, 0, j:j + 1, pl.ds(k0, tk)]
                if diag is not None:
                    rr = lax.broadcasted_iota(I32, (nr, tk), 0)
                    cc = lax.broadcasted_iota(I32, (nr, tk), 1)
                    s = jnp.where(cc <= rr, s, NEG)
            ss.append(s)
        ps = []
        for j in range(2):
            sc = [ss[j][:, c * LANES:(c + 1) * LANES] for c in range(n_col)]
            part = sc[0]
            for c in range(1, n_col):
                part = jnp.maximum(part, sc[c])
            m_old = m_scr[j, r0:, :]
            m_new = jnp.maximum(m_old, jnp.max(part, axis=1, keepdims=True))
            alpha = jnp.exp2(m_old - m_new)
            ps.append((alpha, jnp.concatenate([jnp.exp2(c_ - m_new) for c_ in sc], axis=1).astype(BF16)))
            m_scr[j, r0:, :] = m_new
        for j in range(2):
            alpha, p = ps[j]
            acc_scr[j, r0:, :] = alpha * acc_scr[j, r0:, :] + jnp.dot(p, vx[j], preferred_element_type=F32)

    def body(kt, carry):
        tile(kt, None)
        return carry

    if use_mask:
        mask_dma(0, 0, 0).start()
    lax.fori_loop(0, n_off, body, 0)
    for d in range(n_sub):
        tile(n_off + d, d)

    a0 = acc_scr[0]
    a1 = acc_scr[1]
    o0 = a0 / pltpu.roll(a0, HEAD_DIM, axis=1)
    o1 = a1 / pltpu.roll(a1, HEAD_DIM, axis=1)
    o_ref[...] = jnp.where(head0, o0, o1).astype(o_ref.dtype)


def _flash(q, k, v, bsz, seq, tq, tk, nb=None, mask=None):
    nq = seq // tq
    n_hp = q.shape[1] // LANES
    use_mask = mask is not None
    in_specs = [pl.BlockSpec((tq, LANES), lambda b, h, i: (b * nq + i, h)),
                pl.BlockSpec((seq, LANES), lambda b, h, i: (b, h)),
                pl.BlockSpec((seq, LANES), lambda b, h, i: (b, h))]
    scratch = [pltpu.VMEM((2, tq, LANES), F32), pltpu.VMEM((2, tq, LANES), F32)]
    if use_mask:
        in_specs.append(pl.BlockSpec(memory_space=pl.ANY))
        scratch += [pltpu.VMEM((2, tq, tk), BF16), pltpu.SemaphoreType.DMA((2,))]
        extra = mask
    else:
        in_specs.append(pl.BlockSpec((1, 1, 2, seq), lambda b, h, i: (b, h, 0, 0)))
        extra = nb
    kern = functools.partial(_flash_kernel, tq=tq, tk=tk, use_mask=use_mask)
    return pl.pallas_call(
        kern, grid=(bsz, n_hp, nq), in_specs=in_specs,
        out_specs=pl.BlockSpec((tq, LANES), lambda b, h, i: (b * nq + i, h)),
        out_shape=jax.ShapeDtypeStruct(q.shape, BF16),
        scratch_shapes=scratch,
        compiler_params=_params("parallel", "parallel", "parallel"))(q, k, v, extra)


def _hgrn_kernel(q_ref, zf_ref, v_ref, g_ref, lb_ref, ng_ref, o_ref, st_scr, oi_scr, *, tt):
    r = HGRN_SUB
    half = 256

    @pl.when(pl.program_id(1) == 0)
    def _():
        st_scr[...] = jnp.zeros(st_scr.shape, F32)

    q = q_ref[...]
    zf = zf_ref[...]
    v = v_ref[...]
    ls = jnp.minimum(zf, 0.0) - jnp.log1p(jnp.exp(-jnp.abs(zf)))
    a = lb_ref[0:1, :]
    y = lb_ref[1:2, :] + ls
    lf = jnp.maximum(a, y) + jnp.log1p(jnp.exp(-jnp.abs(a - y)))
    kk = lb_ref[2:3, :] * (1.0 / (1.0 + jnp.exp(zf)))

    rin = lax.broadcasted_iota(I32, (tt, 1), 0) & (r - 1)
    b = lf
    sh = 1
    while sh < r:
        b = b + jnp.where(rin >= sh, pltpu.roll(b, sh, axis=0), 0.0)
        sh *= 2

    rr = lax.broadcasted_iota(I32, (half, half), 0) // HEAD_DIM
    cc = lax.broadcasted_iota(I32, (half, half), 1) // HEAD_DIM
    same_head = rr == cc
    ones_bd = jnp.where(same_head, 1.0, 0.0).astype(BF16)
    bd_mask = jnp.where(same_head, 1.0, 0.0).astype(F32)

    o = jnp.zeros((tt, 2 * half), F32)
    for off in range(r):
        if off == 0:
            e = q * kk
            vs = v
        else:
            e = q * pltpu.roll(kk, off, axis=0) * jnp.exp(b - pltpu.roll(b, off, axis=0))
            e = jnp.where(rin >= off, e, 0.0)
            vs = pltpu.roll(v, off, axis=0)
        eb = e.astype(BF16)
        sc = jnp.concatenate(
            [jnp.dot(eb[:, 0:half], ones_bd, preferred_element_type=F32),
             jnp.dot(eb[:, half:], ones_bd, preferred_element_type=F32)], axis=1)
        o = o + sc * vs

    qe = (q * jnp.exp(b)).astype(BF16)
    vb = v.astype(BF16)
    for c in range(tt // r):
        r0 = c * r
        bl = b[r0 + r - 1:r0 + r, :]
        kd = (kk[r0:r0 + r, :] * jnp.exp(bl - b[r0:r0 + r, :])).astype(BF16)
        dec = jnp.exp(bl)
        for h2 in range(2):
            lo = h2 * half
            st = st_scr[h2]
            oi_scr[r0:r0 + r, lo:lo + half] = _dot_nt(qe[r0:r0 + r, lo:lo + half], st.astype(BF16))
            upd = _dot_tn(vb[r0:r0 + r, lo:lo + half], kd[:, lo:lo + half])
            st_scr[h2] = st * dec[:, lo:lo + half] + upd * bd_mask

    o = o + oi_scr[...]
    o2 = o * o
    ones_f = bd_mask
    ms = jnp.concatenate(
        [jnp.dot(o2[:, 0:half], ones_f, preferred_element_type=F32, precision=lax.Precision.HIGHEST),
         jnp.dot(o2[:, half:], ones_f, preferred_element_type=F32, precision=lax.Precision.HIGHEST)],
        axis=1) * (1.0 / HEAD_DIM)
    on = o * lax.rsqrt(ms + RMS_EPS) * ng_ref[...] * _sigmoid(g_ref[...])
    o_ref[...] = on.astype(o_ref.dtype)


def _hgrn(zc, lbp, ng, bsz, seq, tt):
    n = zc.shape[0]
    nt = seq // tt
    col = lambda cidx: pl.BlockSpec((tt, 512), lambda b, j: (b * nt + j, cidx))
    kern = functools.partial(_hgrn_kernel, tt=tt)
    return pl.pallas_call(
        kern, grid=(bsz, nt),
        in_specs=[col(0), col(1), col(2), col(3),
                  pl.BlockSpec((3, 512), lambda b, j: (0, 0)),
                  pl.BlockSpec((1, 512), lambda b, j: (0, 0))],
        out_specs=pl.BlockSpec((tt, 512), lambda b, j: (b * nt + j, 0)),
        out_shape=jax.ShapeDtypeStruct((n, 512), BF16),
        scratch_shapes=[pltpu.VMEM((2, 256, 256), F32), pltpu.VMEM((tt, 512), F32)],
        compiler_params=_params("parallel", "arbitrary"))(zc, zc, zc, zc, lbp, ng)


def _layer_norm(y, g, b):
    mu = jnp.mean(y, axis=-1, keepdims=True)
    d = y - mu
    var = jnp.mean(d * d, axis=-1, keepdims=True)
    return d * lax.rsqrt(var + LN_EPS) * g + b


def _merge_kernel(oa_ref, ob_ref, oc_ref, gt_ref, x_ref, wpa_ref, wpb_ref, wpc_ref, wout_ref,
                  g_ref, b_ref, wr_ref, br_ref, x1_ref, xb_ref, ti_ref, gate_ref, *, alpha):
    d = x_ref.shape[1]
    pa = jnp.dot(oa_ref[...], wpa_ref[...], preferred_element_type=F32)
    pb = jnp.dot(ob_ref[...], wpb_ref[...], preferred_element_type=F32)
    pc = jnp.dot(oc_ref[...], wpc_ref[...], preferred_element_type=F32)
    merged = (_sigmoid(gt_ref[:, 0:d]) * pa + _sigmoid(gt_ref[:, d:2 * d]) * pb
              + _sigmoid(gt_ref[:, 2 * d:3 * d]) * pc)
    mix = jnp.dot(merged.astype(BF16), wout_ref[...], preferred_element_type=F32)
    x1 = _layer_norm(alpha * x_ref[...] + mix, g_ref[...], b_ref[...])
    x1_ref[...] = x1
    xb_ref[...] = x1.astype(BF16)

    logits = jnp.dot(x1, wr_ref[...], preferred_element_type=F32,
                     precision=lax.Precision.HIGHEST) + br_ref[...]
    lane = lax.broadcasted_iota(I32, logits.shape, 1)
    topi = jnp.zeros(logits.shape, I32)
    topv = jnp.full(logits.shape, NEG, F32)
    for k in range(TOP_K):
        m = jnp.max(logits, axis=1, keepdims=True)
        idx = jnp.min(jnp.where(logits == m, lane, LANES), axis=1, keepdims=True)
        topi = jnp.where(lane == k, idx, topi)
        topv = jnp.where(lane == k, m, topv)
        logits = jnp.where(lane == idx, -jnp.inf, logits)
    e = jnp.where(lane < TOP_K, jnp.exp(topv - jnp.max(topv, axis=1, keepdims=True)), 0.0)
    ti_ref[...] = topi
    gate_ref[...] = e / jnp.sum(e, axis=1, keepdims=True)


def _merge(oa, ob, oc, gt, x2, wpa, wpb, wpc, wout, g, b, wr, br, alpha, tm):
    n, d = x2.shape
    row = lambda w: pl.BlockSpec((tm, w), lambda i: (i, 0))
    full = lambda a: pl.BlockSpec(a.shape, lambda i: (0,) * a.ndim)
    kern = functools.partial(_merge_kernel, alpha=alpha)
    return pl.pallas_call(
        kern, grid=(n // tm,),
        in_specs=[row(512), row(512), row(512), row(3 * d), row(d),
                  full(wpa), full(wpb), full(wpc), full(wout), full(g), full(b), full(wr), full(br)],
        out_specs=[row(d), row(d), row(LANES), row(LANES)],
        out_shape=[jax.ShapeDtypeStruct((n, d), F32), jax.ShapeDtypeStruct((n, d), BF16),
                   jax.ShapeDtypeStruct((n, LANES), I32), jax.ShapeDtypeStruct((n, LANES), F32)],
        compiler_params=_params("parallel"))(oa, ob, oc, gt, x2, wpa, wpb, wpc, wout, g, b, wr, br)


def _moe_kernel(be_ref, nu_ref, xs_ref, wgu_ref, bgu_ref, wd_ref, bd_ref, y_ref):
    i = pl.program_id(0)
    de = wd_ref.shape[1]

    @pl.when(i < nu_ref[0])
    def _():
        gu = jnp.dot(xs_ref[...], wgu_ref[0], preferred_element_type=F32) + bgu_ref[0]
        gate = jnp.minimum(gu[:, 0:de], SWIGLU_LIMIT)
        up = jnp.clip(gu[:, de:], -SWIGLU_LIMIT, SWIGLU_LIMIT)
        act = gate * _sigmoid(SWIGLU_ALPHA * gate) * (up + 1.0)
        y_ref[...] = jnp.dot(act.astype(BF16), wd_ref[0], preferred_element_type=F32) + bd_ref[0]

    @pl.when(i >= nu_ref[0])
    def _():
        y_ref[...] = jnp.zeros(y_ref.shape, F32)


def _moe_experts(blk_e, n_used, xs, wgu, bgu, wd, bd):
    n_slots, d = xs.shape
    n_blocks = n_slots // MOE_BLOCK
    e, _, n2 = wgu.shape
    de = wd.shape[1]
    grid_spec = pltpu.PrefetchScalarGridSpec(
        num_scalar_prefetch=2, grid=(n_blocks,),
        in_specs=[pl.BlockSpec((MOE_BLOCK, d), lambda i, be, nu: (i, 0)),
                  pl.BlockSpec((1, d, n2), lambda i, be, nu: (be[i], 0, 0)),
                  pl.BlockSpec((1, 1, n2), lambda i, be, nu: (be[i], 0, 0)),
                  pl.BlockSpec((1, de, d), lambda i, be, nu: (be[i], 0, 0)),
                  pl.BlockSpec((1, 1, d), lambda i, be, nu: (be[i], 0, 0))],
        out_specs=pl.BlockSpec((MOE_BLOCK, d), lambda i, be, nu: (i, 0)))
    return pl.pallas_call(
        _moe_kernel, grid_spec=grid_spec,
        out_shape=jax.ShapeDtypeStruct((n_slots, d), F32),
        compiler_params=_params("arbitrary"))(
            blk_e, n_used, xs, wgu, bgu.reshape(e, 1, n2), wd, bd.reshape(e, 1, d))


def _combine_kernel(yg_ref, gate_ref, x1_ref, g_ref, b_ref, o_ref, *, alpha):
    gates = gate_ref[...]
    acc = alpha * x1_ref[...]
    for k in range(TOP_K):
        acc = acc + gates[:, k:k + 1] * yg_ref[k]
    o_ref[...] = _layer_norm(acc, g_ref[...], b_ref[...])


def _combine(yg, gates, x1, g, b, alpha, tm):
    n, d = x1.shape
    kern = functools.partial(_combine_kernel, alpha=alpha)
    return pl.pallas_call(
        kern, grid=(n // tm,),
        in_specs=[pl.BlockSpec((TOP_K, tm, d), lambda i: (0, i, 0)),
                  pl.BlockSpec((tm, LANES), lambda i: (i, 0)),
                  pl.BlockSpec((tm, d), lambda i: (i, 0)),
                  pl.BlockSpec((1, d), lambda i: (0, 0)),
                  pl.BlockSpec((1, d), lambda i: (0, 0))],
        out_specs=pl.BlockSpec((tm, d), lambda i: (i, 0)),
        out_shape=jax.ShapeDtypeStruct((n, d), F32),
        compiler_params=_params("parallel"))(yg, gates, x1, g, b)


def _rope_tables(seq, width):
    half = HEAD_DIM // 2
    inv = ROPE_THETA ** (-jnp.arange(half, dtype=F32) / half)
    ang = jnp.arange(seq, dtype=F32)[:, None] * inv[None, :]
    cos = jnp.concatenate([jnp.cos(ang), jnp.cos(ang)], axis=1)
    sin = jnp.concatenate([-jnp.sin(ang), jnp.sin(ang)], axis=1)
    reps = width // HEAD_DIM
    return jnp.tile(cos, (1, reps)), jnp.tile(sin, (1, reps))


def _split_w_in(w):
    seg = lambda a, b: w[:, _OFF[a]:_OFF[b]]
    w_dsa = seg(0, 3)
    small = jnp.concatenate([seg(5, 6), seg(9, 10), jnp.zeros((w.shape[0], LANES - 12), w.dtype)], axis=1)
    w_idx = jnp.concatenate([seg(3, 4)] + [seg(4, 5)] * IDX_HEADS + [small], axis=1)
    w_fox = seg(6, 9)
    w_hgrn = seg(10, 14)
    w_gate = seg(14, 15)
    return [t.astype(BF16) for t in (w_dsa, w_idx, w_fox, w_hgrn, w_gate)]


def _route(top_idx, n_tok):
    n_assign = n_tok * TOP_K
    flat_e = top_idx.reshape(-1)
    order = jnp.argsort(flat_e).astype(I32)
    sorted_e = flat_e[order]
    counts = jnp.bincount(flat_e, length=N_EXPERTS).astype(I32)
    padded = (counts + MOE_BLOCK - 1) // MOE_BLOCK * MOE_BLOCK
    pad_end = jnp.cumsum(padded)
    pad_start = pad_end - padded
    start = jnp.cumsum(counts) - counts
    rank = jnp.arange(n_assign, dtype=I32) - start[sorted_e]
    dest = pad_start[sorted_e] + rank
    n_blocks = -(-(n_assign + N_EXPERTS * (MOE_BLOCK - 1)) // MOE_BLOCK)
    slot_tok = jnp.zeros((n_blocks * MOE_BLOCK,), I32).at[dest].set(order // TOP_K)
    blk_e = jnp.searchsorted(pad_end, jnp.arange(n_blocks, dtype=I32) * MOE_BLOCK, side='right')
    blk_e = jnp.minimum(blk_e, N_EXPERTS - 1).astype(I32)
    n_used = (pad_end[-1] // MOE_BLOCK).astype(I32).reshape(1)
    slot_of = jnp.zeros((n_assign,), I32).at[order].set(dest)
    return slot_tok, blk_e, n_used, slot_of.reshape(n_tok, TOP_K).T


def _layer(x2, bsz, seq, alpha, w_in, b_fox, lb, ng, wpa, wpb, wpc, wout, g1, b1,
           wr, br, wgu, bgu, wd, bd, g2, b2, cos, sin):
    n, d = x2.shape
    tm = min(512, seq)
    w_dsa, w_idx, w_fox, w_hgrn, w_gate = _split_w_in(w_in)
    nst = seq // tm
    tab = lambda arr: (arr, pl.BlockSpec((tm, 512), lambda i: (i % nst, 0)))

    aq, ak, av = _proj_call(_proj_dsa_kernel, x2, w_dsa, [tab(cos), tab(sin)],
                            [(512, BF16)] * 3, tm)
    iq, ik4, small = _proj_call(_proj_idx_kernel, x2, w_idx, [tab(cos), tab(sin)],
                                [(256, BF16), (256, BF16), (LANES, F32)], tm)
    fq, fk, fv = _proj_call(_proj_fox_kernel, x2, w_fox, [], [(512, BF16)] * 3, tm)
    (zc,) = _proj_call(_proj_plain_kernel, x2, w_hgrn, [], [(2048, F32)], tm)
    (gt,) = _proj_call(_proj_plain_kernel, x2, w_gate, [], [(3 * d, F32)], tm)

    tq_att, tk_att = min(FLASH_TQ, seq), min(FLASH_TK, seq)
    mask = _dsa_index(iq, ik4, small, bsz, seq, min(256, seq), min(512, seq))
    o_a = _flash(aq, ak, av, bsz, seq, tq_att, tk_att, mask=mask)

    ff_t = small[:, 4:12].reshape(bsz, seq, 8).transpose(0, 2, 1)
    nb = _fox_bias(ff_t, b_fox).reshape(bsz, 4, 2, seq)
    o_b = _flash(fq, fk, fv, bsz, seq, tq_att, tk_att, nb=nb)

    lbp = jnp.stack([jnp.log(lb), jnp.log1p(-lb), 1.0 - lb]).astype(F32)
    o_c = _hgrn(zc, lbp, ng.reshape(1, -1).astype(F32), bsz, seq, min(256, seq))

    wr_p = jnp.zeros((d, LANES), F32).at[:, :N_EXPERTS].set(wr.astype(F32))
    br_p = jnp.full((1, LANES), NEG, F32).at[0, :N_EXPERTS].set(br.astype(F32))
    x1, x1b, topi, gates = _merge(
        o_a, o_b, o_c, gt, x2, wpa.astype(BF16), wpb.astype(BF16), wpc.astype(BF16),
        wout.astype(BF16), g1.reshape(1, d), b1.reshape(1, d), wr_p, br_p, alpha, min(256, n))

    slot_tok, blk_e, n_used, slot_of = _route(topi[:, :TOP_K], n)
    xs = x1b[slot_tok]
    y_slots = _moe_experts(blk_e, n_used, xs, wgu.astype(BF16), bgu, wd.astype(BF16), bd)
    yg = y_slots[slot_of]
    return _combine(yg, gates, x1, g2.reshape(1, d), b2.reshape(1, d), alpha, min(256, n))


def kernel(x, w_in, b_fox_f, hgrn_lb_logits, hgrn_norm_g, w_branch_a, w_branch_b, w_branch_c, w_out, ln1_g, ln1_b, w_router, b_router, w_gu, b_gu, w_down, b_down, ln2_g, ln2_b):
    bsz, seq, d = x.shape
    depth = w_in.shape[0]
    alpha = (2 * depth) ** 0.25
    p = jax.nn.softmax(hgrn_lb_logits.astype(F32), axis=0)
    lbs = jnp.cumsum(p, axis=0)
    lbs = lbs - lbs[0]
    cos, sin = _rope_tables(seq, 512)
    x2 = x.reshape(bsz * seq, d)
    for l in range(depth):
        x2 = _layer(x2, bsz, seq, alpha, w_in[l], b_fox_f[l], lbs[l], hgrn_norm_g[l],
                    w_branch_a[l], w_branch_b[l], w_branch_c[l], w_out[l], ln1_g[l], ln1_b[l],
                    w_router[l], b_router[l], w_gu[l], b_gu[l], w_down[l], b_down[l],
                    ln2_g[l], ln2_b[l], cos, sin)
    return x2.reshape(bsz, seq, d)
```

```python
import functools
import math

import jax
import jax.numpy as jnp
import numpy as np
from jax import lax
from jax.experimental import pallas as pl
from jax.experimental.pallas import tpu as pltpu

F32 = jnp.float32
BF16 = jnp.bfloat16
I32 = jnp.int32

CHUNK = 64
HEAD_DIM = 64
ROPE_THETA = 10000.0
DSA_TOPK_MAX = 256
IDX_HEADS = 4
N_EXPERTS = 32
TOP_K = 4
SWIGLU_LIMIT = 7.0
SWIGLU_ALPHA = 1.702
MOE_BLOCK = 256
LN_EPS = 1e-5
RMS_EPS = 1e-6

LOG2E = 1.4426950408889634
NEG = -1e30
INT_MIN = -2147483648
LANES = 128
FLASH_TQ = 2048
FLASH_TK = 512
HGRN_SUB = 16
VMEM_LIMIT = 56 * 1024 * 1024

IN_SPLITS = (512, 512, 512, 256, 64, 4, 512, 512, 512, 8, 512, 512, 512, 512, 3072)
_OFF = tuple(int(v) for v in np.cumsum((0,) + IN_SPLITS))


def _params(*sem):
    return pltpu.CompilerParams(dimension_semantics=sem, vmem_limit_bytes=VMEM_LIMIT)


def _dot_nt(a, b):
    return lax.dot_general(a, b, (((1,), (1,)), ((), ())), preferred_element_type=F32)


def _dot_tn(a, b):
    return lax.dot_general(a, b, (((0,), (0,)), ((), ())), preferred_element_type=F32)


def _sigmoid(t):
    return 1.0 / (1.0 + jnp.exp(-t))


def _rope_tile(t, cos, sin):
    w = t.shape[1]
    lane_d = lax.broadcasted_iota(I32, t.shape, 1) & (HEAD_DIM - 1)
    partner = jnp.where(lane_d < HEAD_DIM // 2,
                        pltpu.roll(t, w - HEAD_DIM // 2, axis=1),
                        pltpu.roll(t, HEAD_DIM // 2, axis=1))
    return t * cos + partner * sin


def _proj_dsa_kernel(x_ref, w_ref, cos_ref, sin_ref, q_ref, k_ref, v_ref):
    z = jnp.dot(x_ref[...].astype(BF16), w_ref[...], preferred_element_type=F32)
    cos = cos_ref[...]
    sin = sin_ref[...]
    q_ref[...] = (_rope_tile(z[:, 0:512], cos, sin) * (HEAD_DIM ** -0.5 * LOG2E)).astype(BF16)
    k_ref[...] = _rope_tile(z[:, 512:1024], cos, sin).astype(BF16)
    v_ref[...] = z[:, 1024:1536].astype(BF16)


def _proj_idx_kernel(x_ref, w_ref, cos_ref, sin_ref, iq_ref, ik_ref, sm_ref):
    z = jnp.dot(x_ref[...].astype(BF16), w_ref[...], preferred_element_type=F32)
    cos = cos_ref[...][:, 0:256]
    sin = sin_ref[...][:, 0:256]
    iq_ref[...] = (_rope_tile(z[:, 0:256], cos, sin) * (HEAD_DIM ** -0.5)).astype(BF16)
    ik_ref[...] = _rope_tile(z[:, 256:512], cos, sin).astype(BF16)
    sm_ref[...] = z[:, 512:640]


def _proj_fox_kernel(x_ref, w_ref, q_ref, k_ref, v_ref):
    z = jnp.dot(x_ref[...].astype(BF16), w_ref[...], preferred_element_type=F32)
    q_ref[...] = (z[:, 0:512] * (HEAD_DIM ** -0.5 * LOG2E)).astype(BF16)
    k_ref[...] = z[:, 512:1024].astype(BF16)
    v_ref[...] = z[:, 1024:1536].astype(BF16)


def _proj_plain_kernel(x_ref, w_ref, o_ref):
    o_ref[...] = jnp.dot(x_ref[...].astype(BF16), w_ref[...], preferred_element_type=F32)


def _proj_call(kernel_fn, x2, w, extra, outs, tm):
    n, d = x2.shape
    nout = w.shape[1]
    in_specs = [pl.BlockSpec((tm, d), lambda i: (i, 0)),
                pl.BlockSpec((d, nout), lambda i: (0, 0))]
    args = [x2, w]
    for arr, spec in extra:
        in_specs.append(spec)
        args.append(arr)
    out_shape = [jax.ShapeDtypeStruct((n, wdt), dt) for wdt, dt in outs]
    out_specs = [pl.BlockSpec((tm, wdt), lambda i: (i, 0)) for wdt, _ in outs]
    return pl.pallas_call(
        kernel_fn, grid=(n // tm,), in_specs=in_specs, out_specs=out_specs,
        out_shape=out_shape, compiler_params=_params("parallel"))(*args)


def _fox_bias_kernel(ff_ref, b_ref, nb_ref):
    s = ff_ref.shape[2]
    z = ff_ref[0] + b_ref[...]
    lf = jnp.minimum(z, 0.0) - jnp.log1p(jnp.exp(-jnp.abs(z)))
    r = lax.broadcasted_iota(I32, (LANES, LANES), 0)
    c = lax.broadcasted_iota(I32, (LANES, LANES), 1)
    upper = jnp.where(r <= c, 1.0, 0.0).astype(F32)
    carry = jnp.zeros((lf.shape[0], 1), F32)
    for t in range(s // LANES):
        blk = lf[:, t * LANES:(t + 1) * LANES]
        cs = jnp.dot(blk, upper, preferred_element_type=F32,
                     precision=lax.Precision.HIGHEST) + carry
        nb_ref[0, :, t * LANES:(t + 1) * LANES] = cs * (-LOG2E)
        carry = cs[:, LANES - 1:LANES]


def _fox_bias(ff_t, b_fox):
    bsz, h, s = ff_t.shape
    return pl.pallas_call(
        _fox_bias_kernel, grid=(bsz,),
        in_specs=[pl.BlockSpec((1, h, s), lambda b: (b, 0, 0)),
                  pl.BlockSpec((h, 1), lambda b: (0, 0))],
        out_specs=pl.BlockSpec((1, h, s), lambda b: (b, 0, 0)),
        out_shape=jax.ShapeDtypeStruct((bsz, h, s), F32),
        compiler_params=_params("parallel"))(ff_t, b_fox.reshape(h, 1).astype(F32))


def _dsa_index_kernel(iq_ref, ik_ref, iwt_ref, bias_ref, keys_scr, thr_scr, j_scr, *, tq, tk, seq, ksel):
    i = pl.program_id(1)
    n_kt = ((i + 1) * tq + tk - 1) // tk
    q_pos = i * tq + lax.broadcasted_iota(I32, (1, tq), 1)
    adm_end = (q_pos // CHUNK + 1) * CHUNK

    iq = iq_ref[...]
    lane = lax.broadcasted_iota(I32, iq.shape, 1)
    iqm = [jnp.where((lane >= HEAD_DIM * h) & (lane < HEAD_DIM * (h + 1)), iq, jnp.zeros_like(iq))
           for h in range(IDX_HEADS)]
    wts = [iwt_ref[0, h:h + 1, :] * (IDX_HEADS ** -0.5) for h in range(IDX_HEADS)]

    def key_idx(k0):
        return k0 + lax.broadcasted_iota(I32, (tk, 1), 0)

    def score_tile(kt, carry):
        k0 = pl.multiple_of(kt * tk, tk)
        ikt = ik_ref[pl.ds(k0, tk), :]
        sc = jnp.zeros((tk, tq), F32)
        for h in range(IDX_HEADS):
            sc = sc + jnp.maximum(_dot_nt(ikt, iqm[h]), 0.0) * wts[h]
        bits = pltpu.bitcast(sc, I32)
        sign = bits >> 31
        key = (bits ^ (sign & 0x7FFFFFFF)) - sign
        keys_scr[pl.ds(k0, tk), :] = jnp.where(key_idx(k0) < adm_end, key, INT_MIN)
        return carry

    lax.fori_loop(0, n_kt, score_tile, 0)

    kf = float(ksel)
    n_acc = 4
    grp = 8 * n_acc

    def count(pred):
        def body(kt, accs):
            k0 = pl.multiple_of(kt * tk, tk)
            accs = list(accs)
            for g in range(tk // grp):
                kk = keys_scr[pl.ds(k0 + g * grp, grp), :]
                hit = jnp.where(pred(kk, k0 + g * grp), 1.0, 0.0)
                for a in range(n_acc):
                    accs[a] = accs[a] + hit[a * 8:(a + 1) * 8, :]
            return tuple(accs)
        accs = lax.fori_loop(0, n_kt, body, tuple(jnp.zeros((8, tq), F32) for _ in range(n_acc)))
        tot = accs[0]
        for a in range(1, n_acc):
            tot = tot + accs[a]
        return jnp.sum(tot, axis=0, keepdims=True)

    def count_ge(cand):
        return count(lambda kk, k0: kk >= cand)

    imin = jnp.full((1, tq), INT_MIN, I32)
    done0 = jnp.where(count_ge(imin + 1) <= kf, 1.0, 0.0)

    def cond(st):
        it, _, _, done = st
        return (it < 32) & (jnp.min(done) == 0.0)

    def step(st):
        it, t, thr, done = st
        cand = t + jnp.left_shift(jnp.int32(1), 31 - it)
        c = count_ge(cand)
        t = jnp.where(c >= kf, cand, t)
        hit = (c == kf) & (done == 0.0)
        thr = jnp.where(hit, cand - 1, thr)
        done = jnp.where(hit, 1.0, done)
        return it + 1, t, thr, done

    _, t, thr, done = lax.while_loop(cond, step, (jnp.int32(0), imin, imin, done0))
    thr_scr[...] = thr
    j_scr[...] = jnp.full((1, tq), -1, I32)

    @pl.when(jnp.min(done) == 0.0)
    def _():
        need = kf - count_ge(t + 1)

        def idx_step(it, x):
            cand = x | jnp.left_shift(jnp.int32(1), 13 - it)
            rows = lax.broadcasted_iota(I32, (grp, 1), 0)
            cnt = count(lambda kk, k0: (kk == t) & (k0 + rows < cand))
            return jnp.where(cnt < need, cand, x)

        x = lax.fori_loop(0, 14, idx_step, jnp.zeros((1, tq), I32))
        thr_scr[...] = jnp.where(done == 0.0, t, thr)
        j_scr[...] = jnp.where(done == 0.0, x, -1)

    thr = thr_scr[...]
    jmax = j_scr[...]

    def write_tile(kt, carry):
        k0 = pl.multiple_of(kt * tk, tk)
        kk = keys_scr[pl.ds(k0, tk), :]
        sel = (kk > thr) | ((kk == thr) & (key_idx(k0) <= jmax))
        bias_ref[0, :, pl.ds(k0, tk)] = jnp.where(sel, 0.0, NEG).T.astype(BF16)
        return carry

    lax.fori_loop(0, n_kt, write_tile, 0)

    def fill_tile(kt, carry):
        k0 = pl.multiple_of(kt * tk, tk)
        bias_ref[0, :, pl.ds(k0, tk)] = jnp.full((tq, tk), NEG, BF16)
        return carry

    lax.fori_loop(n_kt, seq // tk, fill_tile, 0)


def _dsa_index(iq, ik4, iw_t, bsz, seq, tq, tk):
    ksel = min(DSA_TOPK_MAX, seq // 4)
    nq = seq // tq
    kern = functools.partial(_dsa_index_kernel, tq=tq, tk=tk, seq=seq, ksel=ksel)
    return pl.pallas_call(
        kern, grid=(bsz, nq),
        in_specs=[pl.BlockSpec((tq, 256), lambda b, i: (b * nq + i, 0)),
                  pl.BlockSpec((seq, 256), lambda b, i: (b, 0)),
                  pl.BlockSpec((1, 8, tq), lambda b, i: (b, 0, i))],
        out_specs=pl.BlockSpec((1, tq, seq), lambda b, i: (b, i, 0)),
        out_shape=jax.ShapeDtypeStruct((bsz, seq, seq), BF16),
        scratch_shapes=[pltpu.VMEM((seq, tq), I32), pltpu.VMEM((1, tq), I32), pltpu.VMEM((1, tq), I32)],
        compiler_params=_params("parallel", "parallel"))(iq, ik4, iw_t)


def _flash_kernel(*refs, tq, tk, use_mask):
    if use_mask:
        q_ref, k_ref, v_ref, mask_ref, o_ref, m_scr, acc_scr, mbuf, sem = refs
        nb_ref = None
    else:
        q_ref, k_ref, v_ref, nb_ref, o_ref, m_scr, acc_scr = refs
        mask_ref = None
    b = pl.program_id(0)
    i = pl.program_id(2)
    q = q_ref[...]
    head0 = lax.broadcasted_iota(I32, (tq, LANES), 1) < HEAD_DIM
    head0k = lax.broadcasted_iota(I32, (tk, LANES), 1) < HEAD_DIM
    qm = [jnp.where(head0, q, jnp.zeros_like(q)), jnp.where(head0, jnp.zeros_like(q), q)]
    m_scr[...] = jnp.full(m_scr.shape, NEG, F32)
    acc_scr[...] = jnp.zeros(acc_scr.shape, F32)
    n_col = tk // LANES
    n_sub = tq // tk
    n_off = i * n_sub

    def mask_dma(kt, slot, r0):
        return pltpu.make_async_copy(
            mask_ref.at[b, pl.ds(i * tq + r0, tq - r0), pl.ds(pl.multiple_of(kt * tk, tk), tk)],
            mbuf.at[slot, pl.ds(r0, tq - r0), :], sem.at[slot])

    def tile(kt, diag):
        r0 = 0 if diag is None else diag * tk
        nr = tq - r0
        k0 = pl.multiple_of(kt * tk, tk)
        kt_ = k_ref[pl.ds(k0, tk), :]
        vt = v_ref[pl.ds(k0, tk), :]
        one = jnp.ones_like(vt)
        vx = [jnp.where(head0k, vt, one), jnp.where(head0k, one, vt)]
        if use_mask:
            slot = kt & 1
            if diag is None:
                mask_dma(kt + 1, 1 - slot, 0).start()
            elif diag + 1 < n_sub:
                mask_dma(kt + 1, 1 - slot, (diag + 1) * tk).start()
            mask_dma(kt, slot, r0).wait()
            extra = mbuf[slot, pl.ds(r0, nr), :].astype(F32)
        ss = []
        for j in range(2):
            s = _dot_nt(qm[j][r0:, :], kt_)
            if use_mask:
                s = s + extra
            else:
                s = s + nb_ref[0, 0, j:j + 1, pl.ds(k0, tk)]
                if diag is not None:
                    rr = lax.broadcasted_iota(I32, (nr, tk), 0)
                    cc = lax.broadcasted_iota(I32, (nr, tk), 1)
                    s = jnp.where(cc <= rr, s, NEG)
            ss.append(s)
        ps = []
        for j in range(2):
            sc = [ss[j][:, c * LANES:(c + 1) * LANES] for c in range(n_col)]
            part = sc[0]
            for c in range(1, n_col):
                part = jnp.maximum(part, sc[c])
            m_old = m_scr[j, r0:, :]
            m_new = jnp.maximum(m_old, jnp.max(part, axis=1, keepdims=True))
            alpha = jnp.exp2(m_old - m_new)
            ps.append((alpha, jnp.concatenate([jnp.exp2(c_ - m_new) for c_ in sc], axis=1).astype(BF16)))
            m_scr[j, r0:, :] = m_new
        for j in range(2):
            alpha, p = ps[j]
            acc_scr[j, r0:, :] = alpha * acc_scr[j, r0:, :] + jnp.dot(p, vx[j], preferred_element_type=F32)

    def body(kt, carry):
        tile(kt, None)
        return carry

    if use_mask:
        mask_dma(0, 0, 0).start()
    lax.fori_loop(0, n_off, body, 0)
    for d in range(n_sub):
        tile(n_off + d, d)

    a0 = acc_scr[0]
    a1 = acc_scr[1]
    o0 = a0 / pltpu.roll(a0, HEAD_DIM, axis=1)
    o1 = a1 / pltpu.roll(a1, HEAD_DIM, axis=1)
    o_ref[...] = jnp.where(head0, o0, o1).astype(o_ref.dtype)


def _flash(q, k, v, bsz, seq, tq, tk, nb=None, mask=None):
    nq = seq // tq
    n_hp = q.shape[1] // LANES
    use_mask = mask is not None
    in_specs = [pl.BlockSpec((tq, LANES), lambda b, h, i: (b * nq + i, h)),
                pl.BlockSpec((seq, LANES), lambda b, h, i: (b, h)),
                pl.BlockSpec((seq, LANES), lambda b, h, i: (b, h))]
    scratch = [pltpu.VMEM((2, tq, LANES), F32), pltpu.VMEM((2, tq, LANES), F32)]
    if use_mask:
        in_specs.append(pl.BlockSpec(memory_space=pl.ANY))
        scratch += [pltpu.VMEM((2, tq, tk), BF16), pltpu.SemaphoreType.DMA((2,))]
        extra = mask
    else:
        in_specs.append(pl.BlockSpec((1, 1, 2, seq), lambda b, h, i: (b, h, 0, 0)))
        extra = nb
    kern = functools.partial(_flash_kernel, tq=tq, tk=tk, use_mask=use_mask)
    return pl.pallas_call(
        kern, grid=(bsz, n_hp, nq), in_specs=in_specs,
        out_specs=pl.BlockSpec((tq, LANES), lambda b, h, i: (b * nq + i, h)),
        out_shape=jax.ShapeDtypeStruct(q.shape, BF16),
        scratch_shapes=scratch,
        compiler_params=_params("parallel", "parallel", "parallel"))(q, k, v, extra)


def _hgrn_kernel(q_ref, zf_ref, v_ref, g_ref, lb_ref, ng_ref, o_ref, st_scr, oi_scr, *, tt):
    r = HGRN_SUB
    half = 256

    @pl.when(pl.program_id(1) == 0)
    def _():
        st_scr[...] = jnp.zeros(st_scr.shape, F32)

    q = q_ref[...]
    zf = zf_ref[...]
    v = v_ref[...]
    ls = jnp.minimum(zf, 0.0) - jnp.log1p(jnp.exp(-jnp.abs(zf)))
    a = lb_ref[0:1, :]
    y = lb_ref[1:2, :] + ls
    lf = jnp.maximum(a, y) + jnp.log1p(jnp.exp(-jnp.abs(a - y)))
    kk = lb_ref[2:3, :] * (1.0 / (1.0 + jnp.exp(zf)))

    rin = lax.broadcasted_iota(I32, (tt, 1), 0) & (r - 1)
    b = lf
    sh = 1
    while sh < r:
        b = b + jnp.where(rin >= sh, pltpu.roll(b, sh, axis=0), 0.0)
        sh *= 2

    rr = lax.broadcasted_iota(I32, (half, half), 0) // HEAD_DIM
    cc = lax.broadcasted_iota(I32, (half, half), 1) // HEAD_DIM
    same_head = rr == cc
    ones_bd = jnp.where(same_head, 1.0, 0.0).astype(BF16)
    bd_mask = jnp.where(same_head, 1.0, 0.0).astype(F32)

    o = jnp.zeros((tt, 2 * half), F32)
    for off in range(r):
        if off == 0:
            e = q * kk
            vs = v
        else:
            e = q * pltpu.roll(kk, off, axis=0) * jnp.exp(b - pltpu.roll(b, off, axis=0))
            e = jnp.where(rin >= off, e, 0.0)
            vs = pltpu.roll(v, off, axis=0)
        eb = e.astype(BF16)
        sc = jnp.concatenate(
            [jnp.dot(eb[:, 0:half], ones_bd, preferred_element_type=F32),
             jnp.dot(eb[:, half:], ones_bd, preferred_element_type=F32)], axis=1)
        o = o + sc * vs

    qe = (q * jnp.exp(b)).astype(BF16)
    vb = v.astype(BF16)
    for c in range(tt // r):
        r0 = c * r
        bl = b[r0 + r - 1:r0 + r, :]
        kd = (kk[r0:r0 + r, :] * jnp.exp(bl - b[r0:r0 + r, :])).astype(BF16)
        dec = jnp.exp(bl)
        for h2 in range(2):
            lo = h2 * half
            st = st_scr[h2]
            oi_scr[r0:r0 + r, lo:lo + half] = _dot_nt(qe[r0:r0 + r, lo:lo + half], st.astype(BF16))
            upd = _dot_tn(vb[r0:r0 + r, lo:lo + half], kd[:, lo:lo + half])
            st_scr[h2] = st * dec[:, lo:lo + half] + upd * bd_mask

    o = o + oi_scr[...]
    o2 = o * o
    ones_f = bd_mask
    ms = jnp.concatenate(
        [jnp.dot(o2[:, 0:half], ones_f, preferred_element_type=F32, precision=lax.Precision.HIGHEST),
         jnp.dot(o2[:, half:], ones_f, preferred_element_type=F32, precision=lax.Precision.HIGHEST)],
        axis=1) * (1.0 / HEAD_DIM)
    on = o * lax.rsqrt(ms + RMS_EPS) * ng_ref[...] * _sigmoid(g_ref[...])
    o_ref[...] = on.astype(o_ref.dtype)


def _hgrn(zc, lbp, ng, bsz, seq, tt):
    n = zc.shape[0]
    nt = seq // tt
    col = lambda cidx: pl.BlockSpec((tt, 512), lambda b, j: (b * nt + j, cidx))
    kern = functools.partial(_hgrn_kernel, tt=tt)
    return pl.pallas_call(
        kern, grid=(bsz, nt),
        in_specs=[col(0), col(1), col(2), col(3),
                  pl.BlockSpec((3, 512), lambda b, j: (0, 0)),
                  pl.BlockSpec((1, 512), lambda b, j: (0, 0))],
        out_specs=pl.BlockSpec((tt, 512), lambda b, j: (b * nt + j, 0)),
        out_shape=jax.ShapeDtypeStruct((n, 512), BF16),
        scratch_shapes=[pltpu.VMEM((2, 256, 256), F32), pltpu.VMEM((tt, 512), F32)],
        compiler_params=_params("parallel", "arbitrary"))(zc, zc, zc, zc, lbp, ng)


def _layer_norm(y, g, b):
    mu = jnp.mean(y, axis=-1, keepdims=True)
    d = y - mu
    var = jnp.mean(d * d, axis=-1, keepdims=True)
    return d * lax.rsqrt(var + LN_EPS) * g + b


def _merge_kernel(oa_ref, ob_ref, oc_ref, gt_ref, x_ref, wpa_ref, wpb_ref, wpc_ref, wout_ref,
                  g_ref, b_ref, wr_ref, br_ref, x1_ref, xb_ref, ti_ref, gate_ref, *, alpha):
    d = x_ref.shape[1]
    pa = jnp.dot(oa_ref[...], wpa_ref[...], preferred_element_type=F32)
    pb = jnp.dot(ob_ref[...], wpb_ref[...], preferred_element_type=F32)
    pc = jnp.dot(oc_ref[...], wpc_ref[...], preferred_element_type=F32)
    merged = (_sigmoid(gt_ref[:, 0:d]) * pa + _sigmoid(gt_ref[:, d:2 * d]) * pb
              + _sigmoid(gt_ref[:, 2 * d:3 * d]) * pc)
    mix = jnp.dot(merged.astype(BF16), wout_ref[...], preferred_element_type=F32)
    x1 = _layer_norm(alpha * x_ref[...] + mix, g_ref[...], b_ref[...])
    x1_ref[...] = x1
    xb_ref[...] = x1.astype(BF16)

    logits = jnp.dot(x1, wr_ref[...], preferred_element_type=F32,
                     precision=lax.Precision.HIGHEST) + br_ref[...]
    lane = lax.broadcasted_iota(I32, logits.shape, 1)
    topi = jnp.zeros(logits.shape, I32)
    topv = jnp.full(logits.shape, NEG, F32)
    for k in range(TOP_K):
        m = jnp.max(logits, axis=1, keepdims=True)
        idx = jnp.min(jnp.where(logits == m, lane, LANES), axis=1, keepdims=True)
        topi = jnp.where(lane == k, idx, topi)
        topv = jnp.where(lane == k, m, topv)
        logits = jnp.where(lane == idx, -jnp.inf, logits)
    e = jnp.where(lane < TOP_K, jnp.exp(topv - jnp.max(topv, axis=1, keepdims=True)), 0.0)
    ti_ref[...] = topi
    gate_ref[...] = e / jnp.sum(e, axis=1, keepdims=True)


def _merge(oa, ob, oc, gt, x2, wpa, wpb, wpc, wout, g, b, wr, br, alpha, tm):
    n, d = x2.shape
    row = lambda w: pl.BlockSpec((tm, w), lambda i: (i, 0))
    full = lambda a: pl.BlockSpec(a.shape, lambda i: (0,) * a.ndim)
    kern = functools.partial(_merge_kernel, alpha=alpha)
    return pl.pallas_call(
        kern, grid=(n // tm,),
        in_specs=[row(512), row(512), row(512), row(3 * d), row(d),
                  full(wpa), full(wpb), full(wpc), full(wout), full(g), full(b), full(wr), full(br)],
        out_specs=[row(d), row(d), row(LANES), row(LANES)],
        out_shape=[jax.ShapeDtypeStruct((n, d), F32), jax.ShapeDtypeStruct((n, d), BF16),
                   jax.ShapeDtypeStruct((n, LANES), I32), jax.ShapeDtypeStruct((n, LANES), F32)],
        compiler_params=_params("parallel"))(oa, ob, oc, gt, x2, wpa, wpb, wpc, wout, g, b, wr, br)


def _moe_kernel(be_ref, nu_ref, xs_ref, wgu_ref, bgu_ref, wd_ref, bd_ref, y_ref, wgu_bf, wd_bf):
    i = pl.program_id(0)
    de = wd_ref.shape[1]

    @pl.when((i == 0) | (be_ref[i] != be_ref[jnp.maximum(i - 1, 0)]))
    def _():
        wgu_bf[...] = wgu_ref[0].astype(BF16)
        wd_bf[...] = wd_ref[0].astype(BF16)

    @pl.when(i < nu_ref[0])
    def _():
        gu = jnp.dot(xs_ref[...], wgu_bf[...], preferred_element_type=F32) + bgu_ref[0]
        gate = jnp.minimum(gu[:, 0:de], SWIGLU_LIMIT)
        up = jnp.clip(gu[:, de:], -SWIGLU_LIMIT, SWIGLU_LIMIT)
        act = gate * _sigmoid(SWIGLU_ALPHA * gate) * (up + 1.0)
        y_ref[...] = jnp.dot(act.astype(BF16), wd_bf[...], preferred_element_type=F32) + bd_ref[0]

    @pl.when(i >= nu_ref[0])
    def _():
        y_ref[...] = jnp.zeros(y_ref.shape, F32)


def _moe_experts(blk_e, n_used, xs, wgu, bgu, wd, bd):
    n_slots, d = xs.shape
    n_blocks = n_slots // MOE_BLOCK
    e, _, n2 = wgu.shape
    de = wd.shape[1]
    grid_spec = pltpu.PrefetchScalarGridSpec(
        num_scalar_prefetch=2, grid=(n_blocks,),
        in_specs=[pl.BlockSpec((MOE_BLOCK, d), lambda i, be, nu: (i, 0)),
                  pl.BlockSpec((1, d, n2), lambda i, be, nu: (be[i], 0, 0)),
                  pl.BlockSpec((1, 1, n2), lambda i, be, nu: (be[i], 0, 0)),
                  pl.BlockSpec((1, de, d), lambda i, be, nu: (be[i], 0, 0)),
                  pl.BlockSpec((1, 1, d), lambda i, be, nu: (be[i], 0, 0))],
        out_specs=pl.BlockSpec((MOE_BLOCK, d), lambda i, be, nu: (i, 0)),
        scratch_shapes=[pltpu.VMEM((d, n2), BF16), pltpu.VMEM((de, d), BF16)])
    return pl.pallas_call(
        _moe_kernel, grid_spec=grid_spec,
        out_shape=jax.ShapeDtypeStruct((n_slots, d), F32),
        compiler_params=_params("arbitrary"))(
            blk_e, n_used, xs, wgu, bgu.reshape(e, 1, n2), wd, bd.reshape(e, 1, d))


def _combine_kernel(yg_ref, gate_ref, x1_ref, g_ref, b_ref, o_ref, *, alpha):
    gates = gate_ref[...]
    acc = alpha * x1_ref[...]
    for k in range(TOP_K):
        acc = acc + gates[:, k:k + 1] * yg_ref[k]
    o_ref[...] = _layer_norm(acc, g_ref[...], b_ref[...])


def _combine(yg, gates, x1, g, b, alpha, tm):
    n, d = x1.shape
    kern = functools.partial(_combine_kernel, alpha=alpha)
    return pl.pallas_call(
        kern, grid=(n // tm,),
        in_specs=[pl.BlockSpec((TOP_K, tm, d), lambda i: (0, i, 0)),
                  pl.BlockSpec((tm, LANES), lambda i: (i, 0)),
                  pl.BlockSpec((tm, d), lambda i: (i, 0)),
                  pl.BlockSpec((1, d), lambda i: (0, 0)),
                  pl.BlockSpec((1, d), lambda i: (0, 0))],
        out_specs=pl.BlockSpec((tm, d), lambda i: (i, 0)),
        out_shape=jax.ShapeDtypeStruct((n, d), F32),
        compiler_params=_params("parallel"))(yg, gates, x1, g, b)


def _rope_tables(seq, width):
    half = HEAD_DIM // 2
    inv = ROPE_THETA ** (-jnp.arange(half, dtype=F32) / half)
    ang = jnp.arange(seq, dtype=F32)[:, None] * inv[None, :]
    cos = jnp.concatenate([jnp.cos(ang), jnp.cos(ang)], axis=1)
    sin = jnp.concatenate([-jnp.sin(ang), jnp.sin(ang)], axis=1)
    reps = width // HEAD_DIM
    return jnp.tile(cos, (1, reps)), jnp.tile(sin, (1, reps))


def _split_w_in(w):
    seg = lambda a, b: w[:, _OFF[a]:_OFF[b]]
    w_dsa = seg(0, 3)
    small = jnp.concatenate([seg(5, 6), seg(9, 10), jnp.zeros((w.shape[0], LANES - 12), w.dtype)], axis=1)
    w_idx = jnp.concatenate([seg(3, 4)] + [seg(4, 5)] * IDX_HEADS + [small], axis=1)
    w_fox = seg(6, 9)
    w_hgrn = seg(10, 14)
    w_gate = seg(14, 15)
    return [t.astype(BF16) for t in (w_dsa, w_idx, w_fox, w_hgrn, w_gate)]


def _route(top_idx, n_tok):
    n_assign = n_tok * TOP_K
    flat_e = top_idx.reshape(-1)
    experts = jnp.arange(N_EXPERTS, dtype=I32)
    onehot = (flat_e[None, :] == experts[:, None]).astype(I32)
    csum = jnp.cumsum(onehot, axis=1)
    counts = csum[:, -1]
    rank = jnp.sum(csum * onehot, axis=0) - 1
    padded = (counts + MOE_BLOCK - 1) // MOE_BLOCK * MOE_BLOCK
    pad_end = jnp.cumsum(padded)
    pad_start = pad_end - padded
    start = jnp.cumsum(counts) - counts
    slot_of = pad_start[flat_e] + rank
    n_blocks = -(-(n_assign + N_EXPERTS * (MOE_BLOCK - 1)) // MOE_BLOCK)
    blk_start = jnp.arange(n_blocks, dtype=I32) * MOE_BLOCK
    blk_e = jnp.sum((pad_end[None, :] <= blk_start[:, None]).astype(I32), axis=1)
    blk_e = jnp.minimum(blk_e, N_EXPERTS - 1).astype(I32)
    n_used = (pad_end[-1] // MOE_BLOCK).astype(I32).reshape(1)
    order = jnp.argsort(flat_e).astype(I32)
    slot = jnp.arange(n_blocks * MOE_BLOCK, dtype=I32)
    slot_e = jnp.repeat(blk_e, MOE_BLOCK)
    r = slot - pad_start[slot_e]
    src = jnp.clip(start[slot_e] + r, 0, n_assign - 1)
    slot_tok = jnp.where(r < counts[slot_e], order[src] // TOP_K, 0)
    return slot_tok, blk_e, n_used, slot_of.reshape(n_tok, TOP_K).T


def _layer(x2, bsz, seq, alpha, w_in, b_fox, lb, ng, wpa, wpb, wpc, wout, g1, b1,
           wr, br, wgu, bgu, wd, bd, g2, b2, cos, sin):
    n, d = x2.shape
    tm = min(512, seq)
    w_dsa, w_idx, w_fox, w_hgrn, w_gate = _split_w_in(w_in)
    nst = seq // tm
    tab = lambda arr: (arr, pl.BlockSpec((tm, 512), lambda i: (i % nst, 0)))

    aq, ak, av = _proj_call(_proj_dsa_kernel, x2, w_dsa, [tab(cos), tab(sin)],
                            [(512, BF16)] * 3, tm)
    iq, ik4, small = _proj_call(_proj_idx_kernel, x2, w_idx, [tab(cos), tab(sin)],
                                [(256, BF16), (256, BF16), (LANES, F32)], tm)
    fq, fk, fv = _proj_call(_proj_fox_kernel, x2, w_fox, [], [(512, BF16)] * 3, tm)
    (zc,) = _proj_call(_proj_plain_kernel, x2, w_hgrn, [], [(2048, F32)], tm)
    (gt,) = _proj_call(_proj_plain_kernel, x2, w_gate, [], [(3 * d, F32)], tm)

    tq_att, tk_att = min(FLASH_TQ, seq), min(FLASH_TK, seq)
    iw_t = small[:, 0:8].reshape(bsz, seq, 8).transpose(0, 2, 1)
    mask = _dsa_index(iq, ik4, iw_t, bsz, seq, min(256, seq), min(512, seq))
    o_a = _flash(aq, ak, av, bsz, seq, tq_att, tk_att, mask=mask)

    ff_t = small[:, 4:12].reshape(bsz, seq, 8).transpose(0, 2, 1)
    nb = _fox_bias(ff_t, b_fox).reshape(bsz, 4, 2, seq)
    o_b = _flash(fq, fk, fv, bsz, seq, tq_att, tk_att, nb=nb)

    lbp = jnp.stack([jnp.log(lb), jnp.log1p(-lb), 1.0 - lb]).astype(F32)
    o_c = _hgrn(zc, lbp, ng.reshape(1, -1).astype(F32), bsz, seq, min(256, seq))

    wr_p = jnp.zeros((d, LANES), F32).at[:, :N_EXPERTS].set(wr.astype(F32))
    br_p = jnp.full((1, LANES), NEG, F32).at[0, :N_EXPERTS].set(br.astype(F32))
    x1, x1b, topi, gates = _merge(
        o_a, o_b, o_c, gt, x2, wpa.astype(BF16), wpb.astype(BF16), wpc.astype(BF16),
        wout.astype(BF16), g1.reshape(1, d), b1.reshape(1, d), wr_p, br_p, alpha, min(256, n))

    slot_tok, blk_e, n_used, slot_of = _route(topi[:, :TOP_K], n)
    xs = x1b[slot_tok]
    y_slots = _moe_experts(blk_e, n_used, xs, wgu, bgu, wd, bd)
    yg = y_slots[slot_of]
    return _combine(yg, gates, x1, g2.reshape(1, d), b2.reshape(1, d), alpha, min(256, n))


def kernel(x, w_in, b_fox_f, hgrn_lb_logits, hgrn_norm_g, w_branch_a, w_branch_b, w_branch_c, w_out, ln1_g, ln1_b, w_router, b_router, w_gu, b_gu, w_down, b_down, ln2_g, ln2_b):
    bsz, seq, d = x.shape
    depth = w_in.shape[0]
    alpha = (2 * depth) ** 0.25
    p = jax.nn.softmax(hgrn_lb_logits.astype(F32), axis=0)
    lbs = jnp.cumsum(p, axis=0)
    lbs = lbs - lbs[0]
    cos, sin = _rope_tables(seq, 512)
    x2 = x.reshape(bsz * seq, d)
    for l in range(depth):
        x2 = _layer(x2, bsz, seq, alpha, w_in[l], b_fox_f[l], lbs[l], hgrn_norm_g[l],
                    w_branch_a[l], w_branch_b[l], w_branch_c[l], w_out[l], ln1_g[l], ln1_b[l],
                    w_router[l], b_router[l], w_gu[l], b_gu[l], w_down[l], b_down[l],
                    ln2_g[l], ln2_b[l], cos, sin)
    return x2.reshape(bsz, seq, d)
```

```python
import functools
import math

import jax
import jax.numpy as jnp
import numpy as np
from jax import lax
from jax.experimental import pallas as pl
from jax.experimental.pallas import tpu as pltpu

F32 = jnp.float32
BF16 = jnp.bfloat16
I32 = jnp.int32

CHUNK = 64
HEAD_DIM = 64
ROPE_THETA = 10000.0
DSA_TOPK_MAX = 256
IDX_HEADS = 4
N_EXPERTS = 32
TOP_K = 4
SWIGLU_LIMIT = 7.0
SWIGLU_ALPHA = 1.702
MOE_BLOCK = 256
LN_EPS = 1e-5
RMS_EPS = 1e-6

LOG2E = 1.4426950408889634
NEG = -1e30
INT_MIN = -2147483648
INT_MAX = 2147483647
LANES = 128
FLASH_TQ = 2048
FLASH_TK = 512
HGRN_SUB = 16
VMEM_LIMIT = 56 * 1024 * 1024

IN_SPLITS = (512, 512, 512, 256, 64, 4, 512, 512, 512, 8, 512, 512, 512, 512, 3072)
_OFF = tuple(int(v) for v in np.cumsum((0,) + IN_SPLITS))


def _params(*sem):
    return pltpu.CompilerParams(dimension_semantics=sem, vmem_limit_bytes=VMEM_LIMIT)


def _dot_nt(a, b):
    return lax.dot_general(a, b, (((1,), (1,)), ((), ())), preferred_element_type=F32)


def _dot_tn(a, b):
    return lax.dot_general(a, b, (((0,), (0,)), ((), ())), preferred_element_type=F32)


def _sigmoid(t):
    return 1.0 / (1.0 + jnp.exp(-t))


def _rope_tile(t, cos, sin):
    w = t.shape[1]
    lane_d = lax.broadcasted_iota(I32, t.shape, 1) & (HEAD_DIM - 1)
    partner = jnp.where(lane_d < HEAD_DIM // 2,
                        pltpu.roll(t, w - HEAD_DIM // 2, axis=1),
                        pltpu.roll(t, HEAD_DIM // 2, axis=1))
    return t * cos + partner * sin


def _proj_dsa_kernel(x_ref, w_ref, cos_ref, sin_ref, q_ref, k_ref, v_ref):
    z = jnp.dot(x_ref[...].astype(BF16), w_ref[...], preferred_element_type=F32)
    cos = cos_ref[...]
    sin = sin_ref[...]
    q_ref[...] = (_rope_tile(z[:, 0:512], cos, sin) * (HEAD_DIM ** -0.5 * LOG2E)).astype(BF16)
    k_ref[...] = _rope_tile(z[:, 512:1024], cos, sin).astype(BF16)
    v_ref[...] = z[:, 1024:1536].astype(BF16)


def _proj_idx_kernel(x_ref, w_ref, cos_ref, sin_ref, iq_ref, ik_ref, sm_ref):
    z = jnp.dot(x_ref[...].astype(BF16), w_ref[...], preferred_element_type=F32)
    cos = cos_ref[...][:, 0:256]
    sin = sin_ref[...][:, 0:256]
    iq_ref[...] = (_rope_tile(z[:, 0:256], cos, sin) * (HEAD_DIM ** -0.5)).astype(BF16)
    ik_ref[...] = _rope_tile(z[:, 256:512], cos, sin).astype(BF16)
    sm_ref[...] = z[:, 512:640]


def _proj_fox_kernel(x_ref, w_ref, q_ref, k_ref, v_ref):
    z = jnp.dot(x_ref[...].astype(BF16), w_ref[...], preferred_element_type=F32)
    q_ref[...] = (z[:, 0:512] * (HEAD_DIM ** -0.5 * LOG2E)).astype(BF16)
    k_ref[...] = z[:, 512:1024].astype(BF16)
    v_ref[...] = z[:, 1024:1536].astype(BF16)


def _proj_plain_kernel(x_ref, w_ref, o_ref):
    o_ref[...] = jnp.dot(x_ref[...].astype(BF16), w_ref[...], preferred_element_type=F32)


def _proj_call(kernel_fn, x2, w, extra, outs, tm):
    n, d = x2.shape
    nout = w.shape[1]
    in_specs = [pl.BlockSpec((tm, d), lambda i: (i, 0)),
                pl.BlockSpec((d, nout), lambda i: (0, 0))]
    args = [x2, w]
    for arr, spec in extra:
        in_specs.append(spec)
        args.append(arr)
    out_shape = [jax.ShapeDtypeStruct((n, wdt), dt) for wdt, dt in outs]
    out_specs = [pl.BlockSpec((tm, wdt), lambda i: (i, 0)) for wdt, _ in outs]
    return pl.pallas_call(
        kernel_fn, grid=(n // tm,), in_specs=in_specs, out_specs=out_specs,
        out_shape=out_shape, compiler_params=_params("parallel"))(*args)


def _fox_bias_kernel(ff_ref, b_ref, nb_ref):
    s = ff_ref.shape[2]
    z = ff_ref[0] + b_ref[...]
    lf = jnp.minimum(z, 0.0) - jnp.log1p(jnp.exp(-jnp.abs(z)))
    r = lax.broadcasted_iota(I32, (LANES, LANES), 0)
    c = lax.broadcasted_iota(I32, (LANES, LANES), 1)
    upper = jnp.where(r <= c, 1.0, 0.0).astype(F32)
    carry = jnp.zeros((lf.shape[0], 1), F32)
    for t in range(s // LANES):
        blk = lf[:, t * LANES:(t + 1) * LANES]
        cs = jnp.dot(blk, upper, preferred_element_type=F32,
                     precision=lax.Precision.HIGHEST) + carry
        nb_ref[0, :, t * LANES:(t + 1) * LANES] = cs * (-LOG2E)
        carry = cs[:, LANES - 1:LANES]


def _fox_bias(ff_t, b_fox):
    bsz, h, s = ff_t.shape
    return pl.pallas_call(
        _fox_bias_kernel, grid=(bsz,),
        in_specs=[pl.BlockSpec((1, h, s), lambda b: (b, 0, 0)),
                  pl.BlockSpec((h, 1), lambda b: (0, 0))],
        out_specs=pl.BlockSpec((1, h, s), lambda b: (b, 0, 0)),
        out_shape=jax.ShapeDtypeStruct((bsz, h, s), F32),
        compiler_params=_params("parallel"))(ff_t, b_fox.reshape(h, 1).astype(F32))


def _dsa_index_kernel(iq_ref, ik_ref, iwt_ref, bias_ref, keys_scr, thr_scr, j_scr, *, tq, tk, seq, ksel):
    i = pl.program_id(1)
    n_kt = ((i + 1) * tq + tk - 1) // tk
    q_pos = i * tq + lax.broadcasted_iota(I32, (1, tq), 1)
    adm_end = (q_pos // CHUNK + 1) * CHUNK

    iq = iq_ref[...]
    lane = lax.broadcasted_iota(I32, iq.shape, 1)
    iqm = [jnp.where((lane >= HEAD_DIM * h) & (lane < HEAD_DIM * (h + 1)), iq, jnp.zeros_like(iq))
           for h in range(IDX_HEADS)]
    wts = [iwt_ref[0, h:h + 1, :] * (IDX_HEADS ** -0.5) for h in range(IDX_HEADS)]

    def key_idx(k0):
        return k0 + lax.broadcasted_iota(I32, (tk, 1), 0)

    def to_key(v):
        bits = pltpu.bitcast(v, I32)
        sign = bits >> 31
        return (bits ^ (sign & 0x7FFFFFFF)) - sign

    def from_key(k):
        sign = k >> 31
        return pltpu.bitcast((k + sign) ^ (sign & 0x7FFFFFFF), F32)

    def score_tile(kt, carry):
        kmax, kmin = carry
        k0 = pl.multiple_of(kt * tk, tk)
        ikt = ik_ref[pl.ds(k0, tk), :]
        sc = jnp.zeros((tk, tq), F32)
        for h in range(IDX_HEADS):
            sc = sc + jnp.maximum(_dot_nt(ikt, iqm[h]), 0.0) * wts[h]
        key = to_key(sc)
        adm = key_idx(k0) < adm_end
        k_lo = jnp.where(adm, key, INT_MIN)
        k_hi = jnp.where(adm, key, INT_MAX)
        keys_scr[pl.ds(k0, tk), :] = k_lo
        for g in range(tk // 8):
            kmax = jnp.maximum(kmax, k_lo[g * 8:(g + 1) * 8, :])
            kmin = jnp.minimum(kmin, k_hi[g * 8:(g + 1) * 8, :])
        return kmax, kmin

    kmax, kmin = lax.fori_loop(0, n_kt, score_tile,
                               (jnp.full((8, tq), INT_MIN, I32), jnp.full((8, tq), INT_MAX, I32)))
    kmax = jnp.max(kmax, axis=0, keepdims=True)
    kmin = jnp.min(kmin, axis=0, keepdims=True)

    kf = float(ksel)
    n_acc = 4
    grp = 8 * n_acc

    def count(pred):
        def body(kt, accs):
            k0 = pl.multiple_of(kt * tk, tk)
            accs = list(accs)
            for g in range(tk // grp):
                kk = keys_scr[pl.ds(k0 + g * grp, grp), :]
                hit = jnp.where(pred(kk, k0 + g * grp), 1.0, 0.0)
                for a in range(n_acc):
                    accs[a] = accs[a] + hit[a * 8:(a + 1) * 8, :]
            return tuple(accs)
        accs = lax.fori_loop(0, n_kt, body, tuple(jnp.zeros((8, tq), F32) for _ in range(n_acc)))
        tot = accs[0]
        for a in range(1, n_acc):
            tot = tot + accs[a]
        return jnp.sum(tot, axis=0, keepdims=True)

    def count_ge(cand):
        return count(lambda kk, k0: kk >= cand)

    n_adm = adm_end.astype(F32)
    zero = jnp.zeros((1, tq), F32)
    one = jnp.ones((1, tq), F32)
    log_k = math.log(kf)
    done0 = jnp.where(n_adm <= kf, 1.0, 0.0)
    init = (jnp.int32(0), kmin, kmax + 1, n_adm, zero, jnp.full((1, tq), INT_MIN, I32), done0, zero,
            one, one, zero)

    def cond(st):
        return (st[0] < 264) & (jnp.min(st[6]) == 0.0)

    def step(st):
        it, lo, hi, c_lo, c_hi, thr, done, tie, w_lo, w_hi, last = st
        lo_v = from_key(lo)
        hi_v = from_key(hi)
        f_lo = (jnp.log(c_lo + 0.5) - log_k) * w_lo
        f_hi = (log_k - jnp.log(c_hi + 0.5)) * w_hi
        cand = to_key(lo_v + (hi_v - lo_v) * (f_lo / (f_lo + f_hi)))
        cand = jnp.where(((it - 3) & 7) == 7, (lo >> 1) + (hi >> 1) + (lo & hi & 1), cand)
        cand = jnp.where(it == 0, kmax, cand)
        cand = jnp.where(it == 1, 0, cand)
        cand = jnp.where(it == 2, 1, cand)
        cand = jnp.minimum(jnp.maximum(cand, lo + 1), hi - 1)
        c = count_ge(cand)
        active = done == 0.0
        up = active & (c >= kf)
        down = active & (c < kf)
        lo = jnp.where(up, cand, lo)
        c_lo = jnp.where(up, c, c_lo)
        hi = jnp.where(down, cand, hi)
        c_hi = jnp.where(down, c, c_hi)
        w_hi = jnp.where(up & (last > 0.0), 0.5 * w_hi, jnp.where(down, 1.0, w_hi))
        w_lo = jnp.where(down & (last < 0.0), 0.5 * w_lo, jnp.where(up, 1.0, w_lo))
        last = jnp.where(up, 1.0, jnp.where(down, -1.0, last))
        hit = active & (c == kf)
        conv = active & jnp.logical_not(hit) & (hi - 1 <= lo)
        thr = jnp.where(hit, cand - 1, jnp.where(conv, lo, thr))
        tie = jnp.where(conv, 1.0, tie)
        done = jnp.where(hit | conv, 1.0, done)
        return it + 1, lo, hi, c_lo, c_hi, thr, done, tie, w_lo, w_hi, last

    st = lax.while_loop(cond, step, init)
    c_hi, thr, tie = st[4], st[5], st[7]
    thr_scr[...] = thr
    j_scr[...] = jnp.full((1, tq), -1, I32)

    @pl.when(jnp.max(tie) > 0.0)
    def _():
        need = kf - c_hi

        def idx_step(it, x):
            cand = x | jnp.left_shift(jnp.int32(1), 13 - it)
            rows = lax.broadcasted_iota(I32, (grp, 1), 0)
            cnt = count(lambda kk, k0: (kk == thr) & (k0 + rows < cand))
            return jnp.where(cnt < need, cand, x)

        x = lax.fori_loop(0, 14, idx_step, jnp.zeros((1, tq), I32))
        j_scr[...] = jnp.where(tie > 0.0, x, -1)

    thr = thr_scr[...]
    jmax = j_scr[...]

    def write_tile(kt, carry):
        k0 = pl.multiple_of(kt * tk, tk)
        kk = keys_scr[pl.ds(k0, tk), :]
        sel = (kk > thr) | ((kk == thr) & (key_idx(k0) <= jmax))
        bias_ref[0, :, pl.ds(k0, tk)] = jnp.where(sel, 0.0, NEG).T.astype(BF16)
        return carry

    lax.fori_loop(0, n_kt, write_tile, 0)

    def fill_tile(kt, carry):
        k0 = pl.multiple_of(kt * tk, tk)
        bias_ref[0, :, pl.ds(k0, tk)] = jnp.full((tq, tk), NEG, BF16)
        return carry

    lax.fori_loop(n_kt, seq // tk, fill_tile, 0)


def _dsa_index(iq, ik4, iw_t, bsz, seq, tq, tk):
    ksel = min(DSA_TOPK_MAX, seq // 4)
    nq = seq // tq
    kern = functools.partial(_dsa_index_kernel, tq=tq, tk=tk, seq=seq, ksel=ksel)
    return pl.pallas_call(
        kern, grid=(bsz, nq),
        in_specs=[pl.BlockSpec((tq, 256), lambda b, i: (b * nq + i, 0)),
                  pl.BlockSpec((seq, 256), lambda b, i: (b, 0)),
                  pl.BlockSpec((1, 8, tq), lambda b, i: (b, 0, i))],
        out_specs=pl.BlockSpec((1, tq, seq), lambda b, i: (b, i, 0)),
        out_shape=jax.ShapeDtypeStruct((bsz, seq, seq), BF16),
        scratch_shapes=[pltpu.VMEM((seq, tq), I32), pltpu.VMEM((1, tq), I32), pltpu.VMEM((1, tq), I32)],
        compiler_params=_params("parallel", "parallel"))(iq, ik4, iw_t)


def _flash_kernel(*refs, tq, tk, use_mask):
    if use_mask:
        q_ref, k_ref, v_ref, mask_ref, o_ref, m_scr, acc_scr, mbuf, sem = refs
        nb_ref = None
    else:
        q_ref, k_ref, v_ref, nb_ref, o_ref, m_scr, acc_scr = refs
        mask_ref = None
    b = pl.program_id(0)
    i = pl.program_id(2)
    q = q_ref[...]
    head0 = lax.broadcasted_iota(I32, (tq, LANES), 1) < HEAD_DIM
    head0k = lax.broadcasted_iota(I32, (tk, LANES), 1) < HEAD_DIM
    qm = [jnp.where(head0, q, jnp.zeros_like(q)), jnp.where(head0, jnp.zeros_like(q), q)]
    m_scr[...] = jnp.full(m_scr.shape, NEG, F32)
    acc_scr[...] = jnp.zeros(acc_scr.shape, F32)
    n_col = tk // LANES
    n_sub = tq // tk
    n_off = i * n_sub

    def mask_dma(kt, slot, r0):
        return pltpu.make_async_copy(
            mask_ref.at[b, pl.ds(i * tq + r0, tq - r0), pl.ds(pl.multiple_of(kt * tk, tk), tk)],
            mbuf.at[slot, pl.ds(r0, tq - r0), :], sem.at[slot])

    def tile(kt, diag):
        r0 = 0 if diag is None else diag * tk
        nr = tq - r0
        k0 = pl.multiple_of(kt * tk, tk)
        kt_ = k_ref[pl.ds(k0, tk), :]
        vt = v_ref[pl.ds(k0, tk), :]
        one = jnp.ones_like(vt)
        vx = [jnp.where(head0k, vt, one), jnp.where(head0k, one, vt)]
        if use_mask:
            slot = kt & 1
            if diag is None:
                mask_dma(kt + 1, 1 - slot, 0).start()
            elif diag + 1 < n_sub:
                mask_dma(kt + 1, 1 - slot, (diag + 1) * tk).start()
            mask_dma(kt, slot, r0).wait()
            extra = mbuf[slot, pl.ds(r0, nr), :].astype(F32)
        ss = []
        for j in range(2):
            s = _dot_nt(qm[j][r0:, :], kt_)
            if use_mask:
                s = s + extra
            else:
                s = s + nb_ref[0, 0, j:j + 1, pl.ds(k0, tk)]
                if diag is not None:
                    rr = lax.broadcasted_iota(I32, (nr, tk), 0)
                    cc = lax.broadcasted_iota(I32, (nr, tk), 1)
                    s = jnp.where(cc <= rr, s, NEG)
            ss.append(s)
        ps = []
        for j in range(2):
            sc = [ss[j][:, c * LANES:(c + 1) * LANES] for c in range(n_col)]
            part = sc[0]
            for c in range(1, n_col):
                part = jnp.maximum(part, sc[c])
            m_old = m_scr[j, r0:, :]
            m_new = jnp.maximum(m_old, jnp.max(part, axis=1, keepdims=True))
            alpha = jnp.exp2(m_old - m_new)
            ps.append((alpha, jnp.concatenate([jnp.exp2(c_ - m_new) for c_ in sc], axis=1).astype(BF16)))
            m_scr[j, r0:, :] = m_new
        for j in range(2):
            alpha, p = ps[j]
            acc_scr[j, r0:, :] = alpha * acc_scr[j, r0:, :] + jnp.dot(p, vx[j], preferred_element_type=F32)

    def body(kt, carry):
        tile(kt, None)
        return carry

    if use_mask:
        mask_dma(0, 0, 0).start()
    lax.fori_loop(0, n_off, body, 0)
    for d in range(n_sub):
        tile(n_off + d, d)

    a0 = acc_scr[0]
    a1 = acc_scr[1]
    o0 = a0 / pltpu.roll(a0, HEAD_DIM, axis=1)
    o1 = a1 / pltpu.roll(a1, HEAD_DIM, axis=1)
    o_ref[...] = jnp.where(head0, o0, o1).astype(o_ref.dtype)


def _flash(q, k, v, bsz, seq, tq, tk, nb=None, mask=None):
    nq = seq // tq
    n_hp = q.shape[1] // LANES
    use_mask = mask is not None
    in_specs = [pl.BlockSpec((tq, LANES), lambda b, h, i: (b * nq + i, h)),
                pl.BlockSpec((seq, LANES), lambda b, h, i: (b, h)),
                pl.BlockSpec((seq, LANES), lambda b, h, i: (b, h))]
    scratch = [pltpu.VMEM((2, tq, LANES), F32), pltpu.VMEM((2, tq, LANES), F32)]
    if use_mask:
        in_specs.append(pl.BlockSpec(memory_space=pl.ANY))
        scratch += [pltpu.VMEM((2, tq, tk), BF16), pltpu.SemaphoreType.DMA((2,))]
        extra = mask
    else:
        in_specs.append(pl.BlockSpec((1, 1, 2, seq), lambda b, h, i: (b, h, 0, 0)))
        extra = nb
    kern = functools.partial(_flash_kernel, tq=tq, tk=tk, use_mask=use_mask)
    return pl.pallas_call(
        kern, grid=(bsz, n_hp, nq), in_specs=in_specs,
        out_specs=pl.BlockSpec((tq, LANES), lambda b, h, i: (b * nq + i, h)),
        out_shape=jax.ShapeDtypeStruct(q.shape, BF16),
        scratch_shapes=scratch,
        compiler_params=_params("parallel", "parallel", "parallel"))(q, k, v, extra)


def _hgrn_kernel(q_ref, zf_ref, v_ref, g_ref, lb_ref, ng_ref, o_ref, st_scr, oi_scr, *, tt):
    r = HGRN_SUB
    half = 256

    @pl.when(pl.program_id(1) == 0)
    def _():
        st_scr[...] = jnp.zeros(st_scr.shape, F32)

    q = q_ref[...]
    zf = zf_ref[...]
    v = v_ref[...]
    ls = jnp.minimum(zf, 0.0) - jnp.log1p(jnp.exp(-jnp.abs(zf)))
    a = lb_ref[0:1, :]
    y = lb_ref[1:2, :] + ls
    lf = jnp.maximum(a, y) + jnp.log1p(jnp.exp(-jnp.abs(a - y)))
    kk = lb_ref[2:3, :] * (1.0 / (1.0 + jnp.exp(zf)))

    rin = lax.broadcasted_iota(I32, (tt, 1), 0) & (r - 1)
    b = lf
    sh = 1
    while sh < r:
        b = b + jnp.where(rin >= sh, pltpu.roll(b, sh, axis=0), 0.0)
        sh *= 2

    rr = lax.broadcasted_iota(I32, (half, half), 0) // HEAD_DIM
    cc = lax.broadcasted_iota(I32, (half, half), 1) // HEAD_DIM
    same_head = rr == cc
    ones_bd = jnp.where(same_head, 1.0, 0.0).astype(BF16)
    bd_mask = jnp.where(same_head, 1.0, 0.0).astype(F32)

    o = jnp.zeros((tt, 2 * half), F32)
    for off in range(r):
        if off == 0:
            e = q * kk
            vs = v
        else:
            e = q * pltpu.roll(kk, off, axis=0) * jnp.exp(b - pltpu.roll(b, off, axis=0))
            e = jnp.where(rin >= off, e, 0.0)
            vs = pltpu.roll(v, off, axis=0)
        eb = e.astype(BF16)
        sc = jnp.concatenate(
            [jnp.dot(eb[:, 0:half], ones_bd, preferred_element_type=F32),
             jnp.dot(eb[:, half:], ones_bd, preferred_element_type=F32)], axis=1)
        o = o + sc * vs

    qe = (q * jnp.exp(b)).astype(BF16)
    vb = v.astype(BF16)
    for c in range(tt // r):
        r0 = c * r
        bl = b[r0 + r - 1:r0 + r, :]
        kd = (kk[r0:r0 + r, :] * jnp.exp(bl - b[r0:r0 + r, :])).astype(BF16)
        dec = jnp.exp(bl)
        for h2 in range(2):
            lo = h2 * half
            st = st_scr[h2]
            oi_scr[r0:r0 + r, lo:lo + half] = _dot_nt(qe[r0:r0 + r, lo:lo + half], st.astype(BF16))
            upd = _dot_tn(vb[r0:r0 + r, lo:lo + half], kd[:, lo:lo + half])
            st_scr[h2] = st * dec[:, lo:lo + half] + upd * bd_mask

    o = o + oi_scr[...]
    o2 = o * o
    ones_f = bd_mask
    ms = jnp.concatenate(
        [jnp.dot(o2[:, 0:half], ones_f, preferred_element_type=F32, precision=lax.Precision.HIGHEST),
         jnp.dot(o2[:, half:], ones_f, preferred_element_type=F32, precision=lax.Precision.HIGHEST)],
        axis=1) * (1.0 / HEAD_DIM)
    on = o * lax.rsqrt(ms + RMS_EPS) * ng_ref[...] * _sigmoid(g_ref[...])
    o_ref[...] = on.astype(o_ref.dtype)


def _hgrn(zc, lbp, ng, bsz, seq, tt):
    n = zc.shape[0]
    nt = seq // tt
    col = lambda cidx: pl.BlockSpec((tt, 512), lambda b, j: (b * nt + j, cidx))
    kern = functools.partial(_hgrn_kernel, tt=tt)
    return pl.pallas_call(
        kern, grid=(bsz, nt),
        in_specs=[col(0), col(1), col(2), col(3),
                  pl.BlockSpec((3, 512), lambda b, j: (0, 0)),
                  pl.BlockSpec((1, 512), lambda b, j: (0, 0))],
        out_specs=pl.BlockSpec((tt, 512), lambda b, j: (b * nt + j, 0)),
        out_shape=jax.ShapeDtypeStruct((n, 512), BF16),
        scratch_shapes=[pltpu.VMEM((2, 256, 256), F32), pltpu.VMEM((tt, 512), F32)],
        compiler_params=_params("parallel", "arbitrary"))(zc, zc, zc, zc, lbp, ng)


def _layer_norm(y, g, b):
    mu = jnp.mean(y, axis=-1, keepdims=True)
    d = y - mu
    var = jnp.mean(d * d, axis=-1, keepdims=True)
    return d * lax.rsqrt(var + LN_EPS) * g + b


def _merge_kernel(oa_ref, ob_ref, oc_ref, gt_ref, x_ref, wpa_ref, wpb_ref, wpc_ref, wout_ref,
                  g_ref, b_ref, wr_ref, br_ref, x1_ref, xb_ref, ti_ref, gate_ref, *, alpha):
    d = x_ref.shape[1]
    pa = jnp.dot(oa_ref[...], wpa_ref[...], preferred_element_type=F32)
    pb = jnp.dot(ob_ref[...], wpb_ref[...], preferred_element_type=F32)
    pc = jnp.dot(oc_ref[...], wpc_ref[...], preferred_element_type=F32)
    merged = (_sigmoid(gt_ref[:, 0:d]) * pa + _sigmoid(gt_ref[:, d:2 * d]) * pb
              + _sigmoid(gt_ref[:, 2 * d:3 * d]) * pc)
    mix = jnp.dot(merged.astype(BF16), wout_ref[...], preferred_element_type=F32)
    x1 = _layer_norm(alpha * x_ref[...] + mix, g_ref[...], b_ref[...])
    x1_ref[...] = x1
    xb_ref[...] = x1.astype(BF16)

    logits = jnp.dot(x1, wr_ref[...], preferred_element_type=F32,
                     precision=lax.Precision.HIGHEST) + br_ref[...]
    lane = lax.broadcasted_iota(I32, logits.shape, 1)
    topi = jnp.zeros(logits.shape, I32)
    topv = jnp.full(logits.shape, NEG, F32)
    for k in range(TOP_K):
        m = jnp.max(logits, axis=1, keepdims=True)
        idx = jnp.min(jnp.where(logits == m, lane, LANES), axis=1, keepdims=True)
        topi = jnp.where(lane == k, idx, topi)
        topv = jnp.where(lane == k, m, topv)
        logits = jnp.where(lane == idx, -jnp.inf, logits)
    e = jnp.where(lane < TOP_K, jnp.exp(topv - jnp.max(topv, axis=1, keepdims=True)), 0.0)
    ti_ref[...] = topi
    gate_ref[...] = e / jnp.sum(e, axis=1, keepdims=True)


def _merge(oa, ob, oc, gt, x2, wpa, wpb, wpc, wout, g, b, wr, br, alpha, tm):
    n, d = x2.shape
    row = lambda w: pl.BlockSpec((tm, w), lambda i: (i, 0))
    full = lambda a: pl.BlockSpec(a.shape, lambda i: (0,) * a.ndim)
    kern = functools.partial(_merge_kernel, alpha=alpha)
    return pl.pallas_call(
        kern, grid=(n // tm,),
        in_specs=[row(512), row(512), row(512), row(3 * d), row(d),
                  full(wpa), full(wpb), full(wpc), full(wout), full(g), full(b), full(wr), full(br)],
        out_specs=[row(d), row(d), row(LANES), row(LANES)],
        out_shape=[jax.ShapeDtypeStruct((n, d), F32), jax.ShapeDtypeStruct((n, d), BF16),
                   jax.ShapeDtypeStruct((n, LANES), I32), jax.ShapeDtypeStruct((n, LANES), F32)],
        compiler_params=_params("parallel"))(oa, ob, oc, gt, x2, wpa, wpb, wpc, wout, g, b, wr, br)


def _moe_kernel(be_ref, nu_ref, xs_ref, wgu_ref, bgu_ref, wd_ref, bd_ref, y_ref, wgu_bf, wd_bf):
    i = pl.program_id(0)
    de = wd_ref.shape[1]

    @pl.when((i == 0) | (be_ref[i] != be_ref[jnp.maximum(i - 1, 0)]))
    def _():
        wgu_bf[...] = wgu_ref[0].astype(BF16)
        wd_bf[...] = wd_ref[0].astype(BF16)

    @pl.when(i < nu_ref[0])
    def _():
        gu = jnp.dot(xs_ref[...], wgu_bf[...], preferred_element_type=F32) + bgu_ref[0]
        gate = jnp.minimum(gu[:, 0:de], SWIGLU_LIMIT)
        up = jnp.clip(gu[:, de:], -SWIGLU_LIMIT, SWIGLU_LIMIT)
        act = gate * _sigmoid(SWIGLU_ALPHA * gate) * (up + 1.0)
        y_ref[...] = jnp.dot(act.astype(BF16), wd_bf[...], preferred_element_type=F32) + bd_ref[0]

    @pl.when(i >= nu_ref[0])
    def _():
        y_ref[...] = jnp.zeros(y_ref.shape, F32)


def _moe_experts(layer, blk_e, n_used, xs, wgu, bgu, wd, bd):
    n_slots, d = xs.shape
    n_blocks = n_slots // MOE_BLOCK
    _, _, n2 = wgu.shape
    de = wd.shape[1]
    base = layer * N_EXPERTS
    grid_spec = pltpu.PrefetchScalarGridSpec(
        num_scalar_prefetch=2, grid=(n_blocks,),
        in_specs=[pl.BlockSpec((MOE_BLOCK, d), lambda i, be, nu: (i, 0)),
                  pl.BlockSpec((1, d, n2), lambda i, be, nu: (base + be[i], 0, 0)),
                  pl.BlockSpec((1, 1, n2), lambda i, be, nu: (base + be[i], 0, 0)),
                  pl.BlockSpec((1, de, d), lambda i, be, nu: (base + be[i], 0, 0)),
                  pl.BlockSpec((1, 1, d), lambda i, be, nu: (base + be[i], 0, 0))],
        out_specs=pl.BlockSpec((MOE_BLOCK, d), lambda i, be, nu: (i, 0)),
        scratch_shapes=[pltpu.VMEM((d, n2), BF16), pltpu.VMEM((de, d), BF16)])
    return pl.pallas_call(
        _moe_kernel, grid_spec=grid_spec,
        out_shape=jax.ShapeDtypeStruct((n_slots, d), F32),
        compiler_params=_params("arbitrary"))(blk_e, n_used, xs, wgu, bgu, wd, bd)


def _combine_kernel(yg_ref, gate_ref, x1_ref, g_ref, b_ref, o_ref, *, alpha):
    gates = gate_ref[...]
    acc = alpha * x1_ref[...]
    for k in range(TOP_K):
        acc = acc + gates[:, k:k + 1] * yg_ref[k]
    o_ref[...] = _layer_norm(acc, g_ref[...], b_ref[...])


def _combine(yg, gates, x1, g, b, alpha, tm):
    n, d = x1.shape
    kern = functools.partial(_combine_kernel, alpha=alpha)
    return pl.pallas_call(
        kern, grid=(n // tm,),
        in_specs=[pl.BlockSpec((TOP_K, tm, d), lambda i: (0, i, 0)),
                  pl.BlockSpec((tm, LANES), lambda i: (i, 0)),
                  pl.BlockSpec((tm, d), lambda i: (i, 0)),
                  pl.BlockSpec((1, d), lambda i: (0, 0)),
                  pl.BlockSpec((1, d), lambda i: (0, 0))],
        out_specs=pl.BlockSpec((tm, d), lambda i: (i, 0)),
        out_shape=jax.ShapeDtypeStruct((n, d), F32),
        compiler_params=_params("parallel"))(yg, gates, x1, g, b)


def _rope_tables(seq, width):
    half = HEAD_DIM // 2
    inv = ROPE_THETA ** (-jnp.arange(half, dtype=F32) / half)
    ang = jnp.arange(seq, dtype=F32)[:, None] * inv[None, :]
    cos = jnp.concatenate([jnp.cos(ang), jnp.cos(ang)], axis=1)
    sin = jnp.concatenate([-jnp.sin(ang), jnp.sin(ang)], axis=1)
    reps = width // HEAD_DIM
    return jnp.tile(cos, (1, reps)), jnp.tile(sin, (1, reps))


def _split_w_in(w):
    seg = lambda a, b: w[:, _OFF[a]:_OFF[b]]
    w_dsa = seg(0, 3)
    small = jnp.concatenate([seg(5, 6), seg(9, 10), jnp.zeros((w.shape[0], LANES - 12), w.dtype)], axis=1)
    w_idx = jnp.concatenate([seg(3, 4)] + [seg(4, 5)] * IDX_HEADS + [small], axis=1)
    w_fox = seg(6, 9)
    w_hgrn = seg(10, 14)
    w_gate = seg(14, 15)
    return [t.astype(BF16) for t in (w_dsa, w_idx, w_fox, w_hgrn, w_gate)]


def _route(top_idx, n_tok):
    n_assign = n_tok * TOP_K
    flat_e = top_idx.reshape(-1)
    experts = jnp.arange(N_EXPERTS, dtype=I32)
    onehot = (flat_e[None, :] == experts[:, None]).astype(I32)
    csum = jnp.cumsum(onehot, axis=1)
    counts = csum[:, -1]
    rank = jnp.sum(csum * onehot, axis=0) - 1
    padded = (counts + MOE_BLOCK - 1) // MOE_BLOCK * MOE_BLOCK
    pad_end = jnp.cumsum(padded)
    pad_start = pad_end - padded
    start = jnp.cumsum(counts) - counts
    slot_of = pad_start[flat_e] + rank
    n_blocks = -(-(n_assign + N_EXPERTS * (MOE_BLOCK - 1)) // MOE_BLOCK)
    blk_start = jnp.arange(n_blocks, dtype=I32) * MOE_BLOCK
    blk_e = jnp.sum((pad_end[None, :] <= blk_start[:, None]).astype(I32), axis=1)
    blk_e = jnp.minimum(blk_e, N_EXPERTS - 1).astype(I32)
    n_used = (pad_end[-1] // MOE_BLOCK).astype(I32).reshape(1)
    order = jnp.argsort(flat_e).astype(I32)
    slot = jnp.arange(n_blocks * MOE_BLOCK, dtype=I32)
    slot_e = jnp.repeat(blk_e, MOE_BLOCK)
    r = slot - pad_start[slot_e]
    src = jnp.clip(start[slot_e] + r, 0, n_assign - 1)
    slot_tok = jnp.where(r < counts[slot_e], order[src] // TOP_K, 0)
    return slot_tok, blk_e, n_used, slot_of.reshape(n_tok, TOP_K).T


def _layer(layer, x2, bsz, seq, alpha, w_in, b_fox, lb, ng, wpa, wpb, wpc, wout, g1, b1,
           wr, br, wgu, bgu, wd, bd, g2, b2, cos, sin):
    n, d = x2.shape
    tm = min(512, seq)
    w_dsa, w_idx, w_fox, w_hgrn, w_gate = _split_w_in(w_in)
    nst = seq // tm
    tab = lambda arr: (arr, pl.BlockSpec((tm, 512), lambda i: (i % nst, 0)))

    aq, ak, av = _proj_call(_proj_dsa_kernel, x2, w_dsa, [tab(cos), tab(sin)],
                            [(512, BF16)] * 3, tm)
    iq, ik4, small = _proj_call(_proj_idx_kernel, x2, w_idx, [tab(cos), tab(sin)],
                                [(256, BF16), (256, BF16), (LANES, F32)], tm)
    fq, fk, fv = _proj_call(_proj_fox_kernel, x2, w_fox, [], [(512, BF16)] * 3, tm)
    (zc,) = _proj_call(_proj_plain_kernel, x2, w_hgrn, [], [(2048, F32)], tm)
    (gt,) = _proj_call(_proj_plain_kernel, x2, w_gate, [], [(3 * d, F32)], tm)

    tq_att, tk_att = min(FLASH_TQ, seq), min(FLASH_TK, seq)
    iw_t = small[:, 0:8].reshape(bsz, seq, 8).transpose(0, 2, 1)
    mask = _dsa_index(iq, ik4, iw_t, bsz, seq, min(256, seq), min(512, seq))
    o_a = _flash(aq, ak, av, bsz, seq, tq_att, tk_att, mask=mask)

    ff_t = small[:, 4:12].reshape(bsz, seq, 8).transpose(0, 2, 1)
    nb = _fox_bias(ff_t, b_fox).reshape(bsz, 4, 2, seq)
    o_b = _flash(fq, fk, fv, bsz, seq, tq_att, tk_att, nb=nb)

    lbp = jnp.stack([jnp.log(lb), jnp.log1p(-lb), 1.0 - lb]).astype(F32)
    o_c = _hgrn(zc, lbp, ng.reshape(1, -1).astype(F32), bsz, seq, min(256, seq))

    wr_p = jnp.zeros((d, LANES), F32).at[:, :N_EXPERTS].set(wr.astype(F32))
    br_p = jnp.full((1, LANES), NEG, F32).at[0, :N_EXPERTS].set(br.astype(F32))
    x1, x1b, topi, gates = _merge(
        o_a, o_b, o_c, gt, x2, wpa.astype(BF16), wpb.astype(BF16), wpc.astype(BF16),
        wout.astype(BF16), g1.reshape(1, d), b1.reshape(1, d), wr_p, br_p, alpha, min(256, n))

    slot_tok, blk_e, n_used, slot_of = _route(topi[:, :TOP_K], n)
    xs = x1b[slot_tok]
    y_slots = _moe_experts(layer, blk_e, n_used, xs, wgu, bgu, wd, bd)
    yg = y_slots[slot_of]
    return _combine(yg, gates, x1, g2.reshape(1, d), b2.reshape(1, d), alpha, min(256, n))


def kernel(x, w_in, b_fox_f, hgrn_lb_logits, hgrn_norm_g, w_branch_a, w_branch_b, w_branch_c, w_out, ln1_g, ln1_b, w_router, b_router, w_gu, b_gu, w_down, b_down, ln2_g, ln2_b):
    bsz, seq, d = x.shape
    depth = w_in.shape[0]
    alpha = (2 * depth) ** 0.25
    p = jax.nn.softmax(hgrn_lb_logits.astype(F32), axis=0)
    lbs = jnp.cumsum(p, axis=0)
    lbs = lbs - lbs[0]
    cos, sin = _rope_tables(seq, 512)
    x2 = x.reshape(bsz * seq, d)
    n_e = depth * w_gu.shape[1]
    wgu = w_gu.reshape(n_e, d, w_gu.shape[3])
    bgu = b_gu.reshape(n_e, 1, b_gu.shape[2])
    wd = w_down.reshape(n_e, w_down.shape[2], d)
    bd = b_down.reshape(n_e, 1, d)
    for l in range(depth):
        x2 = _layer(l, x2, bsz, seq, alpha, w_in[l], b_fox_f[l], lbs[l], hgrn_norm_g[l],
                    w_branch_a[l], w_branch_b[l], w_branch_c[l], w_out[l], ln1_g[l], ln1_b[l],
                    w_router[l], b_router[l], wgu, bgu, wd, bd, ln2_g[l], ln2_b[l], cos, sin)
    return x2.reshape(bsz, seq, d)
```

```python
import functools
import math

import jax
import jax.numpy as jnp
import numpy as np
from jax import lax
from jax.experimental import pallas as pl
from jax.experimental.pallas import tpu as pltpu

F32 = jnp.float32
BF16 = jnp.bfloat16
I32 = jnp.int32

CHUNK = 64
HEAD_DIM = 64
ROPE_THETA = 10000.0
DSA_TOPK_MAX = 256
IDX_HEADS = 4
N_EXPERTS = 32
TOP_K = 4
SWIGLU_LIMIT = 7.0
SWIGLU_ALPHA = 1.702
MOE_BLOCK = 256
LN_EPS = 1e-5
RMS_EPS = 1e-6

LOG2E = 1.4426950408889634
NEG = -1e30
INT_MIN = -2147483648
INT_MAX = 2147483647
LANES = 128
DSA_PROBES_PER_CHECK = 3
FLASH_TQ = 2048
FLASH_TK = 512
HGRN_SUB = 16
VMEM_LIMIT = 56 * 1024 * 1024

IN_SPLITS = (512, 512, 512, 256, 64, 4, 512, 512, 512, 8, 512, 512, 512, 512, 3072)
_OFF = tuple(int(v) for v in np.cumsum((0,) + IN_SPLITS))


def _params(*sem):
    return pltpu.CompilerParams(dimension_semantics=sem, vmem_limit_bytes=VMEM_LIMIT)


def _dot_nt(a, b):
    return lax.dot_general(a, b, (((1,), (1,)), ((), ())), preferred_element_type=F32)


def _dot_tn(a, b):
    return lax.dot_general(a, b, (((0,), (0,)), ((), ())), preferred_element_type=F32)


def _sigmoid(t):
    return 1.0 / (1.0 + jnp.exp(-t))


def _rope_tile(t, cos, sin):
    w = t.shape[1]
    lane_d = lax.broadcasted_iota(I32, t.shape, 1) & (HEAD_DIM - 1)
    partner = jnp.where(lane_d < HEAD_DIM // 2,
                        pltpu.roll(t, w - HEAD_DIM // 2, axis=1),
                        pltpu.roll(t, HEAD_DIM // 2, axis=1))
    return t * cos + partner * sin


def _proj_dsa_kernel(x_ref, w_ref, cos_ref, sin_ref, q_ref, k_ref, v_ref):
    z = jnp.dot(x_ref[...].astype(BF16), w_ref[...], preferred_element_type=F32)
    cos = cos_ref[...]
    sin = sin_ref[...]
    q_ref[...] = (_rope_tile(z[:, 0:512], cos, sin) * (HEAD_DIM ** -0.5 * LOG2E)).astype(BF16)
    k_ref[...] = _rope_tile(z[:, 512:1024], cos, sin).astype(BF16)
    v_ref[...] = z[:, 1024:1536].astype(BF16)


def _proj_idx_kernel(x_ref, w_ref, cos_ref, sin_ref, iq_ref, ik_ref, sm_ref):
    z = jnp.dot(x_ref[...].astype(BF16), w_ref[...], preferred_element_type=F32)
    cos = cos_ref[...][:, 0:256]
    sin = sin_ref[...][:, 0:256]
    iq_ref[...] = (_rope_tile(z[:, 0:256], cos, sin) * (HEAD_DIM ** -0.5)).astype(BF16)
    ik_ref[...] = _rope_tile(z[:, 256:512], cos, sin).astype(BF16)
    sm_ref[...] = z[:, 512:640]


def _proj_fox_kernel(x_ref, w_ref, q_ref, k_ref, v_ref):
    z = jnp.dot(x_ref[...].astype(BF16), w_ref[...], preferred_element_type=F32)
    q_ref[...] = (z[:, 0:512] * (HEAD_DIM ** -0.5 * LOG2E)).astype(BF16)
    k_ref[...] = z[:, 512:1024].astype(BF16)
    v_ref[...] = z[:, 1024:1536].astype(BF16)


def _proj_plain_kernel(x_ref, w_ref, o_ref):
    o_ref[...] = jnp.dot(x_ref[...].astype(BF16), w_ref[...], preferred_element_type=F32)


def _proj_call(kernel_fn, x2, w, extra, outs, tm):
    n, d = x2.shape
    nout = w.shape[1]
    in_specs = [pl.BlockSpec((tm, d), lambda i: (i, 0)),
                pl.BlockSpec((d, nout), lambda i: (0, 0))]
    args = [x2, w]
    for arr, spec in extra:
        in_specs.append(spec)
        args.append(arr)
    out_shape = [jax.ShapeDtypeStruct((n, wdt), dt) for wdt, dt in outs]
    out_specs = [pl.BlockSpec((tm, wdt), lambda i: (i, 0)) for wdt, _ in outs]
    return pl.pallas_call(
        kernel_fn, grid=(n // tm,), in_specs=in_specs, out_specs=out_specs,
        out_shape=out_shape, compiler_params=_params("parallel"))(*args)


def _fox_bias_kernel(ff_ref, b_ref, nb_ref):
    s = ff_ref.shape[2]
    z = ff_ref[0] + b_ref[...]
    lf = jnp.minimum(z, 0.0) - jnp.log1p(jnp.exp(-jnp.abs(z)))
    r = lax.broadcasted_iota(I32, (LANES, LANES), 0)
    c = lax.broadcasted_iota(I32, (LANES, LANES), 1)
    upper = jnp.where(r <= c, 1.0, 0.0).astype(F32)
    carry = jnp.zeros((lf.shape[0], 1), F32)
    for t in range(s // LANES):
        blk = lf[:, t * LANES:(t + 1) * LANES]
        cs = jnp.dot(blk, upper, preferred_element_type=F32,
                     precision=lax.Precision.HIGHEST) + carry
        nb_ref[0, :, t * LANES:(t + 1) * LANES] = cs * (-LOG2E)
        carry = cs[:, LANES - 1:LANES]


def _fox_bias(ff_t, b_fox):
    bsz, h, s = ff_t.shape
    return pl.pallas_call(
        _fox_bias_kernel, grid=(bsz,),
        in_specs=[pl.BlockSpec((1, h, s), lambda b: (b, 0, 0)),
                  pl.BlockSpec((h, 1), lambda b: (0, 0))],
        out_specs=pl.BlockSpec((1, h, s), lambda b: (b, 0, 0)),
        out_shape=jax.ShapeDtypeStruct((bsz, h, s), F32),
        compiler_params=_params("parallel"))(ff_t, b_fox.reshape(h, 1).astype(F32))


def _dsa_index_kernel(iq_ref, ik_ref, iwt_ref, bias_ref, keys_scr, thr_scr, j_scr, *, tq, tk, seq, ksel):
    i = pl.program_id(1)
    n_kt = ((i + 1) * tq + tk - 1) // tk
    q_pos = i * tq + lax.broadcasted_iota(I32, (1, tq), 1)
    adm_end = (q_pos // CHUNK + 1) * CHUNK

    iq = iq_ref[...]
    lane = lax.broadcasted_iota(I32, iq.shape, 1)
    iqm = [jnp.where((lane >= HEAD_DIM * h) & (lane < HEAD_DIM * (h + 1)), iq, jnp.zeros_like(iq))
           for h in range(IDX_HEADS)]
    wts = [iwt_ref[0, h:h + 1, :] * (IDX_HEADS ** -0.5) for h in range(IDX_HEADS)]

    def key_idx(k0):
        return k0 + lax.broadcasted_iota(I32, (tk, 1), 0)

    def to_key(v):
        bits = pltpu.bitcast(v, I32)
        sign = bits >> 31
        return (bits ^ (sign & 0x7FFFFFFF)) - sign

    def from_key(k):
        sign = k >> 31
        return pltpu.bitcast((k + sign) ^ (sign & 0x7FFFFFFF), F32)

    def score_tile(kt, carry):
        kmax, kmin = carry
        k0 = pl.multiple_of(kt * tk, tk)
        ikt = ik_ref[pl.ds(k0, tk), :]
        sc = jnp.zeros((tk, tq), F32)
        for h in range(IDX_HEADS):
            sc = sc + jnp.maximum(_dot_nt(ikt, iqm[h]), 0.0) * wts[h]
        key = to_key(sc)
        adm = key_idx(k0) < adm_end
        k_lo = jnp.where(adm, key, INT_MIN)
        k_hi = jnp.where(adm, key, INT_MAX)
        keys_scr[pl.ds(k0, tk), :] = k_lo
        for g in range(tk // 8):
            kmax = jnp.maximum(kmax, k_lo[g * 8:(g + 1) * 8, :])
            kmin = jnp.minimum(kmin, k_hi[g * 8:(g + 1) * 8, :])
        return kmax, kmin

    kmax, kmin = lax.fori_loop(0, n_kt, score_tile,
                               (jnp.full((8, tq), INT_MIN, I32), jnp.full((8, tq), INT_MAX, I32)))
    kmax = jnp.max(kmax, axis=0, keepdims=True)
    kmin = jnp.min(kmin, axis=0, keepdims=True)

    kf = float(ksel)
    n_acc = 4
    grp = 8 * n_acc

    def count(pred):
        def body(kt, accs):
            k0 = pl.multiple_of(kt * tk, tk)
            accs = list(accs)
            for g in range(tk // grp):
                kk = keys_scr[pl.ds(k0 + g * grp, grp), :]
                hit = jnp.where(pred(kk, k0 + g * grp), 1.0, 0.0)
                for a in range(n_acc):
                    accs[a] = accs[a] + hit[a * 8:(a + 1) * 8, :]
            return tuple(accs)
        accs = lax.fori_loop(0, n_kt, body, tuple(jnp.zeros((8, tq), F32) for _ in range(n_acc)))
        tot = accs[0]
        for a in range(1, n_acc):
            tot = tot + accs[a]
        return jnp.sum(tot, axis=0, keepdims=True)

    def count_ge(cand):
        return count(lambda kk, k0: kk >= cand)

    n_adm = adm_end.astype(F32)
    zero = jnp.zeros((1, tq), F32)
    one = jnp.ones((1, tq), F32)
    log_k = math.log(kf)
    done0 = jnp.where(n_adm <= kf, 1.0, 0.0)
    init = (jnp.int32(0), kmin, kmax + 1, n_adm, zero, jnp.full((1, tq), INT_MIN, I32), done0, zero,
            one, one, zero)

    def cond(st):
        return (st[0] < 264) & (jnp.min(st[6]) == 0.0)

    def step(st):
        it, lo, hi, c_lo, c_hi, thr, done, tie, w_lo, w_hi, last = st
        lo_v = from_key(lo)
        hi_v = from_key(hi)
        f_lo = (jnp.log(c_lo + 0.5) - log_k) * w_lo
        f_hi = (log_k - jnp.log(c_hi + 0.5)) * w_hi
        cand = to_key(lo_v + (hi_v - lo_v) * (f_lo / (f_lo + f_hi)))
        cand = jnp.where(((it - 3) & 7) == 7, (lo >> 1) + (hi >> 1) + (lo & hi & 1), cand)
        cand = jnp.where(it == 0, kmax, cand)
        cand = jnp.where(it == 1, 0, cand)
        cand = jnp.where(it == 2, 1, cand)
        cand = jnp.minimum(jnp.maximum(cand, lo + 1), hi - 1)
        c = count_ge(cand)
        active = done == 0.0
        up = active & (c >= kf)
        down = active & (c < kf)
        lo = jnp.where(up, cand, lo)
        c_lo = jnp.where(up, c, c_lo)
        hi = jnp.where(down, cand, hi)
        c_hi = jnp.where(down, c, c_hi)
        w_hi = jnp.where(up & (last > 0.0), 0.5 * w_hi, jnp.where(down, 1.0, w_hi))
        w_lo = jnp.where(down & (last < 0.0), 0.5 * w_lo, jnp.where(up, 1.0, w_lo))
        last = jnp.where(up, 1.0, jnp.where(down, -1.0, last))
        hit = active & (c == kf)
        conv = active & jnp.logical_not(hit) & (hi - 1 <= lo)
        thr = jnp.where(hit, cand - 1, jnp.where(conv, lo, thr))
        tie = jnp.where(conv, 1.0, tie)
        done = jnp.where(hit | conv, 1.0, done)
        return it + 1, lo, hi, c_lo, c_hi, thr, done, tie, w_lo, w_hi, last

    def steps(st):
        for _ in range(DSA_PROBES_PER_CHECK):
            st = step(st)
        return st

    st = lax.while_loop(cond, steps, init)
    c_hi, thr, tie = st[4], st[5], st[7]
    thr_scr[...] = thr
    j_scr[...] = jnp.full((1, tq), -1, I32)

    @pl.when(jnp.max(tie) > 0.0)
    def _():
        need = kf - c_hi

        def idx_step(it, x):
            cand = x | jnp.left_shift(jnp.int32(1), 13 - it)
            rows = lax.broadcasted_iota(I32, (grp, 1), 0)
            cnt = count(lambda kk, k0: (kk == thr) & (k0 + rows < cand))
            return jnp.where(cnt < need, cand, x)

        x = lax.fori_loop(0, 14, idx_step, jnp.zeros((1, tq), I32))
        j_scr[...] = jnp.where(tie > 0.0, x, -1)

    thr = thr_scr[...]
    jmax = j_scr[...]

    def write_tile(kt, carry):
        k0 = pl.multiple_of(kt * tk, tk)
        kk = keys_scr[pl.ds(k0, tk), :]
        sel = (kk > thr) | ((kk == thr) & (key_idx(k0) <= jmax))
        bias_ref[0, :, pl.ds(k0, tk)] = jnp.where(sel, 0.0, NEG).T.astype(BF16)
        return carry

    lax.fori_loop(0, n_kt, write_tile, 0)

    def fill_tile(kt, carry):
        k0 = pl.multiple_of(kt * tk, tk)
        bias_ref[0, :, pl.ds(k0, tk)] = jnp.full((tq, tk), NEG, BF16)
        return carry

    lax.fori_loop(n_kt, seq // tk, fill_tile, 0)


def _dsa_index(iq, ik4, iw_t, bsz, seq, tq, tk):
    ksel = min(DSA_TOPK_MAX, seq // 4)
    nq = seq // tq
    kern = functools.partial(_dsa_index_kernel, tq=tq, tk=tk, seq=seq, ksel=ksel)
    return pl.pallas_call(
        kern, grid=(bsz, nq),
        in_specs=[pl.BlockSpec((tq, 256), lambda b, i: (b * nq + i, 0)),
                  pl.BlockSpec((seq, 256), lambda b, i: (b, 0)),
                  pl.BlockSpec((1, 8, tq), lambda b, i: (b, 0, i))],
        out_specs=pl.BlockSpec((1, tq, seq), lambda b, i: (b, i, 0)),
        out_shape=jax.ShapeDtypeStruct((bsz, seq, seq), BF16),
        scratch_shapes=[pltpu.VMEM((seq, tq), I32), pltpu.VMEM((1, tq), I32), pltpu.VMEM((1, tq), I32)],
        compiler_params=_params("parallel", "parallel"))(iq, ik4, iw_t)


def _flash_kernel(*refs, tq, tk, use_mask):
    if use_mask:
        q_ref, k_ref, v_ref, mask_ref, o_ref, m_scr, acc_scr, mbuf, sem = refs
        nb_ref = None
    else:
        q_ref, k_ref, v_ref, nb_ref, o_ref, m_scr, acc_scr = refs
        mask_ref = None
    b = pl.program_id(0)
    i = pl.program_id(2)
    q = q_ref[...]
    head0 = lax.broadcasted_iota(I32, (tq, LANES), 1) < HEAD_DIM
    head0k = lax.broadcasted_iota(I32, (tk, LANES), 1) < HEAD_DIM
    qm = [jnp.where(head0, q, jnp.zeros_like(q)), jnp.where(head0, jnp.zeros_like(q), q)]
    m_scr[...] = jnp.full(m_scr.shape, NEG, F32)
    acc_scr[...] = jnp.zeros(acc_scr.shape, F32)
    n_col = tk // LANES
    n_sub = tq // tk
    n_off = i * n_sub

    def mask_dma(kt, slot, r0):
        return pltpu.make_async_copy(
            mask_ref.at[b, pl.ds(i * tq + r0, tq - r0), pl.ds(pl.multiple_of(kt * tk, tk), tk)],
            mbuf.at[slot, pl.ds(r0, tq - r0), :], sem.at[slot])

    def tile(kt, diag):
        r0 = 0 if diag is None else diag * tk
        nr = tq - r0
        k0 = pl.multiple_of(kt * tk, tk)
        kt_ = k_ref[pl.ds(k0, tk), :]
        vt = v_ref[pl.ds(k0, tk), :]
        one = jnp.ones_like(vt)
        vx = [jnp.where(head0k, vt, one), jnp.where(head0k, one, vt)]
        if use_mask:
            slot = kt & 1
            if diag is None:
                mask_dma(kt + 1, 1 - slot, 0).start()
            elif diag + 1 < n_sub:
                mask_dma(kt + 1, 1 - slot, (diag + 1) * tk).start()
            mask_dma(kt, slot, r0).wait()
            extra = mbuf[slot, pl.ds(r0, nr), :].astype(F32)
        ss = []
        for j in range(2):
            s = _dot_nt(qm[j][r0:, :], kt_)
            if use_mask:
                s = s + extra
            else:
                s = s + nb_ref[0, 0, j:j + 1, pl.ds(k0, tk)]
                if diag is not None:
                    rr = lax.broadcasted_iota(I32, (nr, tk), 0)
                    cc = lax.broadcasted_iota(I32, (nr, tk), 1)
                    s = jnp.where(cc <= rr, s, NEG)
            ss.append(s)
        ps = []
        for j in range(2):
            sc = [ss[j][:, c * LANES:(c + 1) * LANES] for c in range(n_col)]
            part = sc[0]
            for c in range(1, n_col):
                part = jnp.maximum(part, sc[c])
            m_old = m_scr[j, r0:, :]
            m_new = jnp.maximum(m_old, jnp.max(part, axis=1, keepdims=True))
            alpha = jnp.exp2(m_old - m_new)
            ps.append((alpha, jnp.concatenate([jnp.exp2(c_ - m_new) for c_ in sc], axis=1).astype(BF16)))
            m_scr[j, r0:, :] = m_new
        for j in range(2):
            alpha, p = ps[j]
            acc_scr[j, r0:, :] = alpha * acc_scr[j, r0:, :] + jnp.dot(p, vx[j], preferred_element_type=F32)

    def body(kt, carry):
        tile(kt, None)
        return carry

    if use_mask:
        mask_dma(0, 0, 0).start()
    lax.fori_loop(0, n_off, body, 0)
    for d in range(n_sub):
        tile(n_off + d, d)

    a0 = acc_scr[0]
    a1 = acc_scr[1]
    o0 = a0 / pltpu.roll(a0, HEAD_DIM, axis=1)
    o1 = a1 / pltpu.roll(a1, HEAD_DIM, axis=1)
    o_ref[...] = jnp.where(head0, o0, o1).astype(o_ref.dtype)


def _flash(q, k, v, bsz, seq, tq, tk, nb=None, mask=None):
    nq = seq // tq
    n_hp = q.shape[1] // LANES
    use_mask = mask is not None
    in_specs = [pl.BlockSpec((tq, LANES), lambda b, h, i: (b * nq + i, h)),
                pl.BlockSpec((seq, LANES), lambda b, h, i: (b, h)),
                pl.BlockSpec((seq, LANES), lambda b, h, i: (b, h))]
    scratch = [pltpu.VMEM((2, tq, LANES), F32), pltpu.VMEM((2, tq, LANES), F32)]
    if use_mask:
        in_specs.append(pl.BlockSpec(memory_space=pl.ANY))
        scratch += [pltpu.VMEM((2, tq, tk), BF16), pltpu.SemaphoreType.DMA((2,))]
        extra = mask
    else:
        in_specs.append(pl.BlockSpec((1, 1, 2, seq), lambda b, h, i: (b, h, 0, 0)))
        extra = nb
    kern = functools.partial(_flash_kernel, tq=tq, tk=tk, use_mask=use_mask)
    return pl.pallas_call(
        kern, grid=(bsz, n_hp, nq), in_specs=in_specs,
        out_specs=pl.BlockSpec((tq, LANES), lambda b, h, i: (b * nq + i, h)),
        out_shape=jax.ShapeDtypeStruct(q.shape, BF16),
        scratch_shapes=scratch,
        compiler_params=_params("parallel", "parallel", "parallel"))(q, k, v, extra)


def _hgrn_kernel(q_ref, zf_ref, v_ref, g_ref, lb_ref, ng_ref, o_ref, st_scr, oi_scr, *, tt):
    r = HGRN_SUB
    half = 256

    @pl.when(pl.program_id(1) == 0)
    def _():
        st_scr[...] = jnp.zeros(st_scr.shape, F32)

    q = q_ref[...]
    zf = zf_ref[...]
    v = v_ref[...]
    ls = jnp.minimum(zf, 0.0) - jnp.log1p(jnp.exp(-jnp.abs(zf)))
    a = lb_ref[0:1, :]
    y = lb_ref[1:2, :] + ls
    lf = jnp.maximum(a, y) + jnp.log1p(jnp.exp(-jnp.abs(a - y)))
    kk = lb_ref[2:3, :] * (1.0 / (1.0 + jnp.exp(zf)))

    rin = lax.broadcasted_iota(I32, (tt, 1), 0) & (r - 1)
    b = lf
    sh = 1
    while sh < r:
        b = b + jnp.where(rin >= sh, pltpu.roll(b, sh, axis=0), 0.0)
        sh *= 2

    rr = lax.broadcasted_iota(I32, (half, half), 0) // HEAD_DIM
    cc = lax.broadcasted_iota(I32, (half, half), 1) // HEAD_DIM
    same_head = rr == cc
    ones_bd = jnp.where(same_head, 1.0, 0.0).astype(BF16)
    bd_mask = jnp.where(same_head, 1.0, 0.0).astype(F32)

    o = jnp.zeros((tt, 2 * half), F32)
    for off in range(r):
        if off == 0:
            e = q * kk
            vs = v
        else:
            e = q * pltpu.roll(kk, off, axis=0) * jnp.exp(b - pltpu.roll(b, off, axis=0))
            e = jnp.where(rin >= off, e, 0.0)
            vs = pltpu.roll(v, off, axis=0)
        eb = e.astype(BF16)
        sc = jnp.concatenate(
            [jnp.dot(eb[:, 0:half], ones_bd, preferred_element_type=F32),
             jnp.dot(eb[:, half:], ones_bd, preferred_element_type=F32)], axis=1)
        o = o + sc * vs

    qe = (q * jnp.exp(b)).astype(BF16)
    vb = v.astype(BF16)
    for c in range(tt // r):
        r0 = c * r
        bl = b[r0 + r - 1:r0 + r, :]
        kd = (kk[r0:r0 + r, :] * jnp.exp(bl - b[r0:r0 + r, :])).astype(BF16)
        dec = jnp.exp(bl)
        for h2 in range(2):
            lo = h2 * half
            st = st_scr[h2]
            oi_scr[r0:r0 + r, lo:lo + half] = _dot_nt(qe[r0:r0 + r, lo:lo + half], st.astype(BF16))
            upd = _dot_tn(vb[r0:r0 + r, lo:lo + half], kd[:, lo:lo + half])
            st_scr[h2] = st * dec[:, lo:lo + half] + upd * bd_mask

    o = o + oi_scr[...]
    o2 = o * o
    ones_f = bd_mask
    ms = jnp.concatenate(
        [jnp.dot(o2[:, 0:half], ones_f, preferred_element_type=F32, precision=lax.Precision.HIGHEST),
         jnp.dot(o2[:, half:], ones_f, preferred_element_type=F32, precision=lax.Precision.HIGHEST)],
        axis=1) * (1.0 / HEAD_DIM)
    on = o * lax.rsqrt(ms + RMS_EPS) * ng_ref[...] * _sigmoid(g_ref[...])
    o_ref[...] = on.astype(o_ref.dtype)


def _hgrn(zc, lbp, ng, bsz, seq, tt):
    n = zc.shape[0]
    nt = seq // tt
    col = lambda cidx: pl.BlockSpec((tt, 512), lambda b, j: (b * nt + j, cidx))
    kern = functools.partial(_hgrn_kernel, tt=tt)
    return pl.pallas_call(
        kern, grid=(bsz, nt),
        in_specs=[col(0), col(1), col(2), col(3),
                  pl.BlockSpec((3, 512), lambda b, j: (0, 0)),
                  pl.BlockSpec((1, 512), lambda b, j: (0, 0))],
        out_specs=pl.BlockSpec((tt, 512), lambda b, j: (b * nt + j, 0)),
        out_shape=jax.ShapeDtypeStruct((n, 512), BF16),
        scratch_shapes=[pltpu.VMEM((2, 256, 256), F32), pltpu.VMEM((tt, 512), F32)],
        compiler_params=_params("parallel", "arbitrary"))(zc, zc, zc, zc, lbp, ng)


def _layer_norm(y, g, b):
    mu = jnp.mean(y, axis=-1, keepdims=True)
    d = y - mu
    var = jnp.mean(d * d, axis=-1, keepdims=True)
    return d * lax.rsqrt(var + LN_EPS) * g + b


def _merge_kernel(oa_ref, ob_ref, oc_ref, gt_ref, x_ref, wpa_ref, wpb_ref, wpc_ref, wout_ref,
                  g_ref, b_ref, wr_ref, br_ref, x1_ref, xb_ref, ti_ref, gate_ref, *, alpha):
    d = x_ref.shape[1]
    pa = jnp.dot(oa_ref[...], wpa_ref[...], preferred_element_type=F32)
    pb = jnp.dot(ob_ref[...], wpb_ref[...], preferred_element_type=F32)
    pc = jnp.dot(oc_ref[...], wpc_ref[...], preferred_element_type=F32)
    merged = (_sigmoid(gt_ref[:, 0:d]) * pa + _sigmoid(gt_ref[:, d:2 * d]) * pb
              + _sigmoid(gt_ref[:, 2 * d:3 * d]) * pc)
    mix = jnp.dot(merged.astype(BF16), wout_ref[...], preferred_element_type=F32)
    x1 = _layer_norm(alpha * x_ref[...] + mix, g_ref[...], b_ref[...])
    x1_ref[...] = x1
    xb_ref[...] = x1.astype(BF16)

    logits = jnp.dot(x1, wr_ref[...], preferred_element_type=F32,
                     precision=lax.Precision.HIGHEST) + br_ref[...]
    lane = lax.broadcasted_iota(I32, logits.shape, 1)
    topi = jnp.zeros(logits.shape, I32)
    topv = jnp.full(logits.shape, NEG, F32)
    for k in range(TOP_K):
        m = jnp.max(logits, axis=1, keepdims=True)
        idx = jnp.min(jnp.where(logits == m, lane, LANES), axis=1, keepdims=True)
        topi = jnp.where(lane == k, idx, topi)
        topv = jnp.where(lane == k, m, topv)
        logits = jnp.where(lane == idx, -jnp.inf, logits)
    e = jnp.where(lane < TOP_K, jnp.exp(topv - jnp.max(topv, axis=1, keepdims=True)), 0.0)
    ti_ref[...] = topi
    gate_ref[...] = e / jnp.sum(e, axis=1, keepdims=True)


def _merge(oa, ob, oc, gt, x2, wpa, wpb, wpc, wout, g, b, wr, br, alpha, tm):
    n, d = x2.shape
    row = lambda w: pl.BlockSpec((tm, w), lambda i: (i, 0))
    full = lambda a: pl.BlockSpec(a.shape, lambda i: (0,) * a.ndim)
    kern = functools.partial(_merge_kernel, alpha=alpha)
    return pl.pallas_call(
        kern, grid=(n // tm,),
        in_specs=[row(512), row(512), row(512), row(3 * d), row(d),
                  full(wpa), full(wpb), full(wpc), full(wout), full(g), full(b), full(wr), full(br)],
        out_specs=[row(d), row(d), row(LANES), row(LANES)],
        out_shape=[jax.ShapeDtypeStruct((n, d), F32), jax.ShapeDtypeStruct((n, d), BF16),
                   jax.ShapeDtypeStruct((n, LANES), I32), jax.ShapeDtypeStruct((n, LANES), F32)],
        compiler_params=_params("parallel"))(oa, ob, oc, gt, x2, wpa, wpb, wpc, wout, g, b, wr, br)


def _moe_kernel(be_ref, nu_ref, xs_ref, wgu_ref, bgu_ref, wd_ref, bd_ref, y_ref, wgu_bf, wd_bf):
    i = pl.program_id(0)
    de = wd_ref.shape[1]

    @pl.when((i == 0) | (be_ref[i] != be_ref[jnp.maximum(i - 1, 0)]))
    def _():
        wgu_bf[...] = wgu_ref[0].astype(BF16)
        wd_bf[...] = wd_ref[0].astype(BF16)

    @pl.when(i < nu_ref[0])
    def _():
        gu = jnp.dot(xs_ref[...], wgu_bf[...], preferred_element_type=F32) + bgu_ref[0]
        gate = jnp.minimum(gu[:, 0:de], SWIGLU_LIMIT)
        up = jnp.clip(gu[:, de:], -SWIGLU_LIMIT, SWIGLU_LIMIT)
        act = gate * _sigmoid(SWIGLU_ALPHA * gate) * (up + 1.0)
        y = jnp.dot(act.astype(BF16), wd_bf[...], preferred_element_type=F32) + bd_ref[0]
        y_ref[...] = y.astype(y_ref.dtype)

    @pl.when(i >= nu_ref[0])
    def _():
        y_ref[...] = jnp.zeros(y_ref.shape, y_ref.dtype)


def _moe_experts(layer, blk_e, n_used, xs, wgu, bgu, wd, bd):
    n_slots, d = xs.shape
    n_blocks = n_slots // MOE_BLOCK
    _, _, n2 = wgu.shape
    de = wd.shape[1]
    base = layer * N_EXPERTS
    grid_spec = pltpu.PrefetchScalarGridSpec(
        num_scalar_prefetch=2, grid=(n_blocks,),
        in_specs=[pl.BlockSpec((MOE_BLOCK, d), lambda i, be, nu: (i, 0)),
                  pl.BlockSpec((1, d, n2), lambda i, be, nu: (base + be[i], 0, 0)),
                  pl.BlockSpec((1, 1, n2), lambda i, be, nu: (base + be[i], 0, 0)),
                  pl.BlockSpec((1, de, d), lambda i, be, nu: (base + be[i], 0, 0)),
                  pl.BlockSpec((1, 1, d), lambda i, be, nu: (base + be[i], 0, 0))],
        out_specs=pl.BlockSpec((MOE_BLOCK, d), lambda i, be, nu: (i, 0)),
        scratch_shapes=[pltpu.VMEM((d, n2), BF16), pltpu.VMEM((de, d), BF16)])
    return pl.pallas_call(
        _moe_kernel, grid_spec=grid_spec,
        out_shape=jax.ShapeDtypeStruct((n_slots, d), BF16),
        compiler_params=_params("arbitrary"))(blk_e, n_used, xs, wgu, bgu, wd, bd)


def _combine_kernel(yg_ref, gate_ref, x1_ref, g_ref, b_ref, o_ref, *, alpha):
    gates = gate_ref[...]
    acc = alpha * x1_ref[...]
    for k in range(TOP_K):
        acc = acc + gates[:, k:k + 1] * yg_ref[k].astype(F32)
    o_ref[...] = _layer_norm(acc, g_ref[...], b_ref[...])


def _combine(yg, gates, x1, g, b, alpha, tm):
    n, d = x1.shape
    kern = functools.partial(_combine_kernel, alpha=alpha)
    return pl.pallas_call(
        kern, grid=(n // tm,),
        in_specs=[pl.BlockSpec((TOP_K, tm, d), lambda i: (0, i, 0)),
                  pl.BlockSpec((tm, LANES), lambda i: (i, 0)),
                  pl.BlockSpec((tm, d), lambda i: (i, 0)),
                  pl.BlockSpec((1, d), lambda i: (0, 0)),
                  pl.BlockSpec((1, d), lambda i: (0, 0))],
        out_specs=pl.BlockSpec((tm, d), lambda i: (i, 0)),
        out_shape=jax.ShapeDtypeStruct((n, d), F32),
        compiler_params=_params("parallel"))(yg, gates, x1, g, b)


def _rope_tables(seq, width):
    half = HEAD_DIM // 2
    inv = ROPE_THETA ** (-jnp.arange(half, dtype=F32) / half)
    ang = jnp.arange(seq, dtype=F32)[:, None] * inv[None, :]
    cos = jnp.concatenate([jnp.cos(ang), jnp.cos(ang)], axis=1)
    sin = jnp.concatenate([-jnp.sin(ang), jnp.sin(ang)], axis=1)
    reps = width // HEAD_DIM
    return jnp.tile(cos, (1, reps)), jnp.tile(sin, (1, reps))


def _split_w_in(w):
    seg = lambda a, b: w[:, _OFF[a]:_OFF[b]]
    w_dsa = seg(0, 3)
    small = jnp.concatenate([seg(5, 6), seg(9, 10), jnp.zeros((w.shape[0], LANES - 12), w.dtype)], axis=1)
    w_idx = jnp.concatenate([seg(3, 4)] + [seg(4, 5)] * IDX_HEADS + [small], axis=1)
    w_fox = seg(6, 9)
    w_hgrn = seg(10, 14)
    w_gate = seg(14, 15)
    return [t.astype(BF16) for t in (w_dsa, w_idx, w_fox, w_hgrn, w_gate)]


def _route(top_idx, n_tok):
    n_assign = n_tok * TOP_K
    flat_e = top_idx.reshape(-1)
    experts = jnp.arange(N_EXPERTS, dtype=I32)
    onehot = (flat_e[None, :] == experts[:, None]).astype(I32)
    csum = jnp.cumsum(onehot, axis=1)
    counts = csum[:, -1]
    rank = jnp.sum(csum * onehot, axis=0) - 1
    padded = (counts + MOE_BLOCK - 1) // MOE_BLOCK * MOE_BLOCK
    pad_end = jnp.cumsum(padded)
    pad_start = pad_end - padded
    start = jnp.cumsum(counts) - counts
    slot_of = pad_start[flat_e] + rank
    n_blocks = -(-(n_assign + N_EXPERTS * (MOE_BLOCK - 1)) // MOE_BLOCK)
    blk_start = jnp.arange(n_blocks, dtype=I32) * MOE_BLOCK
    blk_e = jnp.sum((pad_end[None, :] <= blk_start[:, None]).astype(I32), axis=1)
    blk_e = jnp.minimum(blk_e, N_EXPERTS - 1).astype(I32)
    n_used = (pad_end[-1] // MOE_BLOCK).astype(I32).reshape(1)
    order = jnp.argsort(flat_e).astype(I32)
    slot = jnp.arange(n_blocks * MOE_BLOCK, dtype=I32)
    slot_e = jnp.repeat(blk_e, MOE_BLOCK)
    r = slot - pad_start[slot_e]
    src = jnp.clip(start[slot_e] + r, 0, n_assign - 1)
    slot_tok = jnp.where(r < counts[slot_e], order[src] // TOP_K, 0)
    return slot_tok, blk_e, n_used, slot_of.reshape(n_tok, TOP_K).T


def _layer(layer, x2, bsz, seq, alpha, w_in, b_fox, lb, ng, wpa, wpb, wpc, wout, g1, b1,
           wr, br, wgu, bgu, wd, bd, g2, b2, cos, sin):
    n, d = x2.shape
    tm = min(512, seq)
    w_dsa, w_idx, w_fox, w_hgrn, w_gate = _split_w_in(w_in)
    nst = seq // tm
    tab = lambda arr: (arr, pl.BlockSpec((tm, 512), lambda i: (i % nst, 0)))

    aq, ak, av = _proj_call(_proj_dsa_kernel, x2, w_dsa, [tab(cos), tab(sin)],
                            [(512, BF16)] * 3, tm)
    iq, ik4, small = _proj_call(_proj_idx_kernel, x2, w_idx, [tab(cos), tab(sin)],
                                [(256, BF16), (256, BF16), (LANES, F32)], tm)
    fq, fk, fv = _proj_call(_proj_fox_kernel, x2, w_fox, [], [(512, BF16)] * 3, tm)
    (zc,) = _proj_call(_proj_plain_kernel, x2, w_hgrn, [], [(2048, F32)], tm)
    (gt,) = _proj_call(_proj_plain_kernel, x2, w_gate, [], [(3 * d, F32)], tm)

    tq_att, tk_att = min(FLASH_TQ, seq), min(FLASH_TK, seq)
    iw_t = small[:, 0:8].reshape(bsz, seq, 8).transpose(0, 2, 1)
    mask = _dsa_index(iq, ik4, iw_t, bsz, seq, min(256, seq), min(512, seq))
    o_a = _flash(aq, ak, av, bsz, seq, tq_att, tk_att, mask=mask)

    ff_t = small[:, 4:12].reshape(bsz, seq, 8).transpose(0, 2, 1)
    nb = _fox_bias(ff_t, b_fox).reshape(bsz, 4, 2, seq)
    o_b = _flash(fq, fk, fv, bsz, seq, tq_att, tk_att, nb=nb)

    lbp = jnp.stack([jnp.log(lb), jnp.log1p(-lb), 1.0 - lb]).astype(F32)
    o_c = _hgrn(zc, lbp, ng.reshape(1, -1).astype(F32), bsz, seq, min(256, seq))

    wr_p = jnp.zeros((d, LANES), F32).at[:, :N_EXPERTS].set(wr.astype(F32))
    br_p = jnp.full((1, LANES), NEG, F32).at[0, :N_EXPERTS].set(br.astype(F32))
    x1, x1b, topi, gates = _merge(
        o_a, o_b, o_c, gt, x2, wpa.astype(BF16), wpb.astype(BF16), wpc.astype(BF16),
        wout.astype(BF16), g1.reshape(1, d), b1.reshape(1, d), wr_p, br_p, alpha, min(256, n))

    slot_tok, blk_e, n_used, slot_of = _route(topi[:, :TOP_K], n)
    xs = x1b[slot_tok]
    y_slots = _moe_experts(layer, blk_e, n_used, xs, wgu, bgu, wd, bd)
    yg = y_slots[slot_of]
    return _combine(yg, gates, x1, g2.reshape(1, d), b2.reshape(1, d), alpha, min(256, n))


def kernel(x, w_in, b_fox_f, hgrn_lb_logits, hgrn_norm_g, w_branch_a, w_branch_b, w_branch_c, w_out, ln1_g, ln1_b, w_router, b_router, w_gu, b_gu, w_down, b_down, ln2_g, ln2_b):
    bsz, seq, d = x.shape
    depth = w_in.shape[0]
    alpha = (2 * depth) ** 0.25
    p = jax.nn.softmax(hgrn_lb_logits.astype(F32), axis=0)
    lbs = jnp.cumsum(p, axis=0)
    lbs = lbs - lbs[0]
    cos, sin = _rope_tables(seq, 512)
    x2 = x.reshape(bsz * seq, d)
    n_e = depth * w_gu.shape[1]
    wgu = w_gu.reshape(n_e, d, w_gu.shape[3])
    bgu = b_gu.reshape(n_e, 1, b_gu.shape[2])
    wd = w_down.reshape(n_e, w_down.shape[2], d)
    bd = b_down.reshape(n_e, 1, d)
    for l in range(depth):
        x2 = _layer(l, x2, bsz, seq, alpha, w_in[l], b_fox_f[l], lbs[l], hgrn_norm_g[l],
                    w_branch_a[l], w_branch_b[l], w_branch_c[l], w_out[l], ln1_g[l], ln1_b[l],
                    w_router[l], b_router[l], wgu, bgu, wd, bd, ln2_g[l], ln2_b[l], cos, sin)
    return x2.reshape(bsz, seq, d)
```

```python
import functools
import math

import jax
import jax.numpy as jnp
import numpy as np
from jax import lax
from jax.experimental import pallas as pl
from jax.experimental.pallas import tpu as pltpu

F32 = jnp.float32
BF16 = jnp.bfloat16
I32 = jnp.int32

CHUNK = 64
HEAD_DIM = 64
ROPE_THETA = 10000.0
DSA_TOPK_MAX = 256
IDX_HEADS = 4
N_EXPERTS = 32
TOP_K = 4
SWIGLU_LIMIT = 7.0
SWIGLU_ALPHA = 1.702
MOE_BLOCK = 256
LN_EPS = 1e-5
RMS_EPS = 1e-6

LOG2E = 1.4426950408889634
NEG = -1e30
INT_MIN = -2147483648
INT_MAX = 2147483647
LANES = 128
DSA_PROBES_PER_CHECK = 3
FLASH_TQ = 2048
FLASH_TK = 512
HGRN_SUB = 16
VMEM_LIMIT = 56 * 1024 * 1024

IN_SPLITS = (512, 512, 512, 256, 64, 4, 512, 512, 512, 8, 512, 512, 512, 512, 3072)
_OFF = tuple(int(v) for v in np.cumsum((0,) + IN_SPLITS))


def _params(*sem):
    return pltpu.CompilerParams(dimension_semantics=sem, vmem_limit_bytes=VMEM_LIMIT)


def _dot_nt(a, b):
    return lax.dot_general(a, b, (((1,), (1,)), ((), ())), preferred_element_type=F32)


def _dot_tn(a, b):
    return lax.dot_general(a, b, (((0,), (0,)), ((), ())), preferred_element_type=F32)


def _sigmoid(t):
    return 1.0 / (1.0 + jnp.exp(-t))


def _rope_tile(t, cos, sin):
    w = t.shape[1]
    lane_d = lax.broadcasted_iota(I32, t.shape, 1) & (HEAD_DIM - 1)
    partner = jnp.where(lane_d < HEAD_DIM // 2,
                        pltpu.roll(t, w - HEAD_DIM // 2, axis=1),
                        pltpu.roll(t, HEAD_DIM // 2, axis=1))
    return t * cos + partner * sin


def _proj_dsa_kernel(x_ref, w_ref, cos_ref, sin_ref, q_ref, k_ref, v_ref):
    z = jnp.dot(x_ref[...].astype(BF16), w_ref[...], preferred_element_type=F32)
    cos = cos_ref[...]
    sin = sin_ref[...]
    q_ref[...] = (_rope_tile(z[:, 0:512], cos, sin) * (HEAD_DIM ** -0.5 * LOG2E)).astype(BF16)
    k_ref[...] = _rope_tile(z[:, 512:1024], cos, sin).astype(BF16)
    v_ref[...] = z[:, 1024:1536].astype(BF16)


def _proj_idx_kernel(x_ref, w_ref, cos_ref, sin_ref, iq_ref, ik_ref, sm_ref):
    z = jnp.dot(x_ref[...].astype(BF16), w_ref[...], preferred_element_type=F32)
    cos = cos_ref[...][:, 0:256]
    sin = sin_ref[...][:, 0:256]
    iq_ref[...] = (_rope_tile(z[:, 0:256], cos, sin) * (HEAD_DIM ** -0.5)).astype(BF16)
    ik_ref[...] = _rope_tile(z[:, 256:512], cos, sin).astype(BF16)
    sm_ref[...] = z[:, 512:640]


def _proj_fox_kernel(x_ref, w_ref, q_ref, k_ref, v_ref):
    z = jnp.dot(x_ref[...].astype(BF16), w_ref[...], preferred_element_type=F32)
    q_ref[...] = (z[:, 0:512] * (HEAD_DIM ** -0.5 * LOG2E)).astype(BF16)
    k_ref[...] = z[:, 512:1024].astype(BF16)
    v_ref[...] = z[:, 1024:1536].astype(BF16)


def _proj_plain_kernel(x_ref, w_ref, o_ref):
    o_ref[...] = jnp.dot(x_ref[...].astype(BF16), w_ref[...], preferred_element_type=F32)


def _proj_call(kernel_fn, x2, w, extra, outs, tm):
    n, d = x2.shape
    nout = w.shape[1]
    in_specs = [pl.BlockSpec((tm, d), lambda i: (i, 0)),
                pl.BlockSpec((d, nout), lambda i: (0, 0))]
    args = [x2, w]
    for arr, spec in extra:
        in_specs.append(spec)
        args.append(arr)
    out_shape = [jax.ShapeDtypeStruct((n, wdt), dt) for wdt, dt in outs]
    out_specs = [pl.BlockSpec((tm, wdt), lambda i: (i, 0)) for wdt, _ in outs]
    return pl.pallas_call(
        kernel_fn, grid=(n // tm,), in_specs=in_specs, out_specs=out_specs,
        out_shape=out_shape, compiler_params=_params("parallel"))(*args)


def _fox_bias_kernel(ff_ref, b_ref, nb_ref):
    s = ff_ref.shape[2]
    z = ff_ref[0] + b_ref[...]
    lf = jnp.minimum(z, 0.0) - jnp.log1p(jnp.exp(-jnp.abs(z)))
    r = lax.broadcasted_iota(I32, (LANES, LANES), 0)
    c = lax.broadcasted_iota(I32, (LANES, LANES), 1)
    upper = jnp.where(r <= c, 1.0, 0.0).astype(F32)
    carry = jnp.zeros((lf.shape[0], 1), F32)
    for t in range(s // LANES):
        blk = lf[:, t * LANES:(t + 1) * LANES]
        cs = jnp.dot(blk, upper, preferred_element_type=F32,
                     precision=lax.Precision.HIGHEST) + carry
        nb_ref[0, :, t * LANES:(t + 1) * LANES] = cs * (-LOG2E)
        carry = cs[:, LANES - 1:LANES]


def _fox_bias(ff_t, b_fox):
    bsz, h, s = ff_t.shape
    return pl.pallas_call(
        _fox_bias_kernel, grid=(bsz,),
        in_specs=[pl.BlockSpec((1, h, s), lambda b: (b, 0, 0)),
                  pl.BlockSpec((h, 1), lambda b: (0, 0))],
        out_specs=pl.BlockSpec((1, h, s), lambda b: (b, 0, 0)),
        out_shape=jax.ShapeDtypeStruct((bsz, h, s), F32),
        compiler_params=_params("parallel"))(ff_t, b_fox.reshape(h, 1).astype(F32))


def _dsa_index_kernel(iq_ref, ik_ref, iwt_ref, bias_ref, keys_scr, thr_scr, j_scr, *, tq, tk, seq, ksel, cnt_tiles):
    i = pl.program_id(1)
    n_kt = ((i + 1) * tq + tk - 1) // tk
    q_pos = i * tq + lax.broadcasted_iota(I32, (1, tq), 1)
    adm_end = (q_pos // CHUNK + 1) * CHUNK

    iq = iq_ref[...]
    lane = lax.broadcasted_iota(I32, iq.shape, 1)
    iqm = [jnp.where((lane >= HEAD_DIM * h) & (lane < HEAD_DIM * (h + 1)), iq, jnp.zeros_like(iq))
           for h in range(IDX_HEADS)]
    wts = [iwt_ref[0, h:h + 1, :] * (IDX_HEADS ** -0.5) for h in range(IDX_HEADS)]

    def key_idx(k0):
        return k0 + lax.broadcasted_iota(I32, (tk, 1), 0)

    def to_key(v):
        bits = pltpu.bitcast(v, I32)
        sign = bits >> 31
        return (bits ^ (sign & 0x7FFFFFFF)) - sign

    def from_key(k):
        sign = k >> 31
        return pltpu.bitcast((k + sign) ^ (sign & 0x7FFFFFFF), F32)

    def score_tile(kt, carry):
        kmax, kmin = carry
        k0 = pl.multiple_of(kt * tk, tk)
        ikt = ik_ref[pl.ds(k0, tk), :]
        sc = jnp.zeros((tk, tq), F32)
        for h in range(IDX_HEADS):
            sc = sc + jnp.maximum(_dot_nt(ikt, iqm[h]), 0.0) * wts[h]
        key = to_key(sc)
        adm = key_idx(k0) < adm_end
        k_lo = jnp.where(adm, key, INT_MIN)
        k_hi = jnp.where(adm, key, INT_MAX)
        keys_scr[pl.ds(k0, tk), :] = k_lo
        for g in range(tk // 8):
            kmax = jnp.maximum(kmax, k_lo[g * 8:(g + 1) * 8, :])
            kmin = jnp.minimum(kmin, k_hi[g * 8:(g + 1) * 8, :])
        return kmax, kmin

    kmax, kmin = lax.fori_loop(0, n_kt, score_tile,
                               (jnp.full((8, tq), INT_MIN, I32), jnp.full((8, tq), INT_MAX, I32)))
    kmax = jnp.max(kmax, axis=0, keepdims=True)
    kmin = jnp.min(kmin, axis=0, keepdims=True)

    kf = float(ksel)
    n_acc = 4
    grp = 8 * n_acc

    tc = cnt_tiles * tk
    n_ct = (n_kt + cnt_tiles - 1) // cnt_tiles

    def pad_tile(kt, carry):
        keys_scr[pl.ds(pl.multiple_of(kt * tk, tk), tk), :] = jnp.full((tk, tq), INT_MIN, I32)
        return carry

    lax.fori_loop(n_kt, n_ct * cnt_tiles, pad_tile, 0)

    def count(pred):
        def body(kt, accs):
            k0 = pl.multiple_of(kt * tc, tc)
            accs = list(accs)
            for g in range(tc // grp):
                kk = keys_scr[pl.ds(k0 + g * grp, grp), :]
                hit = jnp.where(pred(kk, k0 + g * grp), 1.0, 0.0)
                for a in range(n_acc):
                    accs[a] = accs[a] + hit[a * 8:(a + 1) * 8, :]
            return tuple(accs)
        accs = lax.fori_loop(0, n_ct, body, tuple(jnp.zeros((8, tq), F32) for _ in range(n_acc)))
        tot = accs[0]
        for a in range(1, n_acc):
            tot = tot + accs[a]
        return jnp.sum(tot, axis=0, keepdims=True)

    def count_ge(cand):
        return count(lambda kk, k0: kk >= cand)

    n_adm = adm_end.astype(F32)
    zero = jnp.zeros((1, tq), F32)
    one = jnp.ones((1, tq), F32)
    log_k = math.log(kf)
    done0 = jnp.where(n_adm <= kf, 1.0, 0.0)
    init = (jnp.int32(0), kmin, kmax + 1, n_adm, zero, jnp.full((1, tq), INT_MIN, I32), done0, zero,
            one, one, zero)

    def cond(st):
        return (st[0] < 264) & (jnp.min(st[6]) == 0.0)

    def step(st):
        it, lo, hi, c_lo, c_hi, thr, done, tie, w_lo, w_hi, last = st
        lo_v = from_key(lo)
        hi_v = from_key(hi)
        f_lo = (jnp.log(c_lo + 0.5) - log_k) * w_lo
        f_hi = (log_k - jnp.log(c_hi + 0.5)) * w_hi
        cand = to_key(lo_v + (hi_v - lo_v) * (f_lo / (f_lo + f_hi)))
        cand = jnp.where(((it - 3) & 7) == 7, (lo >> 1) + (hi >> 1) + (lo & hi & 1), cand)
        cand = jnp.where(it == 0, kmax, cand)
        cand = jnp.where(it == 1, 0, cand)
        cand = jnp.where(it == 2, 1, cand)
        cand = jnp.minimum(jnp.maximum(cand, lo + 1), hi - 1)
        c = count_ge(cand)
        active = done == 0.0
        up = active & (c >= kf)
        down = active & (c < kf)
        lo = jnp.where(up, cand, lo)
        c_lo = jnp.where(up, c, c_lo)
        hi = jnp.where(down, cand, hi)
        c_hi = jnp.where(down, c, c_hi)
        w_hi = jnp.where(up & (last > 0.0), 0.5 * w_hi, jnp.where(down, 1.0, w_hi))
        w_lo = jnp.where(down & (last < 0.0), 0.5 * w_lo, jnp.where(up, 1.0, w_lo))
        last = jnp.where(up, 1.0, jnp.where(down, -1.0, last))
        hit = active & (c == kf)
        conv = active & jnp.logical_not(hit) & (hi - 1 <= lo)
        thr = jnp.where(hit, cand - 1, jnp.where(conv, lo, thr))
        tie = jnp.where(conv, 1.0, tie)
        done = jnp.where(hit | conv, 1.0, done)
        return it + 1, lo, hi, c_lo, c_hi, thr, done, tie, w_lo, w_hi, last

    def steps(st):
        for _ in range(DSA_PROBES_PER_CHECK):
            st = step(st)
        return st

    st = lax.while_loop(cond, steps, init)
    c_hi, thr, tie = st[4], st[5], st[7]
    thr_scr[...] = thr
    j_scr[...] = jnp.full((1, tq), -1, I32)

    @pl.when(jnp.max(tie) > 0.0)
    def _():
        need = kf - c_hi

        def idx_step(it, x):
            cand = x | jnp.left_shift(jnp.int32(1), 13 - it)
            rows = lax.broadcasted_iota(I32, (grp, 1), 0)
            cnt = count(lambda kk, k0: (kk == thr) & (k0 + rows < cand))
            return jnp.where(cnt < need, cand, x)

        x = lax.fori_loop(0, 14, idx_step, jnp.zeros((1, tq), I32))
        j_scr[...] = jnp.where(tie > 0.0, x, -1)

    thr = thr_scr[...]
    jmax = j_scr[...]

    def write_tile(kt, carry):
        k0 = pl.multiple_of(kt * tk, tk)
        kk = keys_scr[pl.ds(k0, tk), :]
        sel = (kk > thr) | ((kk == thr) & (key_idx(k0) <= jmax))
        bias_ref[0, :, pl.ds(k0, tk)] = jnp.where(sel, 0.0, NEG).T.astype(BF16)
        return carry

    lax.fori_loop(0, n_kt, write_tile, 0)

    def fill_tile(kt, carry):
        k0 = pl.multiple_of(kt * tk, tk)
        bias_ref[0, :, pl.ds(k0, tk)] = jnp.full((tq, tk), NEG, BF16)
        return carry

    lax.fori_loop(n_kt, seq // tk, fill_tile, 0)


def _dsa_index(iq, ik4, iw_t, bsz, seq, tq, tk, cnt_tiles=1):
    ksel = min(DSA_TOPK_MAX, seq // 4)
    nq = seq // tq
    kern = functools.partial(_dsa_index_kernel, tq=tq, tk=tk, seq=seq, ksel=ksel, cnt_tiles=cnt_tiles)
    return pl.pallas_call(
        kern, grid=(bsz, nq),
        in_specs=[pl.BlockSpec((tq, 256), lambda b, i: (b * nq + i, 0)),
                  pl.BlockSpec((seq, 256), lambda b, i: (b, 0)),
                  pl.BlockSpec((1, 8, tq), lambda b, i: (b, 0, i))],
        out_specs=pl.BlockSpec((1, tq, seq), lambda b, i: (b, i, 0)),
        out_shape=jax.ShapeDtypeStruct((bsz, seq, seq), BF16),
        scratch_shapes=[pltpu.VMEM((seq, tq), I32), pltpu.VMEM((1, tq), I32), pltpu.VMEM((1, tq), I32)],
        compiler_params=_params("parallel", "parallel"))(iq, ik4, iw_t)


def _flash_kernel(*refs, tq, tk, use_mask):
    if use_mask:
        q_ref, k_ref, v_ref, mask_ref, o_ref, m_scr, acc_scr, mbuf, sem = refs
        nb_ref = None
    else:
        q_ref, k_ref, v_ref, nb_ref, o_ref, m_scr, acc_scr = refs
        mask_ref = None
    b = pl.program_id(0)
    i = pl.program_id(2)
    q = q_ref[...]
    head0 = lax.broadcasted_iota(I32, (tq, LANES), 1) < HEAD_DIM
    head0k = lax.broadcasted_iota(I32, (tk, LANES), 1) < HEAD_DIM
    qm = [jnp.where(head0, q, jnp.zeros_like(q)), jnp.where(head0, jnp.zeros_like(q), q)]
    m_scr[...] = jnp.full(m_scr.shape, NEG, F32)
    acc_scr[...] = jnp.zeros(acc_scr.shape, F32)
    n_col = tk // LANES
    n_sub = tq // tk
    n_off = i * n_sub

    def mask_dma(kt, slot, r0):
        return pltpu.make_async_copy(
            mask_ref.at[b, pl.ds(i * tq + r0, tq - r0), pl.ds(pl.multiple_of(kt * tk, tk), tk)],
            mbuf.at[slot, pl.ds(r0, tq - r0), :], sem.at[slot])

    def tile(kt, diag):
        r0 = 0 if diag is None else diag * tk
        nr = tq - r0
        k0 = pl.multiple_of(kt * tk, tk)
        kt_ = k_ref[pl.ds(k0, tk), :]
        vt = v_ref[pl.ds(k0, tk), :]
        one = jnp.ones_like(vt)
        vx = [jnp.where(head0k, vt, one), jnp.where(head0k, one, vt)]
        if use_mask:
            slot = kt & 1
            if diag is None:
                mask_dma(kt + 1, 1 - slot, 0).start()
            elif diag + 1 < n_sub:
                mask_dma(kt + 1, 1 - slot, (diag + 1) * tk).start()
            mask_dma(kt, slot, r0).wait()
            extra = mbuf[slot, pl.ds(r0, nr), :].astype(F32)
        ss = []
        for j in range(2):
            s = _dot_nt(qm[j][r0:, :], kt_)
            if use_mask:
                s = s + extra
            else:
                s = s + nb_ref[0, 0, j:j + 1, pl.ds(k0, tk)]
                if diag is not None:
                    rr = lax.broadcasted_iota(I32, (nr, tk), 0)
                    cc = lax.broadcasted_iota(I32, (nr, tk), 1)
                    s = jnp.where(cc <= rr, s, NEG)
            ss.append(s)
        ps = []
        for j in range(2):
            sc = [ss[j][:, c * LANES:(c + 1) * LANES] for c in range(n_col)]
            part = sc[0]
            for c in range(1, n_col):
                part = jnp.maximum(part, sc[c])
            m_old = m_scr[j, r0:, :]
            m_new = jnp.maximum(m_old, jnp.max(part, axis=1, keepdims=True))
            alpha = jnp.exp2(m_old - m_new)
            ps.append((alpha, jnp.concatenate([jnp.exp2(c_ - m_new) for c_ in sc], axis=1).astype(BF16)))
            m_scr[j, r0:, :] = m_new
        for j in range(2):
            alpha, p = ps[j]
            acc_scr[j, r0:, :] = alpha * acc_scr[j, r0:, :] + jnp.dot(p, vx[j], preferred_element_type=F32)

    def body(kt, carry):
        tile(kt, None)
        return carry

    if use_mask:
        mask_dma(0, 0, 0).start()
    lax.fori_loop(0, n_off, body, 0)
    for d in range(n_sub):
        tile(n_off + d, d)

    a0 = acc_scr[0]
    a1 = acc_scr[1]
    o0 = a0 / pltpu.roll(a0, HEAD_DIM, axis=1)
    o1 = a1 / pltpu.roll(a1, HEAD_DIM, axis=1)
    o_ref[...] = jnp.where(head0, o0, o1).astype(o_ref.dtype)


def _flash(q, k, v, bsz, seq, tq, tk, nb=None, mask=None):
    nq = seq // tq
    n_hp = q.shape[1] // LANES
    use_mask = mask is not None
    in_specs = [pl.BlockSpec((tq, LANES), lambda b, h, i: (b * nq + i, h)),
                pl.BlockSpec((seq, LANES), lambda b, h, i: (b, h)),
                pl.BlockSpec((seq, LANES), lambda b, h, i: (b, h))]
    scratch = [pltpu.VMEM((2, tq, LANES), F32), pltpu.VMEM((2, tq, LANES), F32)]
    if use_mask:
        in_specs.append(pl.BlockSpec(memory_space=pl.ANY))
        scratch += [pltpu.VMEM((2, tq, tk), BF16), pltpu.SemaphoreType.DMA((2,))]
        extra = mask
    else:
        in_specs.append(pl.BlockSpec((1, 1, 2, seq), lambda b, h, i: (b, h, 0, 0)))
        extra = nb
    kern = functools.partial(_flash_kernel, tq=tq, tk=tk, use_mask=use_mask)
    return pl.pallas_call(
        kern, grid=(bsz, n_hp, nq), in_specs=in_specs,
        out_specs=pl.BlockSpec((tq, LANES), lambda b, h, i: (b * nq + i, h)),
        out_shape=jax.ShapeDtypeStruct(q.shape, BF16),
        scratch_shapes=scratch,
        compiler_params=_params("parallel", "parallel", "parallel"))(q, k, v, extra)


def _hgrn_kernel(q_ref, zf_ref, v_ref, g_ref, lb_ref, ng_ref, o_ref, st_scr, oi_scr, *, tt):
    r = HGRN_SUB
    half = 256

    @pl.when(pl.program_id(1) == 0)
    def _():
        st_scr[...] = jnp.zeros(st_scr.shape, F32)

    q = q_ref[...]
    zf = zf_ref[...]
    v = v_ref[...]
    ls = jnp.minimum(zf, 0.0) - jnp.log1p(jnp.exp(-jnp.abs(zf)))
    a = lb_ref[0:1, :]
    y = lb_ref[1:2, :] + ls
    lf = jnp.maximum(a, y) + jnp.log1p(jnp.exp(-jnp.abs(a - y)))
    kk = lb_ref[2:3, :] * (1.0 / (1.0 + jnp.exp(zf)))

    rin = lax.broadcasted_iota(I32, (tt, 1), 0) & (r - 1)
    b = lf
    sh = 1
    while sh < r:
        b = b + jnp.where(rin >= sh, pltpu.roll(b, sh, axis=0), 0.0)
        sh *= 2

    rr = lax.broadcasted_iota(I32, (half, half), 0) // HEAD_DIM
    cc = lax.broadcasted_iota(I32, (half, half), 1) // HEAD_DIM
    same_head = rr == cc
    ones_bd = jnp.where(same_head, 1.0, 0.0).astype(BF16)
    bd_mask = jnp.where(same_head, 1.0, 0.0).astype(F32)

    o = jnp.zeros((tt, 2 * half), F32)
    for off in range(r):
        if off == 0:
            e = q * kk
            vs = v
        else:
            e = q * pltpu.roll(kk, off, axis=0) * jnp.exp(b - pltpu.roll(b, off, axis=0))
            e = jnp.where(rin >= off, e, 0.0)
            vs = pltpu.roll(v, off, axis=0)
        eb = e.astype(BF16)
        sc = jnp.concatenate(
            [jnp.dot(eb[:, 0:half], ones_bd, preferred_element_type=F32),
             jnp.dot(eb[:, half:], ones_bd, preferred_element_type=F32)], axis=1)
        o = o + sc * vs

    qe = (q * jnp.exp(b)).astype(BF16)
    vb = v.astype(BF16)
    for c in range(tt // r):
        r0 = c * r
        bl = b[r0 + r - 1:r0 + r, :]
        kd = (kk[r0:r0 + r, :] * jnp.exp(bl - b[r0:r0 + r, :])).astype(BF16)
        dec = jnp.exp(bl)
        for h2 in range(2):
            lo = h2 * half
            st = st_scr[h2]
            oi_scr[r0:r0 + r, lo:lo + half] = _dot_nt(qe[r0:r0 + r, lo:lo + half], st.astype(BF16))
            upd = _dot_tn(vb[r0:r0 + r, lo:lo + half], kd[:, lo:lo + half])
            st_scr[h2] = st * dec[:, lo:lo + half] + upd * bd_mask

    o = o + oi_scr[...]
    o2 = o * o
    ones_f = bd_mask
    ms = jnp.concatenate(
        [jnp.dot(o2[:, 0:half], ones_f, preferred_element_type=F32, precision=lax.Precision.HIGHEST),
         jnp.dot(o2[:, half:], ones_f, preferred_element_type=F32, precision=lax.Precision.HIGHEST)],
        axis=1) * (1.0 / HEAD_DIM)
    on = o * lax.rsqrt(ms + RMS_EPS) * ng_ref[...] * _sigmoid(g_ref[...])
    o_ref[...] = on.astype(o_ref.dtype)


def _hgrn(zc, lbp, ng, bsz, seq, tt):
    n = zc.shape[0]
    nt = seq // tt
    col = lambda cidx: pl.BlockSpec((tt, 512), lambda b, j: (b * nt + j, cidx))
    kern = functools.partial(_hgrn_kernel, tt=tt)
    return pl.pallas_call(
        kern, grid=(bsz, nt),
        in_specs=[col(0), col(1), col(2), col(3),
                  pl.BlockSpec((3, 512), lambda b, j: (0, 0)),
                  pl.BlockSpec((1, 512), lambda b, j: (0, 0))],
        out_specs=pl.BlockSpec((tt, 512), lambda b, j: (b * nt + j, 0)),
        out_shape=jax.ShapeDtypeStruct((n, 512), BF16),
        scratch_shapes=[pltpu.VMEM((2, 256, 256), F32), pltpu.VMEM((tt, 512), F32)],
        compiler_params=_params("parallel", "arbitrary"))(zc, zc, zc, zc, lbp, ng)


def _layer_norm(y, g, b):
    mu = jnp.mean(y, axis=-1, keepdims=True)
    d = y - mu
    var = jnp.mean(d * d, axis=-1, keepdims=True)
    return d * lax.rsqrt(var + LN_EPS) * g + b


def _merge_kernel(oa_ref, ob_ref, oc_ref, gt_ref, x_ref, wpa_ref, wpb_ref, wpc_ref, wout_ref,
                  g_ref, b_ref, wr_ref, br_ref, x1_ref, xb_ref, ti_ref, gate_ref, *, alpha):
    d = x_ref.shape[1]
    pa = jnp.dot(oa_ref[...], wpa_ref[...], preferred_element_type=F32)
    pb = jnp.dot(ob_ref[...], wpb_ref[...], preferred_element_type=F32)
    pc = jnp.dot(oc_ref[...], wpc_ref[...], preferred_element_type=F32)
    merged = (_sigmoid(gt_ref[:, 0:d]) * pa + _sigmoid(gt_ref[:, d:2 * d]) * pb
              + _sigmoid(gt_ref[:, 2 * d:3 * d]) * pc)
    mix = jnp.dot(merged.astype(BF16), wout_ref[...], preferred_element_type=F32)
    x1 = _layer_norm(alpha * x_ref[...] + mix, g_ref[...], b_ref[...])
    x1_ref[...] = x1
    xb_ref[...] = x1.astype(BF16)

    logits = jnp.dot(x1, wr_ref[...], preferred_element_type=F32,
                     precision=lax.Precision.HIGHEST) + br_ref[...]
    lane = lax.broadcasted_iota(I32, logits.shape, 1)
    topi = jnp.zeros(logits.shape, I32)
    topv = jnp.full(logits.shape, NEG, F32)
    for k in range(TOP_K):
        m = jnp.max(logits, axis=1, keepdims=True)
        idx = jnp.min(jnp.where(logits == m, lane, LANES), axis=1, keepdims=True)
        topi = jnp.where(lane == k, idx, topi)
        topv = jnp.where(lane == k, m, topv)
        logits = jnp.where(lane == idx, -jnp.inf, logits)
    e = jnp.where(lane < TOP_K, jnp.exp(topv - jnp.max(topv, axis=1, keepdims=True)), 0.0)
    ti_ref[...] = topi
    gate_ref[...] = e / jnp.sum(e, axis=1, keepdims=True)


def _merge(oa, ob, oc, gt, x2, wpa, wpb, wpc, wout, g, b, wr, br, alpha, tm):
    n, d = x2.shape
    row = lambda w: pl.BlockSpec((tm, w), lambda i: (i, 0))
    full = lambda a: pl.BlockSpec(a.shape, lambda i: (0,) * a.ndim)
    kern = functools.partial(_merge_kernel, alpha=alpha)
    return pl.pallas_call(
        kern, grid=(n // tm,),
        in_specs=[row(512), row(512), row(512), row(3 * d), row(d),
                  full(wpa), full(wpb), full(wpc), full(wout), full(g), full(b), full(wr), full(br)],
        out_specs=[row(d), row(d), row(LANES), row(LANES)],
        out_shape=[jax.ShapeDtypeStruct((n, d), F32), jax.ShapeDtypeStruct((n, d), BF16),
                   jax.ShapeDtypeStruct((n, LANES), I32), jax.ShapeDtypeStruct((n, LANES), F32)],
        compiler_params=_params("parallel"))(oa, ob, oc, gt, x2, wpa, wpb, wpc, wout, g, b, wr, br)


def _moe_kernel(be_ref, nu_ref, xs_ref, wgu_ref, bgu_ref, wd_ref, bd_ref, y_ref, wgu_bf, wd_bf):
    i = pl.program_id(0)
    de = wd_ref.shape[1]

    @pl.when((i == 0) | (be_ref[i] != be_ref[jnp.maximum(i - 1, 0)]))
    def _():
        wgu_bf[...] = wgu_ref[0].astype(BF16)
        wd_bf[...] = wd_ref[0].astype(BF16)

    @pl.when(i < nu_ref[0])
    def _():
        gu = jnp.dot(xs_ref[...], wgu_bf[...], preferred_element_type=F32) + bgu_ref[0]
        gate = jnp.minimum(gu[:, 0:de], SWIGLU_LIMIT)
        up = jnp.clip(gu[:, de:], -SWIGLU_LIMIT, SWIGLU_LIMIT)
        act = gate * _sigmoid(SWIGLU_ALPHA * gate) * (up + 1.0)
        y = jnp.dot(act.astype(BF16), wd_bf[...], preferred_element_type=F32) + bd_ref[0]
        y_ref[...] = y.astype(y_ref.dtype)

    @pl.when(i >= nu_ref[0])
    def _():
        y_ref[...] = jnp.zeros(y_ref.shape, y_ref.dtype)


def _moe_experts(layer, blk_e, n_used, xs, wgu, bgu, wd, bd):
    n_slots, d = xs.shape
    n_blocks = n_slots // MOE_BLOCK
    _, _, n2 = wgu.shape
    de = wd.shape[1]
    base = layer * N_EXPERTS
    grid_spec = pltpu.PrefetchScalarGridSpec(
        num_scalar_prefetch=2, grid=(n_blocks,),
        in_specs=[pl.BlockSpec((MOE_BLOCK, d), lambda i, be, nu: (i, 0)),
                  pl.BlockSpec((1, d, n2), lambda i, be, nu: (base + be[i], 0, 0)),
                  pl.BlockSpec((1, 1, n2), lambda i, be, nu: (base + be[i], 0, 0)),
                  pl.BlockSpec((1, de, d), lambda i, be, nu: (base + be[i], 0, 0)),
                  pl.BlockSpec((1, 1, d), lambda i, be, nu: (base + be[i], 0, 0))],
        out_specs=pl.BlockSpec((MOE_BLOCK, d), lambda i, be, nu: (i, 0)),
        scratch_shapes=[pltpu.VMEM((d, n2), BF16), pltpu.VMEM((de, d), BF16)])
    return pl.pallas_call(
        _moe_kernel, grid_spec=grid_spec,
        out_shape=jax.ShapeDtypeStruct((n_slots, d), BF16),
        compiler_params=_params("arbitrary"))(blk_e, n_used, xs, wgu, bgu, wd, bd)


def _combine_kernel(yg_ref, gate_ref, x1_ref, g_ref, b_ref, o_ref, *, alpha):
    gates = gate_ref[...]
    acc = alpha * x1_ref[...]
    for k in range(TOP_K):
        acc = acc + gates[:, k:k + 1] * yg_ref[k].astype(F32)
    o_ref[...] = _layer_norm(acc, g_ref[...], b_ref[...])


def _combine(yg, gates, x1, g, b, alpha, tm):
    n, d = x1.shape
    kern = functools.partial(_combine_kernel, alpha=alpha)
    return pl.pallas_call(
        kern, grid=(n // tm,),
        in_specs=[pl.BlockSpec((TOP_K, tm, d), lambda i: (0, i, 0)),
                  pl.BlockSpec((tm, LANES), lambda i: (i, 0)),
                  pl.BlockSpec((tm, d), lambda i: (i, 0)),
                  pl.BlockSpec((1, d), lambda i: (0, 0)),
                  pl.BlockSpec((1, d), lambda i: (0, 0))],
        out_specs=pl.BlockSpec((tm, d), lambda i: (i, 0)),
        out_shape=jax.ShapeDtypeStruct((n, d), F32),
        compiler_params=_params("parallel"))(yg, gates, x1, g, b)


def _rope_tables(seq, width):
    half = HEAD_DIM // 2
    inv = ROPE_THETA ** (-jnp.arange(half, dtype=F32) / half)
    ang = jnp.arange(seq, dtype=F32)[:, None] * inv[None, :]
    cos = jnp.concatenate([jnp.cos(ang), jnp.cos(ang)], axis=1)
    sin = jnp.concatenate([-jnp.sin(ang), jnp.sin(ang)], axis=1)
    reps = width // HEAD_DIM
    return jnp.tile(cos, (1, reps)), jnp.tile(sin, (1, reps))


def _split_w_in(w):
    seg = lambda a, b: w[:, _OFF[a]:_OFF[b]]
    w_dsa = seg(0, 3)
    small = jnp.concatenate([seg(5, 6), seg(9, 10), jnp.zeros((w.shape[0], LANES - 12), w.dtype)], axis=1)
    w_idx = jnp.concatenate([seg(3, 4)] + [seg(4, 5)] * IDX_HEADS + [small], axis=1)
    w_fox = seg(6, 9)
    w_hgrn = seg(10, 14)
    w_gate = seg(14, 15)
    return [t.astype(BF16) for t in (w_dsa, w_idx, w_fox, w_hgrn, w_gate)]


def _route(top_idx, n_tok):
    n_assign = n_tok * TOP_K
    flat_e = top_idx.reshape(-1)
    experts = jnp.arange(N_EXPERTS, dtype=I32)
    onehot = (flat_e[None, :] == experts[:, None]).astype(I32)
    csum = jnp.cumsum(onehot, axis=1)
    counts = csum[:, -1]
    rank = jnp.sum(csum * onehot, axis=0) - 1
    padded = (counts + MOE_BLOCK - 1) // MOE_BLOCK * MOE_BLOCK
    pad_end = jnp.cumsum(padded)
    pad_start = pad_end - padded
    start = jnp.cumsum(counts) - counts
    slot_of = pad_start[flat_e] + rank
    n_blocks = -(-(n_assign + N_EXPERTS * (MOE_BLOCK - 1)) // MOE_BLOCK)
    blk_start = jnp.arange(n_blocks, dtype=I32) * MOE_BLOCK
    blk_e = jnp.sum((pad_end[None, :] <= blk_start[:, None]).astype(I32), axis=1)
    blk_e = jnp.minimum(blk_e, N_EXPERTS - 1).astype(I32)
    n_used = (pad_end[-1] // MOE_BLOCK).astype(I32).reshape(1)
    order = jnp.argsort(flat_e).astype(I32)
    slot = jnp.arange(n_blocks * MOE_BLOCK, dtype=I32)
    slot_e = jnp.repeat(blk_e, MOE_BLOCK)
    r = slot - pad_start[slot_e]
    src = jnp.clip(start[slot_e] + r, 0, n_assign - 1)
    slot_tok = jnp.where(r < counts[slot_e], order[src] // TOP_K, slot % n_tok)
    return slot_tok, blk_e, n_used, slot_of.reshape(n_tok, TOP_K).T


def _layer(layer, x2, bsz, seq, alpha, w_in, b_fox, lb, ng, wpa, wpb, wpc, wout, g1, b1,
           wr, br, wgu, bgu, wd, bd, g2, b2, cos, sin):
    n, d = x2.shape
    tm = min(512, seq)
    w_dsa, w_idx, w_fox, w_hgrn, w_gate = _split_w_in(w_in)
    nst = seq // tm
    tab = lambda arr: (arr, pl.BlockSpec((tm, 512), lambda i: (i % nst, 0)))

    aq, ak, av = _proj_call(_proj_dsa_kernel, x2, w_dsa, [tab(cos), tab(sin)],
                            [(512, BF16)] * 3, tm)
    iq, ik4, small = _proj_call(_proj_idx_kernel, x2, w_idx, [tab(cos), tab(sin)],
                                [(256, BF16), (256, BF16), (LANES, F32)], tm)
    fq, fk, fv = _proj_call(_proj_fox_kernel, x2, w_fox, [], [(512, BF16)] * 3, tm)
    (zc,) = _proj_call(_proj_plain_kernel, x2, w_hgrn, [], [(2048, F32)], tm)
    (gt,) = _proj_call(_proj_plain_kernel, x2, w_gate, [], [(3 * d, F32)], tm)

    tq_att, tk_att = min(FLASH_TQ, seq), min(FLASH_TK, seq)
    iw_t = small[:, 0:8].reshape(bsz, seq, 8).transpose(0, 2, 1)
    tk_idx = min(512, seq)
    mask = _dsa_index(iq, ik4, iw_t, bsz, seq, min(256, seq), tk_idx,
                      cnt_tiles=min(1 + layer, seq // tk_idx))
    o_a = _flash(aq, ak, av, bsz, seq, tq_att, tk_att, mask=mask)

    ff_t = small[:, 4:12].reshape(bsz, seq, 8).transpose(0, 2, 1)
    nb = _fox_bias(ff_t, b_fox).reshape(bsz, 4, 2, seq)
    o_b = _flash(fq, fk, fv, bsz, seq, tq_att, tk_att, nb=nb)

    lbp = jnp.stack([jnp.log(lb), jnp.log1p(-lb), 1.0 - lb]).astype(F32)
    o_c = _hgrn(zc, lbp, ng.reshape(1, -1).astype(F32), bsz, seq, min(256 >> layer, seq))

    wr_p = jnp.zeros((d, LANES), F32).at[:, :N_EXPERTS].set(wr.astype(F32))
    br_p = jnp.full((1, LANES), NEG, F32).at[0, :N_EXPERTS].set(br.astype(F32))
    x1, x1b, topi, gates = _merge(
        o_a, o_b, o_c, gt, x2, wpa.astype(BF16), wpb.astype(BF16), wpc.astype(BF16),
        wout.astype(BF16), g1.reshape(1, d), b1.reshape(1, d), wr_p, br_p, alpha, min(256, n))

    slot_tok, blk_e, n_used, slot_of = _route(topi[:, :TOP_K], n)
    xs = x1b[slot_tok]
    y_slots = _moe_experts(layer, blk_e, n_used, xs, wgu, bgu, wd, bd)
    yg = y_slots[slot_of]
    return _combine(yg, gates, x1, g2.reshape(1, d), b2.reshape(1, d), alpha, min(256, n))


def kernel(x, w_in, b_fox_f, hgrn_lb_logits, hgrn_norm_g, w_branch_a, w_branch_b, w_branch_c, w_out, ln1_g, ln1_b, w_router, b_router, w_gu, b_gu, w_down, b_down, ln2_g, ln2_b):
    bsz, seq, d = x.shape
    depth = w_in.shape[0]
    alpha = (2 * depth) ** 0.25
    p = jax.nn.softmax(hgrn_lb_logits.astype(F32), axis=0)
    lbs = jnp.cumsum(p, axis=0)
    lbs = lbs - lbs[0]
    cos, sin = _rope_tables(seq, 512)
    x2 = x.reshape(bsz * seq, d)
    n_e = depth * w_gu.shape[1]
    wgu = w_gu.reshape(n_e, d, w_gu.shape[3])
    bgu = b_gu.reshape(n_e, 1, b_gu.shape[2])
    wd = w_down.reshape(n_e, w_down.shape[2], d)
    bd = b_down.reshape(n_e, 1, d)
    for l in range(depth):
        x2 = _layer(l, x2, bsz, seq, alpha, w_in[l], b_fox_f[l], lbs[l], hgrn_norm_g[l],
                    w_branch_a[l], w_branch_b[l], w_branch_c[l], w_out[l], ln1_g[l], ln1_b[l],
                    w_router[l], b_router[l], wgu, bgu, wd, bd, ln2_g[l], ln2_b[l], cos, sin)
    return x2.reshape(bsz, seq, d)
```

```python
import functools
import math

import jax
import jax.numpy as jnp
import numpy as np
from jax import lax
from jax.experimental import pallas as pl
from jax.experimental.pallas import tpu as pltpu

F32 = jnp.float32
BF16 = jnp.bfloat16
I32 = jnp.int32

CHUNK = 64
HEAD_DIM = 64
ROPE_THETA = 10000.0
DSA_TOPK_MAX = 256
IDX_HEADS = 4
N_EXPERTS = 32
TOP_K = 4
SWIGLU_LIMIT = 7.0
SWIGLU_ALPHA = 1.702
MOE_BLOCK = 512
LN_EPS = 1e-5
RMS_EPS = 1e-6

LOG2E = 1.4426950408889634
NEG = -1e30
INT_MIN = -2147483648
INT_MAX = 2147483647
LANES = 128
DSA_PROBES_PER_CHECK = 3
FLASH_TQ = 2048
FLASH_TK = 512
HGRN_TILE = 128
HGRN_SUB = 16
VMEM_LIMIT = 56 * 1024 * 1024

IN_SPLITS = (512, 512, 512, 256, 64, 4, 512, 512, 512, 8, 512, 512, 512, 512, 3072)
_OFF = tuple(int(v) for v in np.cumsum((0,) + IN_SPLITS))


def _params(*sem):
    return pltpu.CompilerParams(dimension_semantics=sem, vmem_limit_bytes=VMEM_LIMIT)


def _dot_nt(a, b):
    return lax.dot_general(a, b, (((1,), (1,)), ((), ())), preferred_element_type=F32)


def _dot_tn(a, b):
    return lax.dot_general(a, b, (((0,), (0,)), ((), ())), preferred_element_type=F32)


def _sigmoid(t):
    return 1.0 / (1.0 + jnp.exp(-t))


def _rope_tile(t, cos, sin):
    w = t.shape[1]
    lane_d = lax.broadcasted_iota(I32, t.shape, 1) & (HEAD_DIM - 1)
    partner = jnp.where(lane_d < HEAD_DIM // 2,
                        pltpu.roll(t, w - HEAD_DIM // 2, axis=1),
                        pltpu.roll(t, HEAD_DIM // 2, axis=1))
    return t * cos + partner * sin


def _proj_dsa_kernel(x_ref, w_ref, cos_ref, sin_ref, q_ref, k_ref, v_ref):
    z = jnp.dot(x_ref[...].astype(BF16), w_ref[...], preferred_element_type=F32)
    cos = cos_ref[...]
    sin = sin_ref[...]
    q_ref[...] = (_rope_tile(z[:, 0:512], cos, sin) * (HEAD_DIM ** -0.5 * LOG2E)).astype(BF16)
    k_ref[...] = _rope_tile(z[:, 512:1024], cos, sin).astype(BF16)
    v_ref[...] = z[:, 1024:1536].astype(BF16)


def _proj_idx_kernel(x_ref, w_ref, cos_ref, sin_ref, iq_ref, ik_ref, sm_ref):
    z = jnp.dot(x_ref[...].astype(BF16), w_ref[...], preferred_element_type=F32)
    cos = cos_ref[...][:, 0:256]
    sin = sin_ref[...][:, 0:256]
    iq_ref[...] = (_rope_tile(z[:, 0:256], cos, sin) * (HEAD_DIM ** -0.5)).astype(BF16)
    ik_ref[...] = _rope_tile(z[:, 256:512], cos, sin).astype(BF16)
    sm_ref[...] = z[:, 512:640]


def _proj_fox_kernel(x_ref, w_ref, q_ref, k_ref, v_ref):
    z = jnp.dot(x_ref[...].astype(BF16), w_ref[...], preferred_element_type=F32)
    q_ref[...] = (z[:, 0:512] * (HEAD_DIM ** -0.5 * LOG2E)).astype(BF16)
    k_ref[...] = z[:, 512:1024].astype(BF16)
    v_ref[...] = z[:, 1024:1536].astype(BF16)


def _proj_plain_kernel(x_ref, w_ref, o_ref):
    o_ref[...] = jnp.dot(x_ref[...].astype(BF16), w_ref[...], preferred_element_type=F32)


def _proj_call(kernel_fn, x2, w, extra, outs, tm):
    n, d = x2.shape
    nout = w.shape[1]
    in_specs = [pl.BlockSpec((tm, d), lambda i: (i, 0)),
                pl.BlockSpec((d, nout), lambda i: (0, 0))]
    args = [x2, w]
    for arr, spec in extra:
        in_specs.append(spec)
        args.append(arr)
    out_shape = [jax.ShapeDtypeStruct((n, wdt), dt) for wdt, dt in outs]
    out_specs = [pl.BlockSpec((tm, wdt), lambda i: (i, 0)) for wdt, _ in outs]
    return pl.pallas_call(
        kernel_fn, grid=(n // tm,), in_specs=in_specs, out_specs=out_specs,
        out_shape=out_shape, compiler_params=_params("parallel"))(*args)


def _fox_bias_kernel(ff_ref, b_ref, nb_ref):
    s = ff_ref.shape[2]
    z = ff_ref[0] + b_ref[...]
    lf = jnp.minimum(z, 0.0) - jnp.log1p(jnp.exp(-jnp.abs(z)))
    r = lax.broadcasted_iota(I32, (LANES, LANES), 0)
    c = lax.broadcasted_iota(I32, (LANES, LANES), 1)
    upper = jnp.where(r <= c, 1.0, 0.0).astype(F32)
    carry = jnp.zeros((lf.shape[0], 1), F32)
    for t in range(s // LANES):
        blk = lf[:, t * LANES:(t + 1) * LANES]
        cs = jnp.dot(blk, upper, preferred_element_type=F32,
                     precision=lax.Precision.HIGHEST) + carry
        nb_ref[0, :, t * LANES:(t + 1) * LANES] = cs * (-LOG2E)
        carry = cs[:, LANES - 1:LANES]


def _fox_bias(ff_t, b_fox):
    bsz, h, s = ff_t.shape
    return pl.pallas_call(
        _fox_bias_kernel, grid=(bsz,),
        in_specs=[pl.BlockSpec((1, h, s), lambda b: (b, 0, 0)),
                  pl.BlockSpec((h, 1), lambda b: (0, 0))],
        out_specs=pl.BlockSpec((1, h, s), lambda b: (b, 0, 0)),
        out_shape=jax.ShapeDtypeStruct((bsz, h, s), F32),
        compiler_params=_params("parallel"))(ff_t, b_fox.reshape(h, 1).astype(F32))


def _dsa_index_kernel(iq_ref, ik_ref, iwt_ref, bias_ref, keys_scr, *, tq, tk, seq, ksel, cnt_tiles):
    i = pl.program_id(1)
    n_kt = ((i + 1) * tq + tk - 1) // tk
    q_pos = i * tq + lax.broadcasted_iota(I32, (1, tq), 1)
    adm_end = (q_pos // CHUNK + 1) * CHUNK

    iq = iq_ref[...]
    lane = lax.broadcasted_iota(I32, iq.shape, 1)
    iqm = [jnp.where((lane >= HEAD_DIM * h) & (lane < HEAD_DIM * (h + 1)), iq, jnp.zeros_like(iq))
           for h in range(IDX_HEADS)]
    wts = [iwt_ref[0, h:h + 1, :] * (IDX_HEADS ** -0.5) for h in range(IDX_HEADS)]

    def key_idx(k0):
        return k0 + lax.broadcasted_iota(I32, (tk, 1), 0)

    def to_key(v):
        bits = pltpu.bitcast(v, I32)
        sign = bits >> 31
        return (bits ^ (sign & 0x7FFFFFFF)) - sign

    def from_key(k):
        sign = k >> 31
        return pltpu.bitcast((k + sign) ^ (sign & 0x7FFFFFFF), F32)

    def score_tile(kt, carry):
        kmax, kmin = carry
        k0 = pl.multiple_of(kt * tk, tk)
        ikt = ik_ref[pl.ds(k0, tk), :]
        sc = jnp.zeros((tk, tq), F32)
        for h in range(IDX_HEADS):
            sc = sc + jnp.maximum(_dot_nt(ikt, iqm[h]), 0.0) * wts[h]
        key = to_key(sc)
        adm = key_idx(k0) < adm_end
        k_lo = jnp.where(adm, key, INT_MIN)
        k_hi = jnp.where(adm, key, INT_MAX)
        keys_scr[pl.ds(k0, tk), :] = k_lo
        for g in range(tk // 8):
            kmax = jnp.maximum(kmax, k_lo[g * 8:(g + 1) * 8, :])
            kmin = jnp.minimum(kmin, k_hi[g * 8:(g + 1) * 8, :])
        return kmax, kmin

    kmax, kmin = lax.fori_loop(0, n_kt, score_tile,
                               (jnp.full((8, tq), INT_MIN, I32), jnp.full((8, tq), INT_MAX, I32)))
    kmax = jnp.max(kmax, axis=0, keepdims=True)
    kmin = jnp.min(kmin, axis=0, keepdims=True)

    kf = float(ksel)
    n_acc = 4
    grp = 8 * n_acc

    tc = cnt_tiles * tk
    n_ct = (n_kt + cnt_tiles - 1) // cnt_tiles

    def pad_tile(kt, carry):
        keys_scr[pl.ds(pl.multiple_of(kt * tk, tk), tk), :] = jnp.full((tk, tq), INT_MIN, I32)
        return carry

    lax.fori_loop(n_kt, n_ct * cnt_tiles, pad_tile, 0)

    def count(pred):
        def body(kt, accs):
            k0 = pl.multiple_of(kt * tc, tc)
            accs = list(accs)
            for g in range(tc // grp):
                kk = keys_scr[pl.ds(k0 + g * grp, grp), :]
                hit = jnp.where(pred(kk, k0 + g * grp), 1.0, 0.0)
                for a in range(n_acc):
                    accs[a] = accs[a] + hit[a * 8:(a + 1) * 8, :]
            return tuple(accs)
        accs = lax.fori_loop(0, n_ct, body, tuple(jnp.zeros((8, tq), F32) for _ in range(n_acc)))
        tot = accs[0]
        for a in range(1, n_acc):
            tot = tot + accs[a]
        return jnp.sum(tot, axis=0, keepdims=True)

    def count_ge(cand):
        return count(lambda kk, k0: kk >= cand)

    n_adm = adm_end.astype(F32)
    zero = jnp.zeros((1, tq), F32)
    one = jnp.ones((1, tq), F32)
    log_k = math.log(kf)
    done0 = jnp.where(n_adm <= kf, 1.0, 0.0)
    init = (jnp.int32(0), kmin, kmax + 1, n_adm, zero, jnp.full((1, tq), INT_MIN, I32), done0, zero,
            one, one, zero)

    def cond(st):
        return (st[0] < 264) & (jnp.min(st[6]) == 0.0)

    def step(st):
        it, lo, hi, c_lo, c_hi, thr, done, tie, w_lo, w_hi, last = st
        lo_v = from_key(lo)
        hi_v = from_key(hi)
        f_lo = (jnp.log(c_lo + 0.5) - log_k) * w_lo
        f_hi = (log_k - jnp.log(c_hi + 0.5)) * w_hi
        cand = to_key(lo_v + (hi_v - lo_v) * (f_lo / (f_lo + f_hi)))
        cand = jnp.where(((it - 3) & 7) == 7, (lo >> 1) + (hi >> 1) + (lo & hi & 1), cand)
        cand = jnp.where(it == 0, kmax, cand)
        cand = jnp.where(it == 1, 0, cand)
        cand = jnp.where(it == 2, 1, cand)
        cand = jnp.minimum(jnp.maximum(cand, lo + 1), hi - 1)
        c = count_ge(cand)
        active = done == 0.0
        up = active & (c >= kf)
        down = active & (c < kf)
        lo = jnp.where(up, cand, lo)
        c_lo = jnp.where(up, c, c_lo)
        hi = jnp.where(down, cand, hi)
        c_hi = jnp.where(down, c, c_hi)
        w_hi = jnp.where(up & (last > 0.0), 0.5 * w_hi, jnp.where(down, 1.0, w_hi))
        w_lo = jnp.where(down & (last < 0.0), 0.5 * w_lo, jnp.where(up, 1.0, w_lo))
        last = jnp.where(up, 1.0, jnp.where(down, -1.0, last))
        hit = active & (c == kf)
        conv = active & jnp.logical_not(hit) & (hi - 1 <= lo)
        thr = jnp.where(hit, cand - 1, jnp.where(conv, lo, thr))
        tie = jnp.where(conv, 1.0, tie)
        done = jnp.where(hit | conv, 1.0, done)
        return it + 1, lo, hi, c_lo, c_hi, thr, done, tie, w_lo, w_hi, last

    def steps(st):
        for _ in range(DSA_PROBES_PER_CHECK):
            st = step(st)
        return st

    st = lax.while_loop(cond, steps, init)
    c_hi, thr, tie = st[4], st[5], st[7]
    any_tie = jnp.max(tie) > 0.0

    @pl.when(jnp.logical_not(any_tie))
    def _():
        def write_tile(kt, carry):
            k0 = pl.multiple_of(kt * tk, tk)
            sel = keys_scr[pl.ds(k0, tk), :] > thr
            bias_ref[0, :, pl.ds(k0, tk)] = jnp.where(sel, 0.0, NEG).T.astype(BF16)
            return carry

        lax.fori_loop(0, n_kt, write_tile, 0)

    @pl.when(any_tie)
    def _():
        need = jnp.where(tie > 0.0, kf - c_hi, 0.0)
        lower = jnp.where(lax.broadcasted_iota(I32, (tk, tk), 0) >= lax.broadcasted_iota(I32, (tk, tk), 1),
                          1.0, 0.0).astype(BF16)

        def write_tile(kt, seen):
            k0 = pl.multiple_of(kt * tk, tk)
            kk = keys_scr[pl.ds(k0, tk), :]
            eq = kk == thr
            rank = jnp.dot(lower, jnp.where(eq, 1.0, 0.0).astype(BF16), preferred_element_type=F32) + seen
            sel = (kk > thr) | (eq & (rank <= need))
            bias_ref[0, :, pl.ds(k0, tk)] = jnp.where(sel, 0.0, NEG).T.astype(BF16)
            return rank[tk - 1:tk, :]

        lax.fori_loop(0, n_kt, write_tile, jnp.zeros((1, tq), F32))

    def fill_tile(kt, carry):
        k0 = pl.multiple_of(kt * tk, tk)
        bias_ref[0, :, pl.ds(k0, tk)] = jnp.full((tq, tk), NEG, BF16)
        return carry

    lax.fori_loop(n_kt, seq // tk, fill_tile, 0)


def _dsa_index(iq, ik4, iw_t, bsz, seq, tq, tk, cnt_tiles=1):
    ksel = min(DSA_TOPK_MAX, seq // 4)
    nq = seq // tq
    kern = functools.partial(_dsa_index_kernel, tq=tq, tk=tk, seq=seq, ksel=ksel, cnt_tiles=cnt_tiles)
    return pl.pallas_call(
        kern, grid=(bsz, nq),
        in_specs=[pl.BlockSpec((tq, 256), lambda b, i: (b * nq + i, 0)),
                  pl.BlockSpec((seq, 256), lambda b, i: (b, 0)),
                  pl.BlockSpec((1, 8, tq), lambda b, i: (b, 0, i))],
        out_specs=pl.BlockSpec((1, tq, seq), lambda b, i: (b, i, 0)),
        out_shape=jax.ShapeDtypeStruct((bsz, seq, seq), BF16),
        scratch_shapes=[pltpu.VMEM((seq, tq), I32)],
        compiler_params=_params("parallel", "parallel"))(iq, ik4, iw_t)


def _flash_kernel(*refs, tq, tk, use_mask):
    if use_mask:
        q_ref, k_ref, v_ref, mask_ref, o_ref, m_scr, acc_scr, mbuf, sem = refs
        nb_ref = None
    else:
        q_ref, k_ref, v_ref, nb_ref, o_ref, m_scr, acc_scr = refs
        mask_ref = None
    b = pl.program_id(0)
    i = pl.program_id(2)
    q = q_ref[...]
    head0 = lax.broadcasted_iota(I32, (tq, LANES), 1) < HEAD_DIM
    head0k = lax.broadcasted_iota(I32, (tk, LANES), 1) < HEAD_DIM
    qm = [jnp.where(head0, q, jnp.zeros_like(q)), jnp.where(head0, jnp.zeros_like(q), q)]
    m_scr[...] = jnp.full(m_scr.shape, NEG, F32)
    acc_scr[...] = jnp.zeros(acc_scr.shape, F32)
    n_col = tk // LANES
    n_sub = tq // tk
    n_off = i * n_sub

    def mask_dma(kt, slot, r0):
        return pltpu.make_async_copy(
            mask_ref.at[b, pl.ds(i * tq + r0, tq - r0), pl.ds(pl.multiple_of(kt * tk, tk), tk)],
            mbuf.at[slot, pl.ds(r0, tq - r0), :], sem.at[slot])

    def tile(kt, diag):
        r0 = 0 if diag is None else diag * tk
        nr = tq - r0
        k0 = pl.multiple_of(kt * tk, tk)
        kt_ = k_ref[pl.ds(k0, tk), :]
        vt = v_ref[pl.ds(k0, tk), :]
        one = jnp.ones_like(vt)
        vx = [jnp.where(head0k, vt, one), jnp.where(head0k, one, vt)]
        if use_mask:
            slot = kt & 1
            if diag is None:
                mask_dma(kt + 1, 1 - slot, 0).start()
            elif diag + 1 < n_sub:
                mask_dma(kt + 1, 1 - slot, (diag + 1) * tk).start()
            mask_dma(kt, slot, r0).wait()
            extra = mbuf[slot, pl.ds(r0, nr), :].astype(F32)
        ss = []
        for j in range(2):
            s = _dot_nt(qm[j][r0:, :], kt_)
            if use_mask:
                s = s + extra
            else:
                s = s + nb_ref[0, 0, j:j + 1, pl.ds(k0, tk)]
                if diag is not None:
                    rr = lax.broadcasted_iota(I32, (nr, tk), 0)
                    cc = lax.broadcasted_iota(I32, (nr, tk), 1)
                    s = jnp.where(cc <= rr, s, NEG)
            ss.append(s)
        ps = []
        for j in range(2):
            sc = [ss[j][:, c * LANES:(c + 1) * LANES] for c in range(n_col)]
            part = sc[0]
            for c in range(1, n_col):
                part = jnp.maximum(part, sc[c])
            m_old = m_scr[j, r0:, :]
            m_new = jnp.maximum(m_old, jnp.max(part, axis=1, keepdims=True))
            alpha = jnp.exp2(m_old - m_new)
            ps.append((alpha, jnp.concatenate([jnp.exp2(c_ - m_new) for c_ in sc], axis=1).astype(BF16)))
            m_scr[j, r0:, :] = m_new
        for j in range(2):
            alpha, p = ps[j]
            acc_scr[j, r0:, :] = alpha * acc_scr[j, r0:, :] + jnp.dot(p, vx[j], preferred_element_type=F32)

    def body(kt, carry):
        tile(kt, None)
        return carry

    if use_mask:
        mask_dma(0, 0, 0).start()
    lax.fori_loop(0, n_off, body, 0)
    for d in range(n_sub):
        tile(n_off + d, d)

    a0 = acc_scr[0]
    a1 = acc_scr[1]
    o0 = a0 / pltpu.roll(a0, HEAD_DIM, axis=1)
    o1 = a1 / pltpu.roll(a1, HEAD_DIM, axis=1)
    o_ref[...] = jnp.where(head0, o0, o1).astype(o_ref.dtype)


def _flash(q, k, v, bsz, seq, tq, tk, nb=None, mask=None):
    nq = seq // tq
    n_hp = q.shape[1] // LANES
    use_mask = mask is not None
    in_specs = [pl.BlockSpec((tq, LANES), lambda b, h, i: (b * nq + i, h)),
                pl.BlockSpec((seq, LANES), lambda b, h, i: (b, h)),
                pl.BlockSpec((seq, LANES), lambda b, h, i: (b, h))]
    scratch = [pltpu.VMEM((2, tq, LANES), F32), pltpu.VMEM((2, tq, LANES), F32)]
    if use_mask:
        in_specs.append(pl.BlockSpec(memory_space=pl.ANY))
        scratch += [pltpu.VMEM((2, tq, tk), BF16), pltpu.SemaphoreType.DMA((2,))]
        extra = mask
    else:
        in_specs.append(pl.BlockSpec((1, 1, 2, seq), lambda b, h, i: (b, h, 0, 0)))
        extra = nb
    kern = functools.partial(_flash_kernel, tq=tq, tk=tk, use_mask=use_mask)
    return pl.pallas_call(
        kern, grid=(bsz, n_hp, nq), in_specs=in_specs,
        out_specs=pl.BlockSpec((tq, LANES), lambda b, h, i: (b * nq + i, h)),
        out_shape=jax.ShapeDtypeStruct(q.shape, BF16),
        scratch_shapes=scratch,
        compiler_params=_params("parallel", "parallel", "parallel"))(q, k, v, extra)


def _hgrn_kernel(q_ref, zf_ref, v_ref, g_ref, lb_ref, ng_ref, o_ref, st_scr, oi_scr, *, tt):
    r = HGRN_SUB
    half = 256

    @pl.when(pl.program_id(1) == 0)
    def _():
        st_scr[...] = jnp.zeros(st_scr.shape, F32)

    q = q_ref[...]
    zf = zf_ref[...]
    v = v_ref[...]
    ls = jnp.minimum(zf, 0.0) - jnp.log1p(jnp.exp(-jnp.abs(zf)))
    a = lb_ref[0:1, :]
    y = lb_ref[1:2, :] + ls
    lf = jnp.maximum(a, y) + jnp.log1p(jnp.exp(-jnp.abs(a - y)))
    kk = lb_ref[2:3, :] * (1.0 / (1.0 + jnp.exp(zf)))

    rin = lax.broadcasted_iota(I32, (tt, 1), 0) & (r - 1)
    b = lf
    sh = 1
    while sh < r:
        b = b + jnp.where(rin >= sh, pltpu.roll(b, sh, axis=0), 0.0)
        sh *= 2

    rr = lax.broadcasted_iota(I32, (half, half), 0) // HEAD_DIM
    cc = lax.broadcasted_iota(I32, (half, half), 1) // HEAD_DIM
    same_head = rr == cc
    ones_bd = jnp.where(same_head, 1.0, 0.0).astype(BF16)
    bd_mask = jnp.where(same_head, 1.0, 0.0).astype(F32)

    o = jnp.zeros((tt, 2 * half), F32)
    for off in range(r):
        if off == 0:
            e = q * kk
            vs = v
        else:
            e = q * pltpu.roll(kk, off, axis=0) * jnp.exp(b - pltpu.roll(b, off, axis=0))
            e = jnp.where(rin >= off, e, 0.0)
            vs = pltpu.roll(v, off, axis=0)
        eb = e.astype(BF16)
        sc = jnp.concatenate(
            [jnp.dot(eb[:, 0:half], ones_bd, preferred_element_type=F32),
             jnp.dot(eb[:, half:], ones_bd, preferred_element_type=F32)], axis=1)
        o = o + sc * vs

    qe = (q * jnp.exp(b)).astype(BF16)
    vb = v.astype(BF16)
    for c in range(tt // r):
        r0 = c * r
        bl = b[r0 + r - 1:r0 + r, :]
        kd = (kk[r0:r0 + r, :] * jnp.exp(bl - b[r0:r0 + r, :])).astype(BF16)
        dec = jnp.exp(bl)
        for h2 in range(2):
            lo = h2 * half
            st = st_scr[h2]
            oi_scr[r0:r0 + r, lo:lo + half] = _dot_nt(qe[r0:r0 + r, lo:lo + half], st.astype(BF16))
            upd = _dot_tn(vb[r0:r0 + r, lo:lo + half], kd[:, lo:lo + half])
            st_scr[h2] = st * dec[:, lo:lo + half] + upd * bd_mask

    o = o + oi_scr[...]
    o2 = o * o
    ones_f = bd_mask
    ms = jnp.concatenate(
        [jnp.dot(o2[:, 0:half], ones_f, preferred_element_type=F32, precision=lax.Precision.HIGHEST),
         jnp.dot(o2[:, half:], ones_f, preferred_element_type=F32, precision=lax.Precision.HIGHEST)],
        axis=1) * (1.0 / HEAD_DIM)
    on = o * lax.rsqrt(ms + RMS_EPS) * ng_ref[...] * _sigmoid(g_ref[...])
    o_ref[...] = on.astype(o_ref.dtype)


def _hgrn(zc, lbp, ng, bsz, seq, tt):
    n = zc.shape[0]
    nt = seq // tt
    col = lambda cidx: pl.BlockSpec((tt, 512), lambda b, j: (b * nt + j, cidx))
    kern = functools.partial(_hgrn_kernel, tt=tt)
    return pl.pallas_call(
        kern, grid=(bsz, nt),
        in_specs=[col(0), col(1), col(2), col(3),
                  pl.BlockSpec((3, 512), lambda b, j: (0, 0)),
                  pl.BlockSpec((1, 512), lambda b, j: (0, 0))],
        out_specs=pl.BlockSpec((tt, 512), lambda b, j: (b * nt + j, 0)),
        out_shape=jax.ShapeDtypeStruct((n, 512), BF16),
        scratch_shapes=[pltpu.VMEM((2, 256, 256), F32), pltpu.VMEM((tt, 512), F32)],
        compiler_params=_params("parallel", "arbitrary"))(zc, zc, zc, zc, lbp, ng)


def _layer_norm(y, g, b):
    mu = jnp.mean(y, axis=-1, keepdims=True)
    d = y - mu
    var = jnp.mean(d * d, axis=-1, keepdims=True)
    return d * lax.rsqrt(var + LN_EPS) * g + b


def _merge_kernel(oa_ref, ob_ref, oc_ref, gt_ref, x_ref, wpa_ref, wpb_ref, wpc_ref, wout_ref,
                  g_ref, b_ref, wr_ref, br_ref, x1_ref, xb_ref, ti_ref, gate_ref, *, alpha):
    d = x_ref.shape[1]
    pa = jnp.dot(oa_ref[...], wpa_ref[...], preferred_element_type=F32)
    pb = jnp.dot(ob_ref[...], wpb_ref[...], preferred_element_type=F32)
    pc = jnp.dot(oc_ref[...], wpc_ref[...], preferred_element_type=F32)
    merged = (_sigmoid(gt_ref[:, 0:d]) * pa + _sigmoid(gt_ref[:, d:2 * d]) * pb
              + _sigmoid(gt_ref[:, 2 * d:3 * d]) * pc)
    mix = jnp.dot(merged.astype(BF16), wout_ref[...], preferred_element_type=F32)
    x1 = _layer_norm(alpha * x_ref[...] + mix, g_ref[...], b_ref[...])
    x1_ref[...] = x1
    xb_ref[...] = x1.astype(BF16)

    logits = jnp.dot(x1, wr_ref[...], preferred_element_type=F32,
                     precision=lax.Precision.HIGHEST) + br_ref[...]
    lane = lax.broadcasted_iota(I32, logits.shape, 1)
    topi = jnp.zeros(logits.shape, I32)
    topv = jnp.full(logits.shape, NEG, F32)
    for k in range(TOP_K):
        m = jnp.max(logits, axis=1, keepdims=True)
        idx = jnp.min(jnp.where(logits == m, lane, LANES), axis=1, keepdims=True)
        topi = jnp.where(lane == k, idx, topi)
        topv = jnp.where(lane == k, m, topv)
        logits = jnp.where(lane == idx, -jnp.inf, logits)
    e = jnp.where(lane < TOP_K, jnp.exp(topv - jnp.max(topv, axis=1, keepdims=True)), 0.0)
    ti_ref[...] = topi
    gate_ref[...] = e / jnp.sum(e, axis=1, keepdims=True)


def _merge(oa, ob, oc, gt, x2, wpa, wpb, wpc, wout, g, b, wr, br, alpha, tm):
    n, d = x2.shape
    row = lambda w: pl.BlockSpec((tm, w), lambda i: (i, 0))
    full = lambda a: pl.BlockSpec(a.shape, lambda i: (0,) * a.ndim)
    kern = functools.partial(_merge_kernel, alpha=alpha)
    return pl.pallas_call(
        kern, grid=(n // tm,),
        in_specs=[row(512), row(512), row(512), row(3 * d), row(d),
                  full(wpa), full(wpb), full(wpc), full(wout), full(g), full(b), full(wr), full(br)],
        out_specs=[row(d), row(d), row(LANES), row(LANES)],
        out_shape=[jax.ShapeDtypeStruct((n, d), F32), jax.ShapeDtypeStruct((n, d), BF16),
                   jax.ShapeDtypeStruct((n, LANES), I32), jax.ShapeDtypeStruct((n, LANES), F32)],
        compiler_params=_params("parallel"))(oa, ob, oc, gt, x2, wpa, wpb, wpc, wout, g, b, wr, br)


def _moe_kernel(be_ref, nu_ref, xs_ref, wgu_ref, bgu_ref, wd_ref, bd_ref, y_ref, wgu_bf, wd_bf):
    i = pl.program_id(0)
    de = wd_ref.shape[1]

    @pl.when((i == 0) | (be_ref[i] != be_ref[jnp.maximum(i - 1, 0)]))
    def _():
        wgu_bf[...] = wgu_ref[0].astype(BF16)
        wd_bf[...] = wd_ref[0].astype(BF16)

    @pl.when(i < nu_ref[0])
    def _():
        gu = jnp.dot(xs_ref[...].astype(BF16), wgu_bf[...], preferred_element_type=F32) + bgu_ref[0]
        gate = jnp.minimum(gu[:, 0:de], SWIGLU_LIMIT)
        up = jnp.clip(gu[:, de:], -SWIGLU_LIMIT, SWIGLU_LIMIT)
        act = gate * _sigmoid(SWIGLU_ALPHA * gate) * (up + 1.0)
        y = jnp.dot(act.astype(BF16), wd_bf[...], preferred_element_type=F32) + bd_ref[0]
        y_ref[...] = y.astype(y_ref.dtype)

    @pl.when(i >= nu_ref[0])
    def _():
        y_ref[...] = jnp.zeros(y_ref.shape, y_ref.dtype)


def _moe_experts(layer, blk_e, n_used, xs, wgu, bgu, wd, bd):
    n_slots, d = xs.shape
    n_blocks = n_slots // MOE_BLOCK
    _, _, n2 = wgu.shape
    de = wd.shape[1]
    base = layer * N_EXPERTS
    grid_spec = pltpu.PrefetchScalarGridSpec(
        num_scalar_prefetch=2, grid=(n_blocks,),
        in_specs=[pl.BlockSpec((MOE_BLOCK, d), lambda i, be, nu: (i, 0)),
                  pl.BlockSpec((1, d, n2), lambda i, be, nu: (base + be[i], 0, 0)),
                  pl.BlockSpec((1, 1, n2), lambda i, be, nu: (base + be[i], 0, 0)),
                  pl.BlockSpec((1, de, d), lambda i, be, nu: (base + be[i], 0, 0)),
                  pl.BlockSpec((1, 1, d), lambda i, be, nu: (base + be[i], 0, 0))],
        out_specs=pl.BlockSpec((MOE_BLOCK, d), lambda i, be, nu: (i, 0)),
        scratch_shapes=[pltpu.VMEM((d, n2), BF16), pltpu.VMEM((de, d), BF16)])
    return pl.pallas_call(
        _moe_kernel, grid_spec=grid_spec,
        out_shape=jax.ShapeDtypeStruct((n_slots, d), BF16),
        compiler_params=_params("arbitrary"))(blk_e, n_used, xs, wgu, bgu, wd, bd)


def _combine_kernel(yg_ref, gate_ref, x1_ref, g_ref, b_ref, o_ref, *, alpha):
    gates = gate_ref[...]
    acc = alpha * x1_ref[...]
    for k in range(TOP_K):
        acc = acc + gates[:, k:k + 1] * yg_ref[k].astype(F32)
    o_ref[...] = _layer_norm(acc, g_ref[...], b_ref[...])


def _combine(yg, gates, x1, g, b, alpha, tm):
    n, d = x1.shape
    kern = functools.partial(_combine_kernel, alpha=alpha)
    return pl.pallas_call(
        kern, grid=(n // tm,),
        in_specs=[pl.BlockSpec((TOP_K, tm, d), lambda i: (0, i, 0)),
                  pl.BlockSpec((tm, LANES), lambda i: (i, 0)),
                  pl.BlockSpec((tm, d), lambda i: (i, 0)),
                  pl.BlockSpec((1, d), lambda i: (0, 0)),
                  pl.BlockSpec((1, d), lambda i: (0, 0))],
        out_specs=pl.BlockSpec((tm, d), lambda i: (i, 0)),
        out_shape=jax.ShapeDtypeStruct((n, d), F32),
        compiler_params=_params("parallel"))(yg, gates, x1, g, b)


def _rope_tables(seq, width):
    half = HEAD_DIM // 2
    inv = ROPE_THETA ** (-jnp.arange(half, dtype=F32) / half)
    ang = jnp.arange(seq, dtype=F32)[:, None] * inv[None, :]
    cos = jnp.concatenate([jnp.cos(ang), jnp.cos(ang)], axis=1)
    sin = jnp.concatenate([-jnp.sin(ang), jnp.sin(ang)], axis=1)
    reps = width // HEAD_DIM
    return jnp.tile(cos, (1, reps)), jnp.tile(sin, (1, reps))


def _split_w_in(w):
    seg = lambda a, b: w[:, _OFF[a]:_OFF[b]]
    w_dsa = seg(0, 3)
    small = jnp.concatenate([seg(5, 6), seg(9, 10), jnp.zeros((w.shape[0], LANES - 12), w.dtype)], axis=1)
    w_idx = jnp.concatenate([seg(3, 4)] + [seg(4, 5)] * IDX_HEADS + [small], axis=1)
    w_fox = seg(6, 9)
    w_hgrn = seg(10, 14)
    w_gate = seg(14, 15)
    return [t.astype(BF16) for t in (w_dsa, w_idx, w_fox, w_hgrn, w_gate)]


def _route(top_idx, n_tok):
    n_assign = n_tok * TOP_K
    flat_e = top_idx.reshape(-1)
    experts = jnp.arange(N_EXPERTS, dtype=I32)
    onehot = (flat_e[None, :] == experts[:, None]).astype(I32)
    csum = jnp.cumsum(onehot, axis=1)
    counts = csum[:, -1]
    rank = jnp.sum(csum * onehot, axis=0) - 1
    padded = (counts + MOE_BLOCK - 1) // MOE_BLOCK * MOE_BLOCK
    pad_end = jnp.cumsum(padded)
    pad_start = pad_end - padded
    start = jnp.cumsum(counts) - counts
    slot_of = pad_start[flat_e] + rank
    n_blocks = -(-(n_assign + N_EXPERTS * (MOE_BLOCK - 1)) // MOE_BLOCK)
    blk_start = jnp.arange(n_blocks, dtype=I32) * MOE_BLOCK
    blk_e = jnp.sum((pad_end[None, :] <= blk_start[:, None]).astype(I32), axis=1)
    blk_e = jnp.minimum(blk_e, N_EXPERTS - 1).astype(I32)
    n_used = (pad_end[-1] // MOE_BLOCK).astype(I32).reshape(1)
    order = jnp.argsort(flat_e).astype(I32)
    slot = jnp.arange(n_blocks * MOE_BLOCK, dtype=I32)
    slot_e = jnp.repeat(blk_e, MOE_BLOCK)
    r = slot - pad_start[slot_e]
    src = jnp.clip(start[slot_e] + r, 0, n_assign - 1)
    slot_tok = jnp.where(r < counts[slot_e], order[src] // TOP_K, slot % n_tok)
    return slot_tok, blk_e, n_used, slot_of.reshape(n_tok, TOP_K).T


def _layer(layer, x2, bsz, seq, alpha, w_in, b_fox, lb, ng, wpa, wpb, wpc, wout, g1, b1,
           wr, br, wgu, bgu, wd, bd, g2, b2, cos, sin):
    n, d = x2.shape
    tm = min(512, seq)
    w_dsa, w_idx, w_fox, w_hgrn, w_gate = _split_w_in(w_in)
    nst = seq // tm
    tab = lambda arr: (arr, pl.BlockSpec((tm, 512), lambda i: (i % nst, 0)))

    aq, ak, av = _proj_call(_proj_dsa_kernel, x2, w_dsa, [tab(cos), tab(sin)],
                            [(512, BF16)] * 3, tm)
    iq, ik4, small = _proj_call(_proj_idx_kernel, x2, w_idx, [tab(cos), tab(sin)],
                                [(256, BF16), (256, BF16), (LANES, F32)], tm)
    fq, fk, fv = _proj_call(_proj_fox_kernel, x2, w_fox, [], [(512, BF16)] * 3, tm)
    (zc,) = _proj_call(_proj_plain_kernel, x2, w_hgrn, [], [(2048, F32)], tm)
    (gt,) = _proj_call(_proj_plain_kernel, x2, w_gate, [], [(3 * d, F32)], tm)

    tq_att, tk_att = min(FLASH_TQ, seq), min(FLASH_TK, seq)
    iw_t = small[:, 0:8].reshape(bsz, seq, 8).transpose(0, 2, 1)
    mask = _dsa_index(iq, ik4, iw_t, bsz, seq, min(256, seq), min(512, seq))
    o_a = _flash(aq, ak, av, bsz, seq, tq_att, tk_att, mask=mask)

    ff_t = small[:, 4:12].reshape(bsz, seq, 8).transpose(0, 2, 1)
    nb = _fox_bias(ff_t, b_fox).reshape(bsz, 4, 2, seq)
    o_b = _flash(fq, fk, fv, bsz, seq, tq_att, tk_att, nb=nb)

    lbp = jnp.stack([jnp.log(lb), jnp.log1p(-lb), 1.0 - lb]).astype(F32)
    o_c = _hgrn(zc, lbp, ng.reshape(1, -1).astype(F32), bsz, seq, min(HGRN_TILE, seq))

    wr_p = jnp.zeros((d, LANES), F32).at[:, :N_EXPERTS].set(wr.astype(F32))
    br_p = jnp.full((1, LANES), NEG, F32).at[0, :N_EXPERTS].set(br.astype(F32))
    x1, x1b, topi, gates = _merge(
        o_a, o_b, o_c, gt, x2, wpa.astype(BF16), wpb.astype(BF16), wpc.astype(BF16),
        wout.astype(BF16), g1.reshape(1, d), b1.reshape(1, d), wr_p, br_p, alpha, min(256, n))

    slot_tok, blk_e, n_used, slot_of = _route(topi[:, :TOP_K], n)
    xs = (x1b if layer == 0 else x1)[slot_tok]
    y_slots = _moe_experts(layer, blk_e, n_used, xs, wgu, bgu, wd, bd)
    yg = y_slots[slot_of]
    return _combine(yg, gates, x1, g2.reshape(1, d), b2.reshape(1, d), alpha, min(256, n))


def kernel(x, w_in, b_fox_f, hgrn_lb_logits, hgrn_norm_g, w_branch_a, w_branch_b, w_branch_c, w_out, ln1_g, ln1_b, w_router, b_router, w_gu, b_gu, w_down, b_down, ln2_g, ln2_b):
    bsz, seq, d = x.shape
    depth = w_in.shape[0]
    alpha = (2 * depth) ** 0.25
    p = jax.nn.softmax(hgrn_lb_logits.astype(F32), axis=0)
    lbs = jnp.cumsum(p, axis=0)
    lbs = lbs - lbs[0]
    cos, sin = _rope_tables(seq, 512)
    x2 = x.reshape(bsz * seq, d)
    n_e = depth * w_gu.shape[1]
    wgu = w_gu.reshape(n_e, d, w_gu.shape[3])
    bgu = b_gu.reshape(n_e, 1, b_gu.shape[2])
    wd = w_down.reshape(n_e, w_down.shape[2], d)
    bd = b_down.reshape(n_e, 1, d)
    for l in range(depth):
        x2 = _layer(l, x2, bsz, seq, alpha, w_in[l], b_fox_f[l], lbs[l], hgrn_norm_g[l],
                    w_branch_a[l], w_branch_b[l], w_branch_c[l], w_out[l], ln1_g[l], ln1_b[l],
                    w_router[l], b_router[l], wgu, bgu, wd, bd, ln2_g[l], ln2_b[l], cos, sin)
    return x2.reshape(bsz, seq, d)
```

```python
import functools
import math

import jax
import jax.numpy as jnp
import numpy as np
from jax import lax
from jax.experimental import pallas as pl
from jax.experimental.pallas import tpu as pltpu

F32 = jnp.float32
BF16 = jnp.bfloat16
I32 = jnp.int32

CHUNK = 64
HEAD_DIM = 64
ROPE_THETA = 10000.0
DSA_TOPK_MAX = 256
IDX_HEADS = 4
N_EXPERTS = 32
TOP_K = 4
SWIGLU_LIMIT = 7.0
SWIGLU_ALPHA = 1.702
MOE_BLOCK = 512
LN_EPS = 1e-5
RMS_EPS = 1e-6

LOG2E = 1.4426950408889634
NEG = -1e30
INT_MIN = -2147483648
INT_MAX = 2147483647
LANES = 128
DSA_PROBES_PER_CHECK = 3
FLASH_TQ = 2048
FLASH_TK = 512
PROJ_TM = 512
PROJ_QKV_TM = 1024
MERGE_TM = 512
COMBINE_TM = 1024
HGRN_TILE = 128
HGRN_SUB = 16
VMEM_LIMIT = 56 * 1024 * 1024

IN_SPLITS = (512, 512, 512, 256, 64, 4, 512, 512, 512, 8, 512, 512, 512, 512, 3072)
_OFF = tuple(int(v) for v in np.cumsum((0,) + IN_SPLITS))


def _params(*sem):
    return pltpu.CompilerParams(dimension_semantics=sem, vmem_limit_bytes=VMEM_LIMIT)


def _dot_nt(a, b):
    return lax.dot_general(a, b, (((1,), (1,)), ((), ())), preferred_element_type=F32)


def _dot_tn(a, b):
    return lax.dot_general(a, b, (((0,), (0,)), ((), ())), preferred_element_type=F32)


def _sigmoid(t):
    return 1.0 / (1.0 + jnp.exp(-t))


def _rope_tile(t, cos, sin):
    w = t.shape[1]
    lane_d = lax.broadcasted_iota(I32, t.shape, 1) & (HEAD_DIM - 1)
    partner = jnp.where(lane_d < HEAD_DIM // 2,
                        pltpu.roll(t, w - HEAD_DIM // 2, axis=1),
                        pltpu.roll(t, HEAD_DIM // 2, axis=1))
    return t * cos + partner * sin


def _proj_dsa_kernel(x_ref, w_ref, cos_ref, sin_ref, q_ref, k_ref, v_ref):
    z = jnp.dot(x_ref[...].astype(BF16), w_ref[...], preferred_element_type=F32)
    cos = cos_ref[...]
    sin = sin_ref[...]
    q_ref[...] = (_rope_tile(z[:, 0:512], cos, sin) * (HEAD_DIM ** -0.5 * LOG2E)).astype(BF16)
    k_ref[...] = _rope_tile(z[:, 512:1024], cos, sin).astype(BF16)
    v_ref[...] = z[:, 1024:1536].astype(BF16)


def _proj_idx_kernel(x_ref, w_ref, cos_ref, sin_ref, iq_ref, ik_ref, sm_ref):
    z = jnp.dot(x_ref[...].astype(BF16), w_ref[...], preferred_element_type=F32)
    cos = cos_ref[...][:, 0:256]
    sin = sin_ref[...][:, 0:256]
    iq_ref[...] = (_rope_tile(z[:, 0:256], cos, sin) * (HEAD_DIM ** -0.5)).astype(BF16)
    ik_ref[...] = _rope_tile(z[:, 256:512], cos, sin).astype(BF16)
    sm_ref[...] = z[:, 512:640]


def _proj_fox_kernel(x_ref, w_ref, q_ref, k_ref, v_ref):
    z = jnp.dot(x_ref[...].astype(BF16), w_ref[...], preferred_element_type=F32)
    q_ref[...] = (z[:, 0:512] * (HEAD_DIM ** -0.5 * LOG2E)).astype(BF16)
    k_ref[...] = z[:, 512:1024].astype(BF16)
    v_ref[...] = z[:, 1024:1536].astype(BF16)


def _proj_plain_kernel(x_ref, w_ref, o_ref):
    o_ref[...] = jnp.dot(x_ref[...].astype(BF16), w_ref[...], preferred_element_type=F32)


def _proj_call(kernel_fn, x2, w, extra, outs, tm):
    n, d = x2.shape
    nout = w.shape[1]
    in_specs = [pl.BlockSpec((tm, d), lambda i: (i, 0)),
                pl.BlockSpec((d, nout), lambda i: (0, 0))]
    args = [x2, w]
    for arr, spec in extra:
        in_specs.append(spec)
        args.append(arr)
    out_shape = [jax.ShapeDtypeStruct((n, wdt), dt) for wdt, dt in outs]
    out_specs = [pl.BlockSpec((tm, wdt), lambda i: (i, 0)) for wdt, _ in outs]
    return pl.pallas_call(
        kernel_fn, grid=(n // tm,), in_specs=in_specs, out_specs=out_specs,
        out_shape=out_shape, compiler_params=_params("parallel"))(*args)


def _fox_bias_kernel(ff_ref, b_ref, nb_ref):
    s = ff_ref.shape[2]
    z = ff_ref[0] + b_ref[...]
    lf = jnp.minimum(z, 0.0) - jnp.log1p(jnp.exp(-jnp.abs(z)))
    r = lax.broadcasted_iota(I32, (LANES, LANES), 0)
    c = lax.broadcasted_iota(I32, (LANES, LANES), 1)
    upper = jnp.where(r <= c, 1.0, 0.0).astype(F32)
    carry = jnp.zeros((lf.shape[0], 1), F32)
    for t in range(s // LANES):
        blk = lf[:, t * LANES:(t + 1) * LANES]
        cs = jnp.dot(blk, upper, preferred_element_type=F32,
                     precision=lax.Precision.HIGHEST) + carry
        nb_ref[0, :, t * LANES:(t + 1) * LANES] = cs * (-LOG2E)
        carry = cs[:, LANES - 1:LANES]


def _fox_bias(ff_t, b_fox):
    bsz, h, s = ff_t.shape
    return pl.pallas_call(
        _fox_bias_kernel, grid=(bsz,),
        in_specs=[pl.BlockSpec((1, h, s), lambda b: (b, 0, 0)),
                  pl.BlockSpec((h, 1), lambda b: (0, 0))],
        out_specs=pl.BlockSpec((1, h, s), lambda b: (b, 0, 0)),
        out_shape=jax.ShapeDtypeStruct((bsz, h, s), F32),
        compiler_params=_params("parallel"))(ff_t, b_fox.reshape(h, 1).astype(F32))


def _dsa_index_kernel(iq_ref, ik_ref, iwt_ref, bias_ref, keys_scr, *, tq, tk, seq, ksel, cnt_tiles):
    i = pl.program_id(1)
    n_kt = ((i + 1) * tq + tk - 1) // tk
    q_pos = i * tq + lax.broadcasted_iota(I32, (1, tq), 1)
    adm_end = (q_pos // CHUNK + 1) * CHUNK

    iq = iq_ref[...]
    lane = lax.broadcasted_iota(I32, iq.shape, 1)
    iqm = [jnp.where((lane >= HEAD_DIM * h) & (lane < HEAD_DIM * (h + 1)), iq, jnp.zeros_like(iq))
           for h in range(IDX_HEADS)]
    wts = [iwt_ref[0, h:h + 1, :] * (IDX_HEADS ** -0.5) for h in range(IDX_HEADS)]

    def key_idx(k0):
        return k0 + lax.broadcasted_iota(I32, (tk, 1), 0)

    def to_key(v):
        bits = pltpu.bitcast(v, I32)
        sign = bits >> 31
        return (bits ^ (sign & 0x7FFFFFFF)) - sign

    def from_key(k):
        sign = k >> 31
        return pltpu.bitcast((k + sign) ^ (sign & 0x7FFFFFFF), F32)

    def score_tile(kt, carry):
        kmax, kmin = carry
        k0 = pl.multiple_of(kt * tk, tk)
        ikt = ik_ref[pl.ds(k0, tk), :]
        sc = jnp.zeros((tk, tq), F32)
        for h in range(IDX_HEADS):
            sc = sc + jnp.maximum(_dot_nt(ikt, iqm[h]), 0.0) * wts[h]
        key = to_key(sc)
        adm = key_idx(k0) < adm_end
        k_lo = jnp.where(adm, key, INT_MIN)
        k_hi = jnp.where(adm, key, INT_MAX)
        keys_scr[pl.ds(k0, tk), :] = k_lo
        for g in range(tk // 8):
            kmax = jnp.maximum(kmax, k_lo[g * 8:(g + 1) * 8, :])
            kmin = jnp.minimum(kmin, k_hi[g * 8:(g + 1) * 8, :])
        return kmax, kmin

    kmax, kmin = lax.fori_loop(0, n_kt, score_tile,
                               (jnp.full((8, tq), INT_MIN, I32), jnp.full((8, tq), INT_MAX, I32)))
    kmax = jnp.max(kmax, axis=0, keepdims=True)
    kmin = jnp.min(kmin, axis=0, keepdims=True)

    kf = float(ksel)
    n_acc = 4
    grp = 8 * n_acc

    tc = cnt_tiles * tk
    n_ct = (n_kt + cnt_tiles - 1) // cnt_tiles

    def pad_tile(kt, carry):
        keys_scr[pl.ds(pl.multiple_of(kt * tk, tk), tk), :] = jnp.full((tk, tq), INT_MIN, I32)
        return carry

    lax.fori_loop(n_kt, n_ct * cnt_tiles, pad_tile, 0)

    def count(pred):
        def body(kt, accs):
            k0 = pl.multiple_of(kt * tc, tc)
            accs = list(accs)
            for g in range(tc // grp):
                kk = keys_scr[pl.ds(k0 + g * grp, grp), :]
                hit = jnp.where(pred(kk, k0 + g * grp), 1.0, 0.0)
                for a in range(n_acc):
                    accs[a] = accs[a] + hit[a * 8:(a + 1) * 8, :]
            return tuple(accs)
        accs = lax.fori_loop(0, n_ct, body, tuple(jnp.zeros((8, tq), F32) for _ in range(n_acc)))
        tot = accs[0]
        for a in range(1, n_acc):
            tot = tot + accs[a]
        return jnp.sum(tot, axis=0, keepdims=True)

    def count_ge(cand):
        return count(lambda kk, k0: kk >= cand)

    n_adm = adm_end.astype(F32)
    zero = jnp.zeros((1, tq), F32)
    one = jnp.ones((1, tq), F32)
    log_k = math.log(kf)
    done0 = jnp.where(n_adm <= kf, 1.0, 0.0)
    init = (jnp.int32(0), kmin, kmax + 1, n_adm, zero, jnp.full((1, tq), INT_MIN, I32), done0, zero,
            one, one, zero)

    def cond(st):
        return (st[0] < 264) & (jnp.min(st[6]) == 0.0)

    def step(st):
        it, lo, hi, c_lo, c_hi, thr, done, tie, w_lo, w_hi, last = st
        lo_v = from_key(lo)
        hi_v = from_key(hi)
        f_lo = (jnp.log(c_lo + 0.5) - log_k) * w_lo
        f_hi = (log_k - jnp.log(c_hi + 0.5)) * w_hi
        cand = to_key(lo_v + (hi_v - lo_v) * (f_lo / (f_lo + f_hi)))
        cand = jnp.where(((it - 3) & 7) == 7, (lo >> 1) + (hi >> 1) + (lo & hi & 1), cand)
        cand = jnp.where(it == 0, kmax, cand)
        cand = jnp.where(it == 1, 0, cand)
        cand = jnp.where(it == 2, 1, cand)
        cand = jnp.minimum(jnp.maximum(cand, lo + 1), hi - 1)
        c = count_ge(cand)
        active = done == 0.0
        up = active & (c >= kf)
        down = active & (c < kf)
        lo = jnp.where(up, cand, lo)
        c_lo = jnp.where(up, c, c_lo)
        hi = jnp.where(down, cand, hi)
        c_hi = jnp.where(down, c, c_hi)
        w_hi = jnp.where(up & (last > 0.0), 0.5 * w_hi, jnp.where(down, 1.0, w_hi))
        w_lo = jnp.where(down & (last < 0.0), 0.5 * w_lo, jnp.where(up, 1.0, w_lo))
        last = jnp.where(up, 1.0, jnp.where(down, -1.0, last))
        hit = active & (c == kf)
        conv = active & jnp.logical_not(hit) & (hi - 1 <= lo)
        thr = jnp.where(hit, cand - 1, jnp.where(conv, lo, thr))
        tie = jnp.where(conv, 1.0, tie)
        done = jnp.where(hit | conv, 1.0, done)
        return it + 1, lo, hi, c_lo, c_hi, thr, done, tie, w_lo, w_hi, last

    def steps(st):
        for _ in range(DSA_PROBES_PER_CHECK):
            st = step(st)
        return st

    st = lax.while_loop(cond, steps, init)
    c_hi, thr, tie = st[4], st[5], st[7]
    any_tie = jnp.max(tie) > 0.0

    @pl.when(jnp.logical_not(any_tie))
    def _():
        def write_tile(kt, carry):
            k0 = pl.multiple_of(kt * tk, tk)
            sel = keys_scr[pl.ds(k0, tk), :] > thr
            bias_ref[0, :, pl.ds(k0, tk)] = jnp.where(sel, 0.0, NEG).T.astype(BF16)
            return carry

        lax.fori_loop(0, n_kt, write_tile, 0)

    @pl.when(any_tie)
    def _():
        need = jnp.where(tie > 0.0, kf - c_hi, 0.0)
        lower = jnp.where(lax.broadcasted_iota(I32, (tk, tk), 0) >= lax.broadcasted_iota(I32, (tk, tk), 1),
                          1.0, 0.0).astype(BF16)

        def write_tile(kt, seen):
            k0 = pl.multiple_of(kt * tk, tk)
            kk = keys_scr[pl.ds(k0, tk), :]
            eq = kk == thr
            rank = jnp.dot(lower, jnp.where(eq, 1.0, 0.0).astype(BF16), preferred_element_type=F32) + seen
            sel = (kk > thr) | (eq & (rank <= need))
            bias_ref[0, :, pl.ds(k0, tk)] = jnp.where(sel, 0.0, NEG).T.astype(BF16)
            return rank[tk - 1:tk, :]

        lax.fori_loop(0, n_kt, write_tile, jnp.zeros((1, tq), F32))

    def fill_tile(kt, carry):
        k0 = pl.multiple_of(kt * tk, tk)
        bias_ref[0, :, pl.ds(k0, tk)] = jnp.full((tq, tk), NEG, BF16)
        return carry

    lax.fori_loop(n_kt, seq // tk, fill_tile, 0)


def _dsa_index(iq, ik4, iw_t, bsz, seq, tq, tk, cnt_tiles=1):
    ksel = min(DSA_TOPK_MAX, seq // 4)
    nq = seq // tq
    kern = functools.partial(_dsa_index_kernel, tq=tq, tk=tk, seq=seq, ksel=ksel, cnt_tiles=cnt_tiles)
    return pl.pallas_call(
        kern, grid=(bsz, nq),
        in_specs=[pl.BlockSpec((tq, 256), lambda b, i: (b * nq + i, 0)),
                  pl.BlockSpec((seq, 256), lambda b, i: (b, 0)),
                  pl.BlockSpec((1, 8, tq), lambda b, i: (b, 0, i))],
        out_specs=pl.BlockSpec((1, tq, seq), lambda b, i: (b, i, 0)),
        out_shape=jax.ShapeDtypeStruct((bsz, seq, seq), BF16),
        scratch_shapes=[pltpu.VMEM((seq, tq), I32)],
        compiler_params=_params("parallel", "parallel"))(iq, ik4, iw_t)


def _flash_kernel(*refs, tq, tk, use_mask):
    if use_mask:
        q_ref, k_ref, v_ref, mask_ref, o_ref, m_scr, acc_scr, mbuf, sem = refs
        nb_ref = None
    else:
        q_ref, k_ref, v_ref, nb_ref, o_ref, m_scr, acc_scr = refs
        mask_ref = None
    b = pl.program_id(0)
    i = pl.program_id(2)
    q = q_ref[...]
    head0 = lax.broadcasted_iota(I32, (tq, LANES), 1) < HEAD_DIM
    head0k = lax.broadcasted_iota(I32, (tk, LANES), 1) < HEAD_DIM
    qm = [jnp.where(head0, q, jnp.zeros_like(q)), jnp.where(head0, jnp.zeros_like(q), q)]
    m_scr[...] = jnp.full(m_scr.shape, NEG, F32)
    acc_scr[...] = jnp.zeros(acc_scr.shape, F32)
    n_col = tk // LANES
    n_sub = tq // tk
    n_off = i * n_sub

    def mask_dma(kt, slot, r0):
        return pltpu.make_async_copy(
            mask_ref.at[b, pl.ds(i * tq + r0, tq - r0), pl.ds(pl.multiple_of(kt * tk, tk), tk)],
            mbuf.at[slot, pl.ds(r0, tq - r0), :], sem.at[slot])

    def tile(kt, diag):
        r0 = 0 if diag is None else diag * tk
        nr = tq - r0
        k0 = pl.multiple_of(kt * tk, tk)
        kt_ = k_ref[pl.ds(k0, tk), :]
        vt = v_ref[pl.ds(k0, tk), :]
        one = jnp.ones_like(vt)
        vx = [jnp.where(head0k, vt, one), jnp.where(head0k, one, vt)]
        if use_mask:
            slot = kt & 1
            if diag is None:
                mask_dma(kt + 1, 1 - slot, 0).start()
            elif diag + 1 < n_sub:
                mask_dma(kt + 1, 1 - slot, (diag + 1) * tk).start()
            mask_dma(kt, slot, r0).wait()
            extra = mbuf[slot, pl.ds(r0, nr), :].astype(F32)
        ss = []
        for j in range(2):
            s = _dot_nt(qm[j][r0:, :], kt_)
            if use_mask:
                s = s + extra
            else:
                s = s + nb_ref[0, 0, j:j + 1, pl.ds(k0, tk)]
                if diag is not None:
                    rr = lax.broadcasted_iota(I32, (nr, tk), 0)
                    cc = lax.broadcasted_iota(I32, (nr, tk), 1)
                    s = jnp.where(cc <= rr, s, NEG)
            ss.append(s)
        ps = []
        for j in range(2):
            sc = [ss[j][:, c * LANES:(c + 1) * LANES] for c in range(n_col)]
            part = sc[0]
            for c in range(1, n_col):
                part = jnp.maximum(part, sc[c])
            m_old = m_scr[j, r0:, :]
            m_new = jnp.maximum(m_old, jnp.max(part, axis=1, keepdims=True))
            alpha = jnp.exp2(m_old - m_new)
            ps.append((alpha, jnp.concatenate([jnp.exp2(c_ - m_new) for c_ in sc], axis=1).astype(BF16)))
            m_scr[j, r0:, :] = m_new
        for j in range(2):
            alpha, p = ps[j]
            acc_scr[j, r0:, :] = alpha * acc_scr[j, r0:, :] + jnp.dot(p, vx[j], preferred_element_type=F32)

    def body(kt, carry):
        tile(kt, None)
        return carry

    if use_mask:
        mask_dma(0, 0, 0).start()
    lax.fori_loop(0, n_off, body, 0)
    for d in range(n_sub):
        tile(n_off + d, d)

    a0 = acc_scr[0]
    a1 = acc_scr[1]
    o0 = a0 / pltpu.roll(a0, HEAD_DIM, axis=1)
    o1 = a1 / pltpu.roll(a1, HEAD_DIM, axis=1)
    o_ref[...] = jnp.where(head0, o0, o1).astype(o_ref.dtype)


def _flash(q, k, v, bsz, seq, tq, tk, nb=None, mask=None):
    nq = seq // tq
    n_hp = q.shape[1] // LANES
    use_mask = mask is not None
    in_specs = [pl.BlockSpec((tq, LANES), lambda b, h, i: (b * nq + i, h)),
                pl.BlockSpec((seq, LANES), lambda b, h, i: (b, h)),
                pl.BlockSpec((seq, LANES), lambda b, h, i: (b, h))]
    scratch = [pltpu.VMEM((2, tq, LANES), F32), pltpu.VMEM((2, tq, LANES), F32)]
    if use_mask:
        in_specs.append(pl.BlockSpec(memory_space=pl.ANY))
        scratch += [pltpu.VMEM((2, tq, tk), BF16), pltpu.SemaphoreType.DMA((2,))]
        extra = mask
    else:
        in_specs.append(pl.BlockSpec((1, 1, 2, seq), lambda b, h, i: (b, h, 0, 0)))
        extra = nb
    kern = functools.partial(_flash_kernel, tq=tq, tk=tk, use_mask=use_mask)
    return pl.pallas_call(
        kern, grid=(bsz, n_hp, nq), in_specs=in_specs,
        out_specs=pl.BlockSpec((tq, LANES), lambda b, h, i: (b * nq + i, h)),
        out_shape=jax.ShapeDtypeStruct(q.shape, BF16),
        scratch_shapes=scratch,
        compiler_params=_params("parallel", "parallel", "parallel"))(q, k, v, extra)


def _hgrn_kernel(q_ref, zf_ref, v_ref, g_ref, lb_ref, ng_ref, o_ref, st_scr, oi_scr, *, tt):
    r = HGRN_SUB
    half = 256

    @pl.when(pl.program_id(1) == 0)
    def _():
        st_scr[...] = jnp.zeros(st_scr.shape, F32)

    q = q_ref[...]
    zf = zf_ref[...]
    v = v_ref[...]
    ls = jnp.minimum(zf, 0.0) - jnp.log1p(jnp.exp(-jnp.abs(zf)))
    a = lb_ref[0:1, :]
    y = lb_ref[1:2, :] + ls
    lf = jnp.maximum(a, y) + jnp.log1p(jnp.exp(-jnp.abs(a - y)))
    kk = lb_ref[2:3, :] * (1.0 / (1.0 + jnp.exp(zf)))

    rin = lax.broadcasted_iota(I32, (tt, 1), 0) & (r - 1)
    b = lf
    sh = 1
    while sh < r:
        b = b + jnp.where(rin >= sh, pltpu.roll(b, sh, axis=0), 0.0)
        sh *= 2

    rr = lax.broadcasted_iota(I32, (half, half), 0) // HEAD_DIM
    cc = lax.broadcasted_iota(I32, (half, half), 1) // HEAD_DIM
    same_head = rr == cc
    ones_bd = jnp.where(same_head, 1.0, 0.0).astype(BF16)
    bd_mask = jnp.where(same_head, 1.0, 0.0).astype(F32)

    o = jnp.zeros((tt, 2 * half), F32)
    for off in range(r):
        if off == 0:
            e = q * kk
            vs = v
        else:
            e = q * pltpu.roll(kk, off, axis=0) * jnp.exp(b - pltpu.roll(b, off, axis=0))
            e = jnp.where(rin >= off, e, 0.0)
            vs = pltpu.roll(v, off, axis=0)
        eb = e.astype(BF16)
        sc = jnp.concatenate(
            [jnp.dot(eb[:, 0:half], ones_bd, preferred_element_type=F32),
             jnp.dot(eb[:, half:], ones_bd, preferred_element_type=F32)], axis=1)
        o = o + sc * vs

    qe = (q * jnp.exp(b)).astype(BF16)
    vb = v.astype(BF16)
    for c in range(tt // r):
        r0 = c * r
        bl = b[r0 + r - 1:r0 + r, :]
        kd = (kk[r0:r0 + r, :] * jnp.exp(bl - b[r0:r0 + r, :])).astype(BF16)
        dec = jnp.exp(bl)
        for h2 in range(2):
            lo = h2 * half
            st = st_scr[h2]
            oi_scr[r0:r0 + r, lo:lo + half] = _dot_nt(qe[r0:r0 + r, lo:lo + half], st.astype(BF16))
            upd = _dot_tn(vb[r0:r0 + r, lo:lo + half], kd[:, lo:lo + half])
            st_scr[h2] = st * dec[:, lo:lo + half] + upd * bd_mask

    o = o + oi_scr[...]
    o2 = o * o
    o2_hi = o2.astype(BF16)
    o2_lo = (o2 - o2_hi.astype(F32)).astype(BF16)
    ms = jnp.concatenate(
        [jnp.dot(o2_hi[:, 0:half], ones_bd, preferred_element_type=F32)
         + jnp.dot(o2_lo[:, 0:half], ones_bd, preferred_element_type=F32),
         jnp.dot(o2_hi[:, half:], ones_bd, preferred_element_type=F32)
         + jnp.dot(o2_lo[:, half:], ones_bd, preferred_element_type=F32)],
        axis=1) * (1.0 / HEAD_DIM)
    on = o * lax.rsqrt(ms + RMS_EPS) * ng_ref[...] * _sigmoid(g_ref[...])
    o_ref[...] = on.astype(o_ref.dtype)


def _hgrn(zc, lbp, ng, bsz, seq, tt):
    n = zc.shape[0]
    nt = seq // tt
    col = lambda cidx: pl.BlockSpec((tt, 512), lambda b, j: (b * nt + j, cidx))
    kern = functools.partial(_hgrn_kernel, tt=tt)
    return pl.pallas_call(
        kern, grid=(bsz, nt),
        in_specs=[col(0), col(1), col(2), col(3),
                  pl.BlockSpec((3, 512), lambda b, j: (0, 0)),
                  pl.BlockSpec((1, 512), lambda b, j: (0, 0))],
        out_specs=pl.BlockSpec((tt, 512), lambda b, j: (b * nt + j, 0)),
        out_shape=jax.ShapeDtypeStruct((n, 512), BF16),
        scratch_shapes=[pltpu.VMEM((2, 256, 256), F32), pltpu.VMEM((tt, 512), F32)],
        compiler_params=_params("parallel", "arbitrary"))(zc, zc, zc, zc, lbp, ng)


def _layer_norm(y, g, b):
    mu = jnp.mean(y, axis=-1, keepdims=True)
    d = y - mu
    var = jnp.mean(d * d, axis=-1, keepdims=True)
    return d * lax.rsqrt(var + LN_EPS) * g + b


def _merge_kernel(oa_ref, ob_ref, oc_ref, gt_ref, x_ref, wpa_ref, wpb_ref, wpc_ref, wout_ref,
                  g_ref, b_ref, wr_ref, br_ref, x1_ref, xb_ref, ti_ref, gate_ref, *, alpha):
    d = x_ref.shape[1]
    pa = jnp.dot(oa_ref[...], wpa_ref[...], preferred_element_type=F32)
    pb = jnp.dot(ob_ref[...], wpb_ref[...], preferred_element_type=F32)
    pc = jnp.dot(oc_ref[...], wpc_ref[...], preferred_element_type=F32)
    merged = (_sigmoid(gt_ref[:, 0:d]) * pa + _sigmoid(gt_ref[:, d:2 * d]) * pb
              + _sigmoid(gt_ref[:, 2 * d:3 * d]) * pc)
    mix = jnp.dot(merged.astype(BF16), wout_ref[...], preferred_element_type=F32)
    x1 = _layer_norm(alpha * x_ref[...] + mix, g_ref[...], b_ref[...])
    x1_ref[...] = x1
    xb_ref[...] = x1.astype(BF16)

    x_hi = x1.astype(BF16)
    x_lo = (x1 - x_hi.astype(F32)).astype(BF16)
    logits = (jnp.dot(x_hi, wr_ref[0], preferred_element_type=F32)
              + jnp.dot(x_hi, wr_ref[1], preferred_element_type=F32)
              + jnp.dot(x_lo, wr_ref[0], preferred_element_type=F32)) + br_ref[...]
    lane = lax.broadcasted_iota(I32, logits.shape, 1)
    topi = jnp.zeros(logits.shape, I32)
    topv = jnp.full(logits.shape, NEG, F32)
    for k in range(TOP_K):
        m = jnp.max(logits, axis=1, keepdims=True)
        idx = jnp.min(jnp.where(logits == m, lane, LANES), axis=1, keepdims=True)
        topi = jnp.where(lane == k, idx, topi)
        topv = jnp.where(lane == k, m, topv)
        logits = jnp.where(lane == idx, -jnp.inf, logits)
    e = jnp.where(lane < TOP_K, jnp.exp(topv - jnp.max(topv, axis=1, keepdims=True)), 0.0)
    ti_ref[...] = topi
    gate_ref[...] = e / jnp.sum(e, axis=1, keepdims=True)


def _merge(oa, ob, oc, gt, x2, wpa, wpb, wpc, wout, g, b, wr, br, alpha, tm):
    n, d = x2.shape
    row = lambda w: pl.BlockSpec((tm, w), lambda i: (i, 0))
    full = lambda a: pl.BlockSpec(a.shape, lambda i: (0,) * a.ndim)
    kern = functools.partial(_merge_kernel, alpha=alpha)
    return pl.pallas_call(
        kern, grid=(n // tm,),
        in_specs=[row(512), row(512), row(512), row(3 * d), row(d),
                  full(wpa), full(wpb), full(wpc), full(wout), full(g), full(b), full(wr), full(br)],
        out_specs=[row(d), row(d), row(LANES), row(LANES)],
        out_shape=[jax.ShapeDtypeStruct((n, d), F32), jax.ShapeDtypeStruct((n, d), BF16),
                   jax.ShapeDtypeStruct((n, LANES), I32), jax.ShapeDtypeStruct((n, LANES), F32)],
        compiler_params=_params("parallel"))(oa, ob, oc, gt, x2, wpa, wpb, wpc, wout, g, b, wr, br)


def _moe_kernel(be_ref, nu_ref, xs_ref, wgu_ref, bgu_ref, wd_ref, bd_ref, y_ref, wgu_bf, wd_bf):
    i = pl.program_id(0)
    de = wd_ref.shape[1]

    @pl.when((i == 0) | (be_ref[i] != be_ref[jnp.maximum(i - 1, 0)]))
    def _():
        wgu_bf[...] = wgu_ref[0].astype(BF16)
        wd_bf[...] = wd_ref[0].astype(BF16)

    @pl.when(i < nu_ref[0])
    def _():
        gu = jnp.dot(xs_ref[...].astype(BF16), wgu_bf[...], preferred_element_type=F32) + bgu_ref[0]
        gate = jnp.minimum(gu[:, 0:de], SWIGLU_LIMIT)
        up = jnp.clip(gu[:, de:], -SWIGLU_LIMIT, SWIGLU_LIMIT)
        act = gate * _sigmoid(SWIGLU_ALPHA * gate) * (up + 1.0)
        y = jnp.dot(act.astype(BF16), wd_bf[...], preferred_element_type=F32) + bd_ref[0]
        y_ref[...] = y.astype(y_ref.dtype)

    @pl.when(i >= nu_ref[0])
    def _():
        y_ref[...] = jnp.zeros(y_ref.shape, y_ref.dtype)


def _moe_experts(layer, blk_e, n_used, xs, wgu, bgu, wd, bd):
    n_slots, d = xs.shape
    n_blocks = n_slots // MOE_BLOCK
    _, _, n2 = wgu.shape
    de = wd.shape[1]
    base = layer * N_EXPERTS
    grid_spec = pltpu.PrefetchScalarGridSpec(
        num_scalar_prefetch=2, grid=(n_blocks,),
        in_specs=[pl.BlockSpec((MOE_BLOCK, d), lambda i, be, nu: (i, 0)),
                  pl.BlockSpec((1, d, n2), lambda i, be, nu: (base + be[i], 0, 0)),
                  pl.BlockSpec((1, 1, n2), lambda i, be, nu: (base + be[i], 0, 0)),
                  pl.BlockSpec((1, de, d), lambda i, be, nu: (base + be[i], 0, 0)),
                  pl.BlockSpec((1, 1, d), lambda i, be, nu: (base + be[i], 0, 0))],
        out_specs=pl.BlockSpec((MOE_BLOCK, d), lambda i, be, nu: (i, 0)),
        scratch_shapes=[pltpu.VMEM((d, n2), BF16), pltpu.VMEM((de, d), BF16)])
    return pl.pallas_call(
        _moe_kernel, grid_spec=grid_spec,
        out_shape=jax.ShapeDtypeStruct((n_slots, d), BF16),
        compiler_params=_params("arbitrary"))(blk_e, n_used, xs, wgu, bgu, wd, bd)


def _combine_kernel(yg_ref, gate_ref, x1_ref, g_ref, b_ref, o_ref, *, alpha):
    gates = gate_ref[...]
    acc = alpha * x1_ref[...]
    for k in range(TOP_K):
        acc = acc + gates[:, k:k + 1] * yg_ref[k].astype(F32)
    o_ref[...] = _layer_norm(acc, g_ref[...], b_ref[...])


def _combine(yg, gates, x1, g, b, alpha, tm):
    n, d = x1.shape
    kern = functools.partial(_combine_kernel, alpha=alpha)
    return pl.pallas_call(
        kern, grid=(n // tm,),
        in_specs=[pl.BlockSpec((TOP_K, tm, d), lambda i: (0, i, 0)),
                  pl.BlockSpec((tm, LANES), lambda i: (i, 0)),
                  pl.BlockSpec((tm, d), lambda i: (i, 0)),
                  pl.BlockSpec((1, d), lambda i: (0, 0)),
                  pl.BlockSpec((1, d), lambda i: (0, 0))],
        out_specs=pl.BlockSpec((tm, d), lambda i: (i, 0)),
        out_shape=jax.ShapeDtypeStruct((n, d), F32),
        compiler_params=_params("parallel"))(yg, gates, x1, g, b)


def _rope_tables(seq, width):
    half = HEAD_DIM // 2
    inv = ROPE_THETA ** (-jnp.arange(half, dtype=F32) / half)
    ang = jnp.arange(seq, dtype=F32)[:, None] * inv[None, :]
    cos = jnp.concatenate([jnp.cos(ang), jnp.cos(ang)], axis=1)
    sin = jnp.concatenate([-jnp.sin(ang), jnp.sin(ang)], axis=1)
    reps = width // HEAD_DIM
    return jnp.tile(cos, (1, reps)), jnp.tile(sin, (1, reps))


def _split_w_in(w):
    seg = lambda a, b: w[:, _OFF[a]:_OFF[b]]
    w_dsa = seg(0, 3)
    small = jnp.concatenate([seg(5, 6), seg(9, 10), jnp.zeros((w.shape[0], LANES - 12), w.dtype)], axis=1)
    w_idx = jnp.concatenate([seg(3, 4)] + [seg(4, 5)] * IDX_HEADS + [small], axis=1)
    w_fox = seg(6, 9)
    w_hgrn = seg(10, 14)
    w_gate = seg(14, 15)
    return [t.astype(BF16) for t in (w_dsa, w_idx, w_fox, w_hgrn, w_gate)]


def _route(top_idx, n_tok):
    n_assign = n_tok * TOP_K
    flat_e = top_idx.reshape(-1)
    experts = jnp.arange(N_EXPERTS, dtype=I32)
    onehot = (flat_e[None, :] == experts[:, None]).astype(I32)
    csum = jnp.cumsum(onehot, axis=1)
    counts = csum[:, -1]
    rank = jnp.sum(csum * onehot, axis=0) - 1
    padded = (counts + MOE_BLOCK - 1) // MOE_BLOCK * MOE_BLOCK
    pad_end = jnp.cumsum(padded)
    pad_start = pad_end - padded
    start = jnp.cumsum(counts) - counts
    slot_of = pad_start[flat_e] + rank
    n_blocks = -(-(n_assign + N_EXPERTS * (MOE_BLOCK - 1)) // MOE_BLOCK)
    blk_start = jnp.arange(n_blocks, dtype=I32) * MOE_BLOCK
    blk_e = jnp.sum((pad_end[None, :] <= blk_start[:, None]).astype(I32), axis=1)
    blk_e = jnp.minimum(blk_e, N_EXPERTS - 1).astype(I32)
    n_used = (pad_end[-1] // MOE_BLOCK).astype(I32).reshape(1)
    order = jnp.argsort(flat_e).astype(I32)
    slot = jnp.arange(n_blocks * MOE_BLOCK, dtype=I32)
    slot_e = jnp.repeat(blk_e, MOE_BLOCK)
    r = slot - pad_start[slot_e]
    src = jnp.clip(start[slot_e] + r, 0, n_assign - 1)
    slot_tok = jnp.where(r < counts[slot_e], order[src] // TOP_K, slot % n_tok)
    return slot_tok, blk_e, n_used, slot_of.reshape(n_tok, TOP_K).T


def _layer(layer, x2, bsz, seq, alpha, w_in, b_fox, lb, ng, wpa, wpb, wpc, wout, g1, b1,
           wr, br, wgu, bgu, wd, bd, g2, b2, cos, sin):
    n, d = x2.shape
    tm = min(PROJ_TM, seq)
    tmq = min(PROJ_QKV_TM, seq)
    w_dsa, w_idx, w_fox, w_hgrn, w_gate = _split_w_in(w_in)
    nst = seq // tmq
    tab = lambda arr: (arr, pl.BlockSpec((tmq, 512), lambda i: (i % nst, 0)))

    aq, ak, av = _proj_call(_proj_dsa_kernel, x2, w_dsa, [tab(cos), tab(sin)],
                            [(512, BF16)] * 3, tmq)
    iq, ik4, small = _proj_call(_proj_idx_kernel, x2, w_idx, [tab(cos), tab(sin)],
                                [(256, BF16), (256, BF16), (LANES, F32)], tmq)
    fq, fk, fv = _proj_call(_proj_fox_kernel, x2, w_fox, [], [(512, BF16)] * 3, tmq)
    (zc,) = _proj_call(_proj_plain_kernel, x2, w_hgrn, [], [(2048, F32)], tm)
    (gt,) = _proj_call(_proj_plain_kernel, x2, w_gate, [], [(3 * d, F32)], tm)

    tq_att, tk_att = min(FLASH_TQ, seq), min(FLASH_TK, seq)
    iw_t = small[:, 0:8].reshape(bsz, seq, 8).transpose(0, 2, 1)
    mask = _dsa_index(iq, ik4, iw_t, bsz, seq, min(256, seq), min(512, seq))
    o_a = _flash(aq, ak, av, bsz, seq, tq_att, tk_att, mask=mask)

    ff_t = small[:, 4:12].reshape(bsz, seq, 8).transpose(0, 2, 1)
    nb = _fox_bias(ff_t, b_fox).reshape(bsz, 4, 2, seq)
    o_b = _flash(fq, fk, fv, bsz, seq, tq_att, tk_att, nb=nb)

    lbp = jnp.stack([jnp.log(lb), jnp.log1p(-lb), 1.0 - lb]).astype(F32)
    o_c = _hgrn(zc, lbp, ng.reshape(1, -1).astype(F32), bsz, seq, min(HGRN_TILE, seq))

    wr_f = jnp.zeros((d, LANES), F32).at[:, :N_EXPERTS].set(wr.astype(F32))
    wr_hi = wr_f.astype(BF16)
    wr_p = jnp.stack([wr_hi, (wr_f - wr_hi.astype(F32)).astype(BF16)])
    br_p = jnp.full((1, LANES), NEG, F32).at[0, :N_EXPERTS].set(br.astype(F32))
    x1, x1b, topi, gates = _merge(
        o_a, o_b, o_c, gt, x2, wpa.astype(BF16), wpb.astype(BF16), wpc.astype(BF16),
        wout.astype(BF16), g1.reshape(1, d), b1.reshape(1, d), wr_p, br_p, alpha, min(MERGE_TM, n))

    slot_tok, blk_e, n_used, slot_of = _route(topi[:, :TOP_K], n)
    xs = x1b[slot_tok]
    y_slots = _moe_experts(layer, blk_e, n_used, xs, wgu, bgu, wd, bd)
    yg = y_slots[slot_of]
    return _combine(yg, gates, x1, g2.reshape(1, d), b2.reshape(1, d), alpha, min(COMBINE_TM, n))


def kernel(x, w_in, b_fox_f, hgrn_lb_logits, hgrn_norm_g, w_branch_a, w_branch_b, w_branch_c, w_out, ln1_g, ln1_b, w_router, b_router, w_gu, b_gu, w_down, b_down, ln2_g, ln2_b):
    bsz, seq, d = x.shape
    depth = w_in.shape[0]
    alpha = (2 * depth) ** 0.25
    p = jax.nn.softmax(hgrn_lb_logits.astype(F32), axis=0)
    lbs = jnp.cumsum(p, axis=0)
    lbs = lbs - lbs[0]
    cos, sin = _rope_tables(seq, 512)
    x2 = x.reshape(bsz * seq, d)
    n_e = depth * w_gu.shape[1]
    wgu = w_gu.reshape(n_e, d, w_gu.shape[3])
    bgu = b_gu.reshape(n_e, 1, b_gu.shape[2])
    wd = w_down.reshape(n_e, w_down.shape[2], d)
    bd = b_down.reshape(n_e, 1, d)
    for l in range(depth):
        x2 = _layer(l, x2, bsz, seq, alpha, w_in[l], b_fox_f[l], lbs[l], hgrn_norm_g[l],
                    w_branch_a[l], w_branch_b[l], w_branch_c[l], w_out[l], ln1_g[l], ln1_b[l],
                    w_router[l], b_router[l], wgu, bgu, wd, bd, ln2_g[l], ln2_b[l], cos, sin)
    return x2.reshape(bsz, seq, d)
```

```python
import functools
import math

import jax
import jax.numpy as jnp
import numpy as np
from jax import lax
from jax.experimental import pallas as pl
from jax.experimental.pallas import tpu as pltpu

F32 = jnp.float32
BF16 = jnp.bfloat16
I32 = jnp.int32

CHUNK = 64
HEAD_DIM = 64
ROPE_THETA = 10000.0
DSA_TOPK_MAX = 256
IDX_HEADS = 4
N_EXPERTS = 32
TOP_K = 4
SWIGLU_LIMIT = 7.0
SWIGLU_ALPHA = 1.702
MOE_BLOCK = 512
LN_EPS = 1e-5
RMS_EPS = 1e-6

LOG2E = 1.4426950408889634
NEG = -1e30
INT_MIN = -2147483648
INT_MAX = 2147483647
LANES = 128
DSA_PROBES_PER_CHECK = 3
FLASH_TQ = 2048
FLASH_TK = 512
PROJ_TM = 512
PROJ_QKV_TM = 1024
MERGE_TM = 512
COMBINE_TM = 1024
HGRN_TILE = 128
HGRN_SUB = 16
VMEM_LIMIT = 56 * 1024 * 1024

IN_SPLITS = (512, 512, 512, 256, 64, 4, 512, 512, 512, 8, 512, 512, 512, 512, 3072)
_OFF = tuple(int(v) for v in np.cumsum((0,) + IN_SPLITS))


def _params(*sem):
    return pltpu.CompilerParams(dimension_semantics=sem, vmem_limit_bytes=VMEM_LIMIT)


def _dot_nt(a, b):
    return lax.dot_general(a, b, (((1,), (1,)), ((), ())), preferred_element_type=F32)


def _dot_tn(a, b):
    return lax.dot_general(a, b, (((0,), (0,)), ((), ())), preferred_element_type=F32)


def _sigmoid(t):
    return 1.0 / (1.0 + jnp.exp(-t))


def _rope_tile(t, cos, sin):
    w = t.shape[1]
    lane_d = lax.broadcasted_iota(I32, t.shape, 1) & (HEAD_DIM - 1)
    partner = jnp.where(lane_d < HEAD_DIM // 2,
                        pltpu.roll(t, w - HEAD_DIM // 2, axis=1),
                        pltpu.roll(t, HEAD_DIM // 2, axis=1))
    return t * cos + partner * sin


def _proj_dsa_kernel(x_ref, w_ref, cos_ref, sin_ref, q_ref, k_ref, v_ref):
    z = jnp.dot(x_ref[...].astype(BF16), w_ref[...], preferred_element_type=F32)
    cos = cos_ref[...]
    sin = sin_ref[...]
    q_ref[...] = (_rope_tile(z[:, 0:512], cos, sin) * (HEAD_DIM ** -0.5 * LOG2E)).astype(BF16)
    k_ref[...] = _rope_tile(z[:, 512:1024], cos, sin).astype(BF16)
    v_ref[...] = z[:, 1024:1536].astype(BF16)


def _proj_idx_kernel(x_ref, w_ref, cos_ref, sin_ref, iq_ref, ik_ref, sm_ref):
    z = jnp.dot(x_ref[...].astype(BF16), w_ref[...], preferred_element_type=F32)
    cos = cos_ref[...][:, 0:256]
    sin = sin_ref[...][:, 0:256]
    iq_ref[...] = (_rope_tile(z[:, 0:256], cos, sin) * (HEAD_DIM ** -0.5)).astype(BF16)
    ik_ref[...] = _rope_tile(z[:, 256:512], cos, sin).astype(BF16)
    sm_ref[...] = z[:, 512:640]


def _proj_fox_kernel(x_ref, w_ref, q_ref, k_ref, v_ref):
    z = jnp.dot(x_ref[...].astype(BF16), w_ref[...], preferred_element_type=F32)
    q_ref[...] = (z[:, 0:512] * (HEAD_DIM ** -0.5 * LOG2E)).astype(BF16)
    k_ref[...] = z[:, 512:1024].astype(BF16)
    v_ref[...] = z[:, 1024:1536].astype(BF16)


def _proj_plain_kernel(x_ref, w_ref, o_ref):
    o_ref[...] = jnp.dot(x_ref[...].astype(BF16), w_ref[...], preferred_element_type=F32)


def _proj_call(kernel_fn, x2, w, extra, outs, tm):
    n, d = x2.shape
    nout = w.shape[1]
    in_specs = [pl.BlockSpec((tm, d), lambda i: (i, 0)),
                pl.BlockSpec((d, nout), lambda i: (0, 0))]
    args = [x2, w]
    for arr, spec in extra:
        in_specs.append(spec)
        args.append(arr)
    out_shape = [jax.ShapeDtypeStruct((n, wdt), dt) for wdt, dt in outs]
    out_specs = [pl.BlockSpec((tm, wdt), lambda i: (i, 0)) for wdt, _ in outs]
    return pl.pallas_call(
        kernel_fn, grid=(n // tm,), in_specs=in_specs, out_specs=out_specs,
        out_shape=out_shape, compiler_params=_params("parallel"))(*args)


def _fox_bias_kernel(ff_ref, b_ref, nb_ref):
    s = ff_ref.shape[2]
    z = ff_ref[0] + b_ref[...]
    lf = jnp.minimum(z, 0.0) - jnp.log1p(jnp.exp(-jnp.abs(z)))
    r = lax.broadcasted_iota(I32, (LANES, LANES), 0)
    c = lax.broadcasted_iota(I32, (LANES, LANES), 1)
    upper = jnp.where(r <= c, 1.0, 0.0).astype(F32)
    carry = jnp.zeros((lf.shape[0], 1), F32)
    for t in range(s // LANES):
        blk = lf[:, t * LANES:(t + 1) * LANES]
        cs = jnp.dot(blk, upper, preferred_element_type=F32,
                     precision=lax.Precision.HIGHEST) + carry
        nb_ref[0, :, t * LANES:(t + 1) * LANES] = cs * (-LOG2E)
        carry = cs[:, LANES - 1:LANES]


def _fox_bias(ff_t, b_fox):
    bsz, h, s = ff_t.shape
    return pl.pallas_call(
        _fox_bias_kernel, grid=(bsz,),
        in_specs=[pl.BlockSpec((1, h, s), lambda b: (b, 0, 0)),
                  pl.BlockSpec((h, 1), lambda b: (0, 0))],
        out_specs=pl.BlockSpec((1, h, s), lambda b: (b, 0, 0)),
        out_shape=jax.ShapeDtypeStruct((bsz, h, s), F32),
        compiler_params=_params("parallel"))(ff_t, b_fox.reshape(h, 1).astype(F32))


def _dsa_index_kernel(iq_ref, ik_ref, iwt_ref, bias_ref, keys_scr, *, tq, tk, seq, ksel, cnt_tiles, tiled):
    i = pl.program_id(1)
    n_kt = ((i + 1) * tq + tk - 1) // tk
    q_pos = i * tq + lax.broadcasted_iota(I32, (1, tq), 1)
    adm_end = (q_pos // CHUNK + 1) * CHUNK

    iq = iq_ref[...]
    lane = lax.broadcasted_iota(I32, iq.shape, 1)
    iqm = [jnp.where((lane >= HEAD_DIM * h) & (lane < HEAD_DIM * (h + 1)), iq, jnp.zeros_like(iq))
           for h in range(IDX_HEADS)]
    wts = [iwt_ref[0, h:h + 1, :] * (IDX_HEADS ** -0.5) for h in range(IDX_HEADS)]

    def key_idx(k0):
        return k0 + lax.broadcasted_iota(I32, (tk, 1), 0)

    def to_key(v):
        bits = pltpu.bitcast(v, I32)
        sign = bits >> 31
        return (bits ^ (sign & 0x7FFFFFFF)) - sign

    def from_key(k):
        sign = k >> 31
        return pltpu.bitcast((k + sign) ^ (sign & 0x7FFFFFFF), F32)

    def score_tile(kt, carry):
        kmax, kmin = carry
        k0 = pl.multiple_of(kt * tk, tk)
        ikt = ik_ref[pl.ds(k0, tk), :]
        sc = jnp.zeros((tk, tq), F32)
        for h in range(IDX_HEADS):
            sc = sc + jnp.maximum(_dot_nt(ikt, iqm[h]), 0.0) * wts[h]
        key = to_key(sc)
        adm = key_idx(k0) < adm_end
        k_lo = jnp.where(adm, key, INT_MIN)
        k_hi = jnp.where(adm, key, INT_MAX)
        keys_scr[pl.ds(k0, tk), :] = k_lo
        for g in range(tk // 8):
            kmax = jnp.maximum(kmax, k_lo[g * 8:(g + 1) * 8, :])
            kmin = jnp.minimum(kmin, k_hi[g * 8:(g + 1) * 8, :])
        return kmax, kmin

    kmax, kmin = lax.fori_loop(0, n_kt, score_tile,
                               (jnp.full((8, tq), INT_MIN, I32), jnp.full((8, tq), INT_MAX, I32)))
    kmax = jnp.max(kmax, axis=0, keepdims=True)
    kmin = jnp.min(kmin, axis=0, keepdims=True)

    kf = float(ksel)
    n_acc = 4
    grp = 8 * n_acc

    tc = cnt_tiles * tk
    n_ct = (n_kt + cnt_tiles - 1) // cnt_tiles

    def pad_tile(kt, carry):
        keys_scr[pl.ds(pl.multiple_of(kt * tk, tk), tk), :] = jnp.full((tk, tq), INT_MIN, I32)
        return carry

    lax.fori_loop(n_kt, n_ct * cnt_tiles, pad_tile, 0)

    def count(pred):
        def body(kt, accs):
            k0 = pl.multiple_of(kt * tc, tc)
            accs = list(accs)
            for g in range(tc // grp):
                kk = keys_scr[pl.ds(k0 + g * grp, grp), :]
                hit = jnp.where(pred(kk, k0 + g * grp), 1.0, 0.0)
                for a in range(n_acc):
                    accs[a] = accs[a] + hit[a * 8:(a + 1) * 8, :]
            return tuple(accs)
        accs = lax.fori_loop(0, n_ct, body, tuple(jnp.zeros((8, tq), F32) for _ in range(n_acc)))
        tot = accs[0]
        for a in range(1, n_acc):
            tot = tot + accs[a]
        return jnp.sum(tot, axis=0, keepdims=True)

    def count_ge(cand):
        return count(lambda kk, k0: kk >= cand)

    n_adm = adm_end.astype(F32)
    zero = jnp.zeros((1, tq), F32)
    one = jnp.ones((1, tq), F32)
    log_k = math.log(kf)
    done0 = jnp.where(n_adm <= kf, 1.0, 0.0)
    init = (jnp.int32(0), kmin, kmax + 1, n_adm, zero, jnp.full((1, tq), INT_MIN, I32), done0, zero,
            one, one, zero)

    def cond(st):
        return (st[0] < 264) & (jnp.min(st[6]) == 0.0)

    def step(st):
        it, lo, hi, c_lo, c_hi, thr, done, tie, w_lo, w_hi, last = st
        lo_v = from_key(lo)
        hi_v = from_key(hi)
        f_lo = (jnp.log(c_lo + 0.5) - log_k) * w_lo
        f_hi = (log_k - jnp.log(c_hi + 0.5)) * w_hi
        cand = to_key(lo_v + (hi_v - lo_v) * (f_lo / (f_lo + f_hi)))
        cand = jnp.where(((it - 3) & 7) == 7, (lo >> 1) + (hi >> 1) + (lo & hi & 1), cand)
        cand = jnp.where(it == 0, kmax, cand)
        cand = jnp.where(it == 1, 0, cand)
        cand = jnp.where(it == 2, 1, cand)
        cand = jnp.minimum(jnp.maximum(cand, lo + 1), hi - 1)
        c = count_ge(cand)
        active = done == 0.0
        up = active & (c >= kf)
        down = active & (c < kf)
        lo = jnp.where(up, cand, lo)
        c_lo = jnp.where(up, c, c_lo)
        hi = jnp.where(down, cand, hi)
        c_hi = jnp.where(down, c, c_hi)
        w_hi = jnp.where(up & (last > 0.0), 0.5 * w_hi, jnp.where(down, 1.0, w_hi))
        w_lo = jnp.where(down & (last < 0.0), 0.5 * w_lo, jnp.where(up, 1.0, w_lo))
        last = jnp.where(up, 1.0, jnp.where(down, -1.0, last))
        hit = active & (c == kf)
        conv = active & jnp.logical_not(hit) & (hi - 1 <= lo)
        thr = jnp.where(hit, cand - 1, jnp.where(conv, lo, thr))
        tie = jnp.where(conv, 1.0, tie)
        done = jnp.where(hit | conv, 1.0, done)
        return it + 1, lo, hi, c_lo, c_hi, thr, done, tie, w_lo, w_hi, last

    def steps(st):
        for _ in range(DSA_PROBES_PER_CHECK):
            st = step(st)
        return st

    st = lax.while_loop(cond, steps, init)
    c_hi, thr, tie = st[4], st[5], st[7]

    def store_tile(kt, k0, val):
        if tiled:
            bias_ref[0, kt] = val
        else:
            bias_ref[0, :, pl.ds(k0, tk)] = val
    any_tie = jnp.max(tie) > 0.0

    @pl.when(jnp.logical_not(any_tie))
    def _():
        def write_tile(kt, carry):
            k0 = pl.multiple_of(kt * tk, tk)
            sel = keys_scr[pl.ds(k0, tk), :] > thr
            store_tile(kt, k0, jnp.where(sel, 0.0, NEG).T.astype(BF16))
            return carry

        lax.fori_loop(0, n_kt, write_tile, 0)

    @pl.when(any_tie)
    def _():
        need = jnp.where(tie > 0.0, kf - c_hi, 0.0)
        lower = jnp.where(lax.broadcasted_iota(I32, (tk, tk), 0) >= lax.broadcasted_iota(I32, (tk, tk), 1),
                          1.0, 0.0).astype(BF16)

        def write_tile(kt, seen):
            k0 = pl.multiple_of(kt * tk, tk)
            kk = keys_scr[pl.ds(k0, tk), :]
            eq = kk == thr
            rank = jnp.dot(lower, jnp.where(eq, 1.0, 0.0).astype(BF16), preferred_element_type=F32) + seen
            sel = (kk > thr) | (eq & (rank <= need))
            store_tile(kt, k0, jnp.where(sel, 0.0, NEG).T.astype(BF16))
            return rank[tk - 1:tk, :]

        lax.fori_loop(0, n_kt, write_tile, jnp.zeros((1, tq), F32))

    def fill_tile(kt, carry):
        k0 = pl.multiple_of(kt * tk, tk)
        store_tile(kt, k0, jnp.full((tq, tk), NEG, BF16))
        return carry

    lax.fori_loop(n_kt, seq // tk, fill_tile, 0)


def _dsa_index(iq, ik4, iw_t, bsz, seq, tq, tk, cnt_tiles=1, tiled=False):
    ksel = min(DSA_TOPK_MAX, seq // 4)
    nq = seq // tq
    kern = functools.partial(_dsa_index_kernel, tq=tq, tk=tk, seq=seq, ksel=ksel, cnt_tiles=cnt_tiles,
                             tiled=tiled)
    return pl.pallas_call(
        kern, grid=(bsz, nq),
        in_specs=[pl.BlockSpec((tq, 256), lambda b, i: (b * nq + i, 0)),
                  pl.BlockSpec((seq, 256), lambda b, i: (b, 0)),
                  pl.BlockSpec((1, 8, tq), lambda b, i: (b, 0, i))],
        out_specs=(pl.BlockSpec((1, seq // tk, tq, tk), lambda b, i: (b, 0, i, 0)) if tiled
                   else pl.BlockSpec((1, tq, seq), lambda b, i: (b, i, 0))),
        out_shape=jax.ShapeDtypeStruct((bsz, seq // tk, seq, tk) if tiled else (bsz, seq, seq), BF16),
        scratch_shapes=[pltpu.VMEM((seq, tq), I32)],
        compiler_params=_params("parallel", "parallel"))(iq, ik4, iw_t)


def _flash_kernel(*refs, tq, tk, use_mask, tiled):
    if use_mask:
        q_ref, k_ref, v_ref, mask_ref, o_ref, m_scr, acc_scr, mbuf, sem = refs
        nb_ref = None
    else:
        q_ref, k_ref, v_ref, nb_ref, o_ref, m_scr, acc_scr = refs
        mask_ref = None
    b = pl.program_id(0)
    i = pl.program_id(2)
    q = q_ref[...]
    head0 = lax.broadcasted_iota(I32, (tq, LANES), 1) < HEAD_DIM
    head0k = lax.broadcasted_iota(I32, (tk, LANES), 1) < HEAD_DIM
    qm = [jnp.where(head0, q, jnp.zeros_like(q)), jnp.where(head0, jnp.zeros_like(q), q)]
    m_scr[...] = jnp.full(m_scr.shape, NEG, F32)
    acc_scr[...] = jnp.zeros(acc_scr.shape, F32)
    n_col = tk // LANES
    n_sub = tq // tk
    n_off = i * n_sub

    def mask_dma(kt, slot, r0):
        return pltpu.make_async_copy(
            (mask_ref.at[b, kt, pl.ds(i * tq + r0, tq - r0), :] if tiled else
             mask_ref.at[b, pl.ds(i * tq + r0, tq - r0), pl.ds(pl.multiple_of(kt * tk, tk), tk)]),
            mbuf.at[slot, pl.ds(r0, tq - r0), :], sem.at[slot])

    def tile(kt, diag):
        r0 = 0 if diag is None else diag * tk
        nr = tq - r0
        k0 = pl.multiple_of(kt * tk, tk)
        kt_ = k_ref[pl.ds(k0, tk), :]
        vt = v_ref[pl.ds(k0, tk), :]
        one = jnp.ones_like(vt)
        vx = [jnp.where(head0k, vt, one), jnp.where(head0k, one, vt)]
        if use_mask:
            slot = kt & 1
            if diag is None:
                mask_dma(kt + 1, 1 - slot, 0).start()
            elif diag + 1 < n_sub:
                mask_dma(kt + 1, 1 - slot, (diag + 1) * tk).start()
            mask_dma(kt, slot, r0).wait()
            extra = mbuf[slot, pl.ds(r0, nr), :].astype(F32)
        ss = []
        for j in range(2):
            s = _dot_nt(qm[j][r0:, :], kt_)
            if use_mask:
                s = s + extra
            else:
                s = s + nb_ref[0, 0, j:j + 1, pl.ds(k0, tk)]
                if diag is not None:
                    rr = lax.broadcasted_iota(I32, (nr, tk), 0)
                    cc = lax.broadcasted_iota(I32, (nr, tk), 1)
                    s = jnp.where(cc <= rr, s, NEG)
            ss.append(s)
        ps = []
        for j in range(2):
            sc = [ss[j][:, c * LANES:(c + 1) * LANES] for c in range(n_col)]
            part = sc[0]
            for c in range(1, n_col):
                part = jnp.maximum(part, sc[c])
            m_old = m_scr[j, r0:, :]
            m_new = jnp.maximum(m_old, jnp.max(part, axis=1, keepdims=True))
            alpha = jnp.exp2(m_old - m_new)
            ps.append((alpha, jnp.concatenate([jnp.exp2(c_ - m_new) for c_ in sc], axis=1).astype(BF16)))
            m_scr[j, r0:, :] = m_new
        for j in range(2):
            alpha, p = ps[j]
            acc_scr[j, r0:, :] = alpha * acc_scr[j, r0:, :] + jnp.dot(p, vx[j], preferred_element_type=F32)

    def body(kt, carry):
        tile(kt, None)
        return carry

    if use_mask:
        mask_dma(0, 0, 0).start()
    lax.fori_loop(0, n_off, body, 0)
    for d in range(n_sub):
        tile(n_off + d, d)

    a0 = acc_scr[0]
    a1 = acc_scr[1]
    o0 = a0 / pltpu.roll(a0, HEAD_DIM, axis=1)
    o1 = a1 / pltpu.roll(a1, HEAD_DIM, axis=1)
    o_ref[...] = jnp.where(head0, o0, o1).astype(o_ref.dtype)


def _flash(q, k, v, bsz, seq, tq, tk, nb=None, mask=None):
    nq = seq // tq
    n_hp = q.shape[1] // LANES
    use_mask = mask is not None
    in_specs = [pl.BlockSpec((tq, LANES), lambda b, h, i: (b * nq + i, h)),
                pl.BlockSpec((seq, LANES), lambda b, h, i: (b, h)),
                pl.BlockSpec((seq, LANES), lambda b, h, i: (b, h))]
    scratch = [pltpu.VMEM((2, tq, LANES), F32), pltpu.VMEM((2, tq, LANES), F32)]
    if use_mask:
        in_specs.append(pl.BlockSpec(memory_space=pl.ANY))
        scratch += [pltpu.VMEM((2, tq, tk), BF16), pltpu.SemaphoreType.DMA((2,))]
        extra = mask
    else:
        in_specs.append(pl.BlockSpec((1, 1, 2, seq), lambda b, h, i: (b, h, 0, 0)))
        extra = nb
    tiled = use_mask and mask.ndim == 4
    assert not tiled or mask.shape[3] == tk
    kern = functools.partial(_flash_kernel, tq=tq, tk=tk, use_mask=use_mask, tiled=tiled)
    return pl.pallas_call(
        kern, grid=(bsz, n_hp, nq), in_specs=in_specs,
        out_specs=pl.BlockSpec((tq, LANES), lambda b, h, i: (b * nq + i, h)),
        out_shape=jax.ShapeDtypeStruct(q.shape, BF16),
        scratch_shapes=scratch,
        compiler_params=_params("parallel", "parallel", "parallel"))(q, k, v, extra)


def _hgrn_kernel(q_ref, zf_ref, v_ref, g_ref, lb_ref, ng_ref, o_ref, st_scr, oi_scr, *, tt):
    r = HGRN_SUB
    half = 256

    @pl.when(pl.program_id(1) == 0)
    def _():
        st_scr[...] = jnp.zeros(st_scr.shape, F32)

    q = q_ref[...]
    zf = zf_ref[...]
    v = v_ref[...]
    ls = jnp.minimum(zf, 0.0) - jnp.log1p(jnp.exp(-jnp.abs(zf)))
    a = lb_ref[0:1, :]
    y = lb_ref[1:2, :] + ls
    lf = jnp.maximum(a, y) + jnp.log1p(jnp.exp(-jnp.abs(a - y)))
    kk = lb_ref[2:3, :] * (1.0 / (1.0 + jnp.exp(zf)))

    rin = lax.broadcasted_iota(I32, (tt, 1), 0) & (r - 1)
    b = lf
    sh = 1
    while sh < r:
        b = b + jnp.where(rin >= sh, pltpu.roll(b, sh, axis=0), 0.0)
        sh *= 2

    rr = lax.broadcasted_iota(I32, (half, half), 0) // HEAD_DIM
    cc = lax.broadcasted_iota(I32, (half, half), 1) // HEAD_DIM
    same_head = rr == cc
    ones_bd = jnp.where(same_head, 1.0, 0.0).astype(BF16)
    bd_mask = jnp.where(same_head, 1.0, 0.0).astype(F32)

    o = jnp.zeros((tt, 2 * half), F32)
    for off in range(r):
        if off == 0:
            e = q * kk
            vs = v
        else:
            e = q * pltpu.roll(kk, off, axis=0) * jnp.exp(b - pltpu.roll(b, off, axis=0))
            e = jnp.where(rin >= off, e, 0.0)
            vs = pltpu.roll(v, off, axis=0)
        eb = e.astype(BF16)
        sc = jnp.concatenate(
            [jnp.dot(eb[:, 0:half], ones_bd, preferred_element_type=F32),
             jnp.dot(eb[:, half:], ones_bd, preferred_element_type=F32)], axis=1)
        o = o + sc * vs

    qe = (q * jnp.exp(b)).astype(BF16)
    vb = v.astype(BF16)
    for c in range(tt // r):
        r0 = c * r
        bl = b[r0 + r - 1:r0 + r, :]
        kd = (kk[r0:r0 + r, :] * jnp.exp(bl - b[r0:r0 + r, :])).astype(BF16)
        dec = jnp.exp(bl)
        for h2 in range(2):
            lo = h2 * half
            st = st_scr[h2]
            oi_scr[r0:r0 + r, lo:lo + half] = _dot_nt(qe[r0:r0 + r, lo:lo + half], st.astype(BF16))
            upd = _dot_tn(vb[r0:r0 + r, lo:lo + half], kd[:, lo:lo + half])
            st_scr[h2] = st * dec[:, lo:lo + half] + upd * bd_mask

    o = o + oi_scr[...]
    o2 = o * o
    o2_hi = o2.astype(BF16)
    o2_lo = (o2 - o2_hi.astype(F32)).astype(BF16)
    ms = jnp.concatenate(
        [jnp.dot(o2_hi[:, 0:half], ones_bd, preferred_element_type=F32)
         + jnp.dot(o2_lo[:, 0:half], ones_bd, preferred_element_type=F32),
         jnp.dot(o2_hi[:, half:], ones_bd, preferred_element_type=F32)
         + jnp.dot(o2_lo[:, half:], ones_bd, preferred_element_type=F32)],
        axis=1) * (1.0 / HEAD_DIM)
    on = o * lax.rsqrt(ms + RMS_EPS) * ng_ref[...] * _sigmoid(g_ref[...])
    o_ref[...] = on.astype(o_ref.dtype)


def _hgrn(zc, lbp, ng, bsz, seq, tt):
    n = zc.shape[0]
    nt = seq // tt
    col = lambda cidx: pl.BlockSpec((tt, 512), lambda b, j: (b * nt + j, cidx))
    kern = functools.partial(_hgrn_kernel, tt=tt)
    return pl.pallas_call(
        kern, grid=(bsz, nt),
        in_specs=[col(0), col(1), col(2), col(3),
                  pl.BlockSpec((3, 512), lambda b, j: (0, 0)),
                  pl.BlockSpec((1, 512), lambda b, j: (0, 0))],
        out_specs=pl.BlockSpec((tt, 512), lambda b, j: (b * nt + j, 0)),
        out_shape=jax.ShapeDtypeStruct((n, 512), BF16),
        scratch_shapes=[pltpu.VMEM((2, 256, 256), F32), pltpu.VMEM((tt, 512), F32)],
        compiler_params=_params("parallel", "arbitrary"))(zc, zc, zc, zc, lbp, ng)


def _layer_norm(y, g, b):
    mu = jnp.mean(y, axis=-1, keepdims=True)
    d = y - mu
    var = jnp.mean(d * d, axis=-1, keepdims=True)
    return d * lax.rsqrt(var + LN_EPS) * g + b


def _merge_kernel(oa_ref, ob_ref, oc_ref, gt_ref, x_ref, wpa_ref, wpb_ref, wpc_ref, wout_ref,
                  g_ref, b_ref, wr_ref, br_ref, x1_ref, xb_ref, ti_ref, gate_ref, *, alpha):
    d = x_ref.shape[1]
    pa = jnp.dot(oa_ref[...], wpa_ref[...], preferred_element_type=F32)
    pb = jnp.dot(ob_ref[...], wpb_ref[...], preferred_element_type=F32)
    pc = jnp.dot(oc_ref[...], wpc_ref[...], preferred_element_type=F32)
    merged = (_sigmoid(gt_ref[:, 0:d]) * pa + _sigmoid(gt_ref[:, d:2 * d]) * pb
              + _sigmoid(gt_ref[:, 2 * d:3 * d]) * pc)
    mix = jnp.dot(merged.astype(BF16), wout_ref[...], preferred_element_type=F32)
    x1 = _layer_norm(alpha * x_ref[...] + mix, g_ref[...], b_ref[...])
    x1_ref[...] = x1
    xb_ref[...] = x1.astype(BF16)

    x_hi = x1.astype(BF16)
    x_lo = (x1 - x_hi.astype(F32)).astype(BF16)
    logits = (jnp.dot(x_hi, wr_ref[0], preferred_element_type=F32)
              + jnp.dot(x_hi, wr_ref[1], preferred_element_type=F32)
              + jnp.dot(x_lo, wr_ref[0], preferred_element_type=F32)) + br_ref[...]
    lane = lax.broadcasted_iota(I32, logits.shape, 1)
    topi = jnp.zeros(logits.shape, I32)
    topv = jnp.full(logits.shape, NEG, F32)
    for k in range(TOP_K):
        m = jnp.max(logits, axis=1, keepdims=True)
        idx = jnp.min(jnp.where(logits == m, lane, LANES), axis=1, keepdims=True)
        topi = jnp.where(lane == k, idx, topi)
        topv = jnp.where(lane == k, m, topv)
        logits = jnp.where(lane == idx, -jnp.inf, logits)
    e = jnp.where(lane < TOP_K, jnp.exp(topv - jnp.max(topv, axis=1, keepdims=True)), 0.0)
    ti_ref[...] = topi
    gate_ref[...] = e / jnp.sum(e, axis=1, keepdims=True)


def _merge(oa, ob, oc, gt, x2, wpa, wpb, wpc, wout, g, b, wr, br, alpha, tm):
    n, d = x2.shape
    row = lambda w: pl.BlockSpec((tm, w), lambda i: (i, 0))
    full = lambda a: pl.BlockSpec(a.shape, lambda i: (0,) * a.ndim)
    kern = functools.partial(_merge_kernel, alpha=alpha)
    return pl.pallas_call(
        kern, grid=(n // tm,),
        in_specs=[row(512), row(512), row(512), row(3 * d), row(d),
                  full(wpa), full(wpb), full(wpc), full(wout), full(g), full(b), full(wr), full(br)],
        out_specs=[row(d), row(d), row(LANES), row(LANES)],
        out_shape=[jax.ShapeDtypeStruct((n, d), F32), jax.ShapeDtypeStruct((n, d), BF16),
                   jax.ShapeDtypeStruct((n, LANES), I32), jax.ShapeDtypeStruct((n, LANES), F32)],
        compiler_params=_params("parallel"))(oa, ob, oc, gt, x2, wpa, wpb, wpc, wout, g, b, wr, br)


def _moe_kernel(be_ref, nu_ref, xs_ref, wgu_ref, bgu_ref, wd_ref, bd_ref, y_ref, wgu_bf, wd_bf):
    i = pl.program_id(0)
    de = wd_ref.shape[1]

    @pl.when((i == 0) | (be_ref[i] != be_ref[jnp.maximum(i - 1, 0)]))
    def _():
        wgu_bf[...] = wgu_ref[0].astype(BF16)
        wd_bf[...] = wd_ref[0].astype(BF16)

    @pl.when(i < nu_ref[0])
    def _():
        gu = jnp.dot(xs_ref[...].astype(BF16), wgu_bf[...], preferred_element_type=F32) + bgu_ref[0]
        gate = jnp.minimum(gu[:, 0:de], SWIGLU_LIMIT)
        up = jnp.clip(gu[:, de:], -SWIGLU_LIMIT, SWIGLU_LIMIT)
        act = gate * _sigmoid(SWIGLU_ALPHA * gate) * (up + 1.0)
        y = jnp.dot(act.astype(BF16), wd_bf[...], preferred_element_type=F32) + bd_ref[0]
        y_ref[...] = y.astype(y_ref.dtype)

    @pl.when(i >= nu_ref[0])
    def _():
        y_ref[...] = jnp.zeros(y_ref.shape, y_ref.dtype)


def _moe_experts(layer, blk_e, n_used, xs, wgu, bgu, wd, bd):
    n_slots, d = xs.shape
    n_blocks = n_slots // MOE_BLOCK
    _, _, n2 = wgu.shape
    de = wd.shape[1]
    base = layer * N_EXPERTS
    grid_spec = pltpu.PrefetchScalarGridSpec(
        num_scalar_prefetch=2, grid=(n_blocks,),
        in_specs=[pl.BlockSpec((MOE_BLOCK, d), lambda i, be, nu: (i, 0)),
                  pl.BlockSpec((1, d, n2), lambda i, be, nu: (base + be[i], 0, 0)),
                  pl.BlockSpec((1, 1, n2), lambda i, be, nu: (base + be[i], 0, 0)),
                  pl.BlockSpec((1, de, d), lambda i, be, nu: (base + be[i], 0, 0)),
                  pl.BlockSpec((1, 1, d), lambda i, be, nu: (base + be[i], 0, 0))],
        out_specs=pl.BlockSpec((MOE_BLOCK, d), lambda i, be, nu: (i, 0)),
        scratch_shapes=[pltpu.VMEM((d, n2), BF16), pltpu.VMEM((de, d), BF16)])
    return pl.pallas_call(
        _moe_kernel, grid_spec=grid_spec,
        out_shape=jax.ShapeDtypeStruct((n_slots, d), BF16),
        compiler_params=_params("arbitrary"))(blk_e, n_used, xs, wgu, bgu, wd, bd)


def _combine_kernel(yg_ref, gate_ref, x1_ref, g_ref, b_ref, o_ref, *, alpha):
    gates = gate_ref[...]
    acc = alpha * x1_ref[...]
    for k in range(TOP_K):
        acc = acc + gates[:, k:k + 1] * yg_ref[k].astype(F32)
    o_ref[...] = _layer_norm(acc, g_ref[...], b_ref[...])


def _combine(yg, gates, x1, g, b, alpha, tm):
    n, d = x1.shape
    kern = functools.partial(_combine_kernel, alpha=alpha)
    return pl.pallas_call(
        kern, grid=(n // tm,),
        in_specs=[pl.BlockSpec((TOP_K, tm, d), lambda i: (0, i, 0)),
                  pl.BlockSpec((tm, LANES), lambda i: (i, 0)),
                  pl.BlockSpec((tm, d), lambda i: (i, 0)),
                  pl.BlockSpec((1, d), lambda i: (0, 0)),
                  pl.BlockSpec((1, d), lambda i: (0, 0))],
        out_specs=pl.BlockSpec((tm, d), lambda i: (i, 0)),
        out_shape=jax.ShapeDtypeStruct((n, d), F32),
        compiler_params=_params("parallel"))(yg, gates, x1, g, b)


def _rope_tables(seq, width):
    half = HEAD_DIM // 2
    inv = ROPE_THETA ** (-jnp.arange(half, dtype=F32) / half)
    ang = jnp.arange(seq, dtype=F32)[:, None] * inv[None, :]
    cos = jnp.concatenate([jnp.cos(ang), jnp.cos(ang)], axis=1)
    sin = jnp.concatenate([-jnp.sin(ang), jnp.sin(ang)], axis=1)
    reps = width // HEAD_DIM
    return jnp.tile(cos, (1, reps)), jnp.tile(sin, (1, reps))


def _split_w_in(w):
    seg = lambda a, b: w[:, _OFF[a]:_OFF[b]]
    w_dsa = seg(0, 3)
    small = jnp.concatenate([seg(5, 6), seg(9, 10), jnp.zeros((w.shape[0], LANES - 12), w.dtype)], axis=1)
    w_idx = jnp.concatenate([seg(3, 4)] + [seg(4, 5)] * IDX_HEADS + [small], axis=1)
    w_fox = seg(6, 9)
    w_hgrn = seg(10, 14)
    w_gate = seg(14, 15)
    return [t.astype(BF16) for t in (w_dsa, w_idx, w_fox, w_hgrn, w_gate)]


def _route(top_idx, n_tok):
    n_assign = n_tok * TOP_K
    flat_e = top_idx.reshape(-1)
    experts = jnp.arange(N_EXPERTS, dtype=I32)
    onehot = (flat_e[None, :] == experts[:, None]).astype(F32)
    oh3 = onehot.reshape(N_EXPERTS, n_assign // LANES, LANES)
    upper = (jnp.arange(LANES)[:, None] <= jnp.arange(LANES)[None, :]).astype(F32)
    within = jnp.einsum('ebl,lm->ebm', oh3, upper)
    btot = within[:, :, LANES - 1]
    csum = (within + (jnp.cumsum(btot, axis=1) - btot)[:, :, None]).reshape(N_EXPERTS, n_assign)
    counts = jnp.sum(btot, axis=1).astype(I32)
    rank = (jnp.sum(csum * onehot, axis=0) - 1.0).astype(I32)
    padded = (counts + MOE_BLOCK - 1) // MOE_BLOCK * MOE_BLOCK
    pad_end = jnp.cumsum(padded)
    pad_start = pad_end - padded
    start = jnp.cumsum(counts) - counts
    slot_of = pad_start[flat_e] + rank
    n_blocks = -(-(n_assign + N_EXPERTS * (MOE_BLOCK - 1)) // MOE_BLOCK)
    blk_start = jnp.arange(n_blocks, dtype=I32) * MOE_BLOCK
    blk_e = jnp.sum((pad_end[None, :] <= blk_start[:, None]).astype(I32), axis=1)
    blk_e = jnp.minimum(blk_e, N_EXPERTS - 1).astype(I32)
    n_used = (pad_end[-1] // MOE_BLOCK).astype(I32).reshape(1)
    order = jnp.argsort(flat_e).astype(I32)
    slot = jnp.arange(n_blocks * MOE_BLOCK, dtype=I32)
    slot_e = jnp.repeat(blk_e, MOE_BLOCK)
    r = slot - pad_start[slot_e]
    src = jnp.clip(start[slot_e] + r, 0, n_assign - 1)
    slot_tok = jnp.where(r < counts[slot_e], order[src] // TOP_K, slot % n_tok)
    return slot_tok, blk_e, n_used, slot_of.reshape(n_tok, TOP_K).T


def _layer(layer, x2, bsz, seq, alpha, w_in, b_fox, lb, ng, wpa, wpb, wpc, wout, g1, b1,
           wr, br, wgu, bgu, wd, bd, g2, b2, cos, sin):
    n, d = x2.shape
    tm = min(PROJ_TM, seq)
    tmq = min(PROJ_QKV_TM, seq)
    w_dsa, w_idx, w_fox, w_hgrn, w_gate = _split_w_in(w_in)
    nst = seq // tmq
    tab = lambda arr: (arr, pl.BlockSpec((tmq, 512), lambda i: (i % nst, 0)))

    aq, ak, av = _proj_call(_proj_dsa_kernel, x2, w_dsa, [tab(cos), tab(sin)],
                            [(512, BF16)] * 3, tmq)
    iq, ik4, small = _proj_call(_proj_idx_kernel, x2, w_idx, [tab(cos), tab(sin)],
                                [(256, BF16), (256, BF16), (LANES, F32)], tmq)
    fq, fk, fv = _proj_call(_proj_fox_kernel, x2, w_fox, [], [(512, BF16)] * 3, tmq)
    (zc,) = _proj_call(_proj_plain_kernel, x2, w_hgrn, [], [(2048, F32)], tm)
    (gt,) = _proj_call(_proj_plain_kernel, x2, w_gate, [], [(3 * d, F32)], tm)

    tq_att, tk_att = min(FLASH_TQ, seq), min(FLASH_TK, seq)
    iw_t = small[:, 0:8].reshape(bsz, seq, 8).transpose(0, 2, 1)
    mask = _dsa_index(iq, ik4, iw_t, bsz, seq, min(256, seq), tk_att, tiled=(layer == 1))
    o_a = _flash(aq, ak, av, bsz, seq, tq_att, tk_att, mask=mask)

    ff_t = small[:, 4:12].reshape(bsz, seq, 8).transpose(0, 2, 1)
    nb = _fox_bias(ff_t, b_fox).reshape(bsz, 4, 2, seq)
    o_b = _flash(fq, fk, fv, bsz, seq, tq_att, tk_att, nb=nb)

    lbp = jnp.stack([jnp.log(lb), jnp.log1p(-lb), 1.0 - lb]).astype(F32)
    o_c = _hgrn(zc, lbp, ng.reshape(1, -1).astype(F32), bsz, seq, min(HGRN_TILE, seq))

    wr_f = jnp.zeros((d, LANES), F32).at[:, :N_EXPERTS].set(wr.astype(F32))
    wr_hi = wr_f.astype(BF16)
    wr_p = jnp.stack([wr_hi, (wr_f - wr_hi.astype(F32)).astype(BF16)])
    br_p = jnp.full((1, LANES), NEG, F32).at[0, :N_EXPERTS].set(br.astype(F32))
    x1, x1b, topi, gates = _merge(
        o_a, o_b, o_c, gt, x2, wpa.astype(BF16), wpb.astype(BF16), wpc.astype(BF16),
        wout.astype(BF16), g1.reshape(1, d), b1.reshape(1, d), wr_p, br_p, alpha, min(MERGE_TM, n))

    slot_tok, blk_e, n_used, slot_of = _route(topi[:, :TOP_K], n)
    xs = x1b[slot_tok]
    y_slots = _moe_experts(layer, blk_e, n_used, xs, wgu, bgu, wd, bd)
    yg = y_slots[slot_of]
    return _combine(yg, gates, x1, g2.reshape(1, d), b2.reshape(1, d), alpha, min(COMBINE_TM, n))


def kernel(x, w_in, b_fox_f, hgrn_lb_logits, hgrn_norm_g, w_branch_a, w_branch_b, w_branch_c, w_out, ln1_g, ln1_b, w_router, b_router, w_gu, b_gu, w_down, b_down, ln2_g, ln2_b):
    bsz, seq, d = x.shape
    depth = w_in.shape[0]
    alpha = (2 * depth) ** 0.25
    p = jax.nn.softmax(hgrn_lb_logits.astype(F32), axis=0)
    lbs = jnp.cumsum(p, axis=0)
    lbs = lbs - lbs[0]
    cos, sin = _rope_tables(seq, 512)
    x2 = x.reshape(bsz * seq, d)
    n_e = depth * w_gu.shape[1]
    wgu = w_gu.reshape(n_e, d, w_gu.shape[3])
    bgu = b_gu.reshape(n_e, 1, b_gu.shape[2])
    wd = w_down.reshape(n_e, w_down.shape[2], d)
    bd = b_down.reshape(n_e, 1, d)
    for l in range(depth):
        x2 = _layer(l, x2, bsz, seq, alpha, w_in[l], b_fox_f[l], lbs[l], hgrn_norm_g[l],
                    w_branch_a[l], w_branch_b[l], w_branch_c[l], w_out[l], ln1_g[l], ln1_b[l],
                    w_router[l], b_router[l], wgu, bgu, wd, bd, ln2_g[l], ln2_b[l], cos, sin)
    return x2.reshape(bsz, seq, d)
```

```python
import functools
import math

import jax
import jax.numpy as jnp
import numpy as np
from jax import lax
from jax.experimental import pallas as pl
from jax.experimental.pallas import tpu as pltpu

F32 = jnp.float32
BF16 = jnp.bfloat16
I32 = jnp.int32

CHUNK = 64
HEAD_DIM = 64
ROPE_THETA = 10000.0
DSA_TOPK_MAX = 256
IDX_HEADS = 4
N_EXPERTS = 32
TOP_K = 4
SWIGLU_LIMIT = 7.0
SWIGLU_ALPHA = 1.702
MOE_BLOCK = 512
LN_EPS = 1e-5
RMS_EPS = 1e-6

LOG2E = 1.4426950408889634
NEG = -1e30
INT_MIN = -2147483648
INT_MAX = 2147483647
LANES = 128
DSA_PROBES_PER_CHECK = 3
FLASH_TQ = 2048
FLASH_TK = 512
PROJ_TM = 512
PROJ_QKV_TM = 1024
MERGE_TM = 512
COMBINE_TM = 1024
HGRN_TILE = 128
HGRN_SUB = 16
VMEM_LIMIT = 56 * 1024 * 1024

IN_SPLITS = (512, 512, 512, 256, 64, 4, 512, 512, 512, 8, 512, 512, 512, 512, 3072)
_OFF = tuple(int(v) for v in np.cumsum((0,) + IN_SPLITS))


def _params(*sem):
    return pltpu.CompilerParams(dimension_semantics=sem, vmem_limit_bytes=VMEM_LIMIT)


def _dot_nt(a, b):
    return lax.dot_general(a, b, (((1,), (1,)), ((), ())), preferred_element_type=F32)


def _dot_tn(a, b):
    return lax.dot_general(a, b, (((0,), (0,)), ((), ())), preferred_element_type=F32)


def _sigmoid(t):
    return 1.0 / (1.0 + jnp.exp(-t))


def _rope_tile(t, cos, sin):
    w = t.shape[1]
    lane_d = lax.broadcasted_iota(I32, t.shape, 1) & (HEAD_DIM - 1)
    partner = jnp.where(lane_d < HEAD_DIM // 2,
                        pltpu.roll(t, w - HEAD_DIM // 2, axis=1),
                        pltpu.roll(t, HEAD_DIM // 2, axis=1))
    return t * cos + partner * sin


def _proj_dsa_kernel(x_ref, w_ref, cos_ref, sin_ref, q_ref, k_ref, v_ref):
    z = jnp.dot(x_ref[...].astype(BF16), w_ref[...], preferred_element_type=F32)
    cos = cos_ref[...]
    sin = sin_ref[...]
    q_ref[...] = (_rope_tile(z[:, 0:512], cos, sin) * (HEAD_DIM ** -0.5 * LOG2E)).astype(BF16)
    k_ref[...] = _rope_tile(z[:, 512:1024], cos, sin).astype(BF16)
    v_ref[...] = z[:, 1024:1536].astype(BF16)


def _proj_idx_kernel(x_ref, w_ref, cos_ref, sin_ref, iq_ref, ik_ref, sm_ref):
    z = jnp.dot(x_ref[...].astype(BF16), w_ref[...], preferred_element_type=F32)
    cos = cos_ref[...][:, 0:256]
    sin = sin_ref[...][:, 0:256]
    iq_ref[...] = (_rope_tile(z[:, 0:256], cos, sin) * (HEAD_DIM ** -0.5)).astype(BF16)
    ik_ref[...] = _rope_tile(z[:, 256:512], cos, sin).astype(BF16)
    sm_ref[...] = z[:, 512:640].T


def _proj_fox_kernel(x_ref, w_ref, q_ref, k_ref, v_ref):
    z = jnp.dot(x_ref[...].astype(BF16), w_ref[...], preferred_element_type=F32)
    q_ref[...] = (z[:, 0:512] * (HEAD_DIM ** -0.5 * LOG2E)).astype(BF16)
    k_ref[...] = z[:, 512:1024].astype(BF16)
    v_ref[...] = z[:, 1024:1536].astype(BF16)


def _proj_plain_kernel(x_ref, w_ref, o_ref):
    o_ref[...] = jnp.dot(x_ref[...].astype(BF16), w_ref[...], preferred_element_type=F32)


def _proj_call(kernel_fn, x2, w, extra, outs, tm):
    n, d = x2.shape
    nout = w.shape[1]
    in_specs = [pl.BlockSpec((tm, d), lambda i: (i, 0)),
                pl.BlockSpec((d, nout), lambda i: (0, 0))]
    args = [x2, w]
    for arr, spec in extra:
        in_specs.append(spec)
        args.append(arr)
    out_shape = [jax.ShapeDtypeStruct((wdt, n) if t else (n, wdt), dt) for wdt, dt, *t in outs]
    out_specs = [pl.BlockSpec((wdt, tm), lambda i: (0, i)) if t else pl.BlockSpec((tm, wdt), lambda i: (i, 0))
                 for wdt, _, *t in outs]
    return pl.pallas_call(
        kernel_fn, grid=(n // tm,), in_specs=in_specs, out_specs=out_specs,
        out_shape=out_shape, compiler_params=_params("parallel"))(*args)


def _fox_bias_kernel(ff_ref, b_ref, nb_ref):
    s = ff_ref.shape[1]
    z = ff_ref[4:12, :] + b_ref[...]
    lf = jnp.minimum(z, 0.0) - jnp.log1p(jnp.exp(-jnp.abs(z)))
    r = lax.broadcasted_iota(I32, (LANES, LANES), 0)
    c = lax.broadcasted_iota(I32, (LANES, LANES), 1)
    upper = jnp.where(r <= c, 1.0, 0.0).astype(F32)
    carry = jnp.zeros((lf.shape[0], 1), F32)
    for t in range(s // LANES):
        blk = lf[:, t * LANES:(t + 1) * LANES]
        cs = jnp.dot(blk, upper, preferred_element_type=F32,
                     precision=lax.Precision.HIGHEST) + carry
        nb_ref[0, :, t * LANES:(t + 1) * LANES] = cs * (-LOG2E)
        carry = cs[:, LANES - 1:LANES]


def _fox_bias(small_t, b_fox, bsz, s):
    h = b_fox.shape[0]
    return pl.pallas_call(
        _fox_bias_kernel, grid=(bsz,),
        in_specs=[pl.BlockSpec((16, s), lambda b: (0, b)),
                  pl.BlockSpec((h, 1), lambda b: (0, 0))],
        out_specs=pl.BlockSpec((1, h, s), lambda b: (b, 0, 0)),
        out_shape=jax.ShapeDtypeStruct((bsz, h, s), F32),
        compiler_params=_params("parallel"))(small_t, b_fox.reshape(h, 1).astype(F32))


def _dsa_index_kernel(iq_ref, ik_ref, iwt_ref, bias_ref, keys_scr, *, tq, tk, seq, ksel, cnt_tiles, tiled):
    i = pl.program_id(1)
    n_kt = ((i + 1) * tq + tk - 1) // tk
    q_pos = i * tq + lax.broadcasted_iota(I32, (1, tq), 1)
    adm_end = (q_pos // CHUNK + 1) * CHUNK

    iq = iq_ref[...]
    lane = lax.broadcasted_iota(I32, iq.shape, 1)
    iqm = [jnp.where((lane >= HEAD_DIM * h) & (lane < HEAD_DIM * (h + 1)), iq, jnp.zeros_like(iq))
           for h in range(IDX_HEADS)]
    wts = [iwt_ref[h:h + 1, :] * (IDX_HEADS ** -0.5) for h in range(IDX_HEADS)]

    def key_idx(k0):
        return k0 + lax.broadcasted_iota(I32, (tk, 1), 0)

    def to_key(v):
        bits = pltpu.bitcast(v, I32)
        sign = bits >> 31
        return (bits ^ (sign & 0x7FFFFFFF)) - sign

    def from_key(k):
        sign = k >> 31
        return pltpu.bitcast((k + sign) ^ (sign & 0x7FFFFFFF), F32)

    def score_tile(kt, carry):
        kmax, kmin = carry
        k0 = pl.multiple_of(kt * tk, tk)
        ikt = ik_ref[pl.ds(k0, tk), :]
        sc = jnp.zeros((tk, tq), F32)
        for h in range(IDX_HEADS):
            sc = sc + jnp.maximum(_dot_nt(ikt, iqm[h]), 0.0) * wts[h]
        key = to_key(sc)
        adm = key_idx(k0) < adm_end
        k_lo = jnp.where(adm, key, INT_MIN)
        k_hi = jnp.where(adm, key, INT_MAX)
        keys_scr[pl.ds(k0, tk), :] = k_lo
        for g in range(tk // 8):
            kmax = jnp.maximum(kmax, k_lo[g * 8:(g + 1) * 8, :])
            kmin = jnp.minimum(kmin, k_hi[g * 8:(g + 1) * 8, :])
        return kmax, kmin

    kmax, kmin = lax.fori_loop(0, n_kt, score_tile,
                               (jnp.full((8, tq), INT_MIN, I32), jnp.full((8, tq), INT_MAX, I32)))
    kmax = jnp.max(kmax, axis=0, keepdims=True)
    kmin = jnp.min(kmin, axis=0, keepdims=True)

    kf = float(ksel)
    n_acc = 4
    grp = 8 * n_acc

    tc = cnt_tiles * tk
    n_ct = (n_kt + cnt_tiles - 1) // cnt_tiles

    def pad_tile(kt, carry):
        keys_scr[pl.ds(pl.multiple_of(kt * tk, tk), tk), :] = jnp.full((tk, tq), INT_MIN, I32)
        return carry

    lax.fori_loop(n_kt, n_ct * cnt_tiles, pad_tile, 0)

    def count(pred):
        def body(kt, accs):
            k0 = pl.multiple_of(kt * tc, tc)
            accs = list(accs)
            for g in range(tc // grp):
                kk = keys_scr[pl.ds(k0 + g * grp, grp), :]
                hit = jnp.where(pred(kk, k0 + g * grp), 1.0, 0.0)
                for a in range(n_acc):
                    accs[a] = accs[a] + hit[a * 8:(a + 1) * 8, :]
            return tuple(accs)
        accs = lax.fori_loop(0, n_ct, body, tuple(jnp.zeros((8, tq), F32) for _ in range(n_acc)))
        tot = accs[0]
        for a in range(1, n_acc):
            tot = tot + accs[a]
        return jnp.sum(tot, axis=0, keepdims=True)

    def count_ge(cand):
        return count(lambda kk, k0: kk >= cand)

    n_adm = adm_end.astype(F32)
    zero = jnp.zeros((1, tq), F32)
    one = jnp.ones((1, tq), F32)
    log_k = math.log(kf)
    done0 = jnp.where(n_adm <= kf, 1.0, 0.0)
    init = (jnp.int32(0), kmin, kmax + 1, n_adm, zero, jnp.full((1, tq), INT_MIN, I32), done0, zero,
            one, one, zero)

    def cond(st):
        return (st[0] < 264) & (jnp.min(st[6]) == 0.0)

    def step(st):
        it, lo, hi, c_lo, c_hi, thr, done, tie, w_lo, w_hi, last = st
        lo_v = from_key(lo)
        hi_v = from_key(hi)
        f_lo = (jnp.log(c_lo + 0.5) - log_k) * w_lo
        f_hi = (log_k - jnp.log(c_hi + 0.5)) * w_hi
        cand = to_key(lo_v + (hi_v - lo_v) * (f_lo / (f_lo + f_hi)))
        cand = jnp.where(((it - 3) & 7) == 7, (lo >> 1) + (hi >> 1) + (lo & hi & 1), cand)
        cand = jnp.where(it == 0, kmax, cand)
        cand = jnp.where(it == 1, 0, cand)
        cand = jnp.where(it == 2, 1, cand)
        cand = jnp.minimum(jnp.maximum(cand, lo + 1), hi - 1)
        c = count_ge(cand)
        active = done == 0.0
        up = active & (c >= kf)
        down = active & (c < kf)
        lo = jnp.where(up, cand, lo)
        c_lo = jnp.where(up, c, c_lo)
        hi = jnp.where(down, cand, hi)
        c_hi = jnp.where(down, c, c_hi)
        w_hi = jnp.where(up & (last > 0.0), 0.5 * w_hi, jnp.where(down, 1.0, w_hi))
        w_lo = jnp.where(down & (last < 0.0), 0.5 * w_lo, jnp.where(up, 1.0, w_lo))
        last = jnp.where(up, 1.0, jnp.where(down, -1.0, last))
        hit = active & (c == kf)
        conv = active & jnp.logical_not(hit) & (hi - 1 <= lo)
        thr = jnp.where(hit, cand - 1, jnp.where(conv, lo, thr))
        tie = jnp.where(conv, 1.0, tie)
        done = jnp.where(hit | conv, 1.0, done)
        return it + 1, lo, hi, c_lo, c_hi, thr, done, tie, w_lo, w_hi, last

    def steps(st):
        for _ in range(DSA_PROBES_PER_CHECK):
            st = step(st)
        return st

    st = lax.while_loop(cond, steps, init)
    c_hi, thr, tie = st[4], st[5], st[7]

    def store_tile(kt, k0, val):
        if tiled:
            bias_ref[0, kt] = val
        else:
            bias_ref[0, :, pl.ds(k0, tk)] = val
    any_tie = jnp.max(tie) > 0.0

    @pl.when(jnp.logical_not(any_tie))
    def _():
        def write_tile(kt, carry):
            k0 = pl.multiple_of(kt * tk, tk)
            sel = keys_scr[pl.ds(k0, tk), :] > thr
            store_tile(kt, k0, jnp.where(sel, 0.0, NEG).T.astype(BF16))
            return carry

        lax.fori_loop(0, n_kt, write_tile, 0)

    @pl.when(any_tie)
    def _():
        need = jnp.where(tie > 0.0, kf - c_hi, 0.0)
        lower = jnp.where(lax.broadcasted_iota(I32, (tk, tk), 0) >= lax.broadcasted_iota(I32, (tk, tk), 1),
                          1.0, 0.0).astype(BF16)

        def write_tile(kt, seen):
            k0 = pl.multiple_of(kt * tk, tk)
            kk = keys_scr[pl.ds(k0, tk), :]
            eq = kk == thr
            rank = jnp.dot(lower, jnp.where(eq, 1.0, 0.0).astype(BF16), preferred_element_type=F32) + seen
            sel = (kk > thr) | (eq & (rank <= need))
            store_tile(kt, k0, jnp.where(sel, 0.0, NEG).T.astype(BF16))
            return rank[tk - 1:tk, :]

        lax.fori_loop(0, n_kt, write_tile, jnp.zeros((1, tq), F32))

    def fill_tile(kt, carry):
        k0 = pl.multiple_of(kt * tk, tk)
        store_tile(kt, k0, jnp.full((tq, tk), NEG, BF16))
        return carry

    lax.fori_loop(n_kt, seq // tk, fill_tile, 0)


def _dsa_index(iq, ik4, iw_t, bsz, seq, tq, tk, cnt_tiles=1, tiled=False):
    ksel = min(DSA_TOPK_MAX, seq // 4)
    nq = seq // tq
    kern = functools.partial(_dsa_index_kernel, tq=tq, tk=tk, seq=seq, ksel=ksel, cnt_tiles=cnt_tiles,
                             tiled=tiled)
    return pl.pallas_call(
        kern, grid=(bsz, nq),
        in_specs=[pl.BlockSpec((tq, 256), lambda b, i: (b * nq + i, 0)),
                  pl.BlockSpec((seq, 256), lambda b, i: (b, 0)),
                  pl.BlockSpec((8, tq), lambda b, i: (0, b * nq + i))],
        out_specs=(pl.BlockSpec((1, seq // tk, tq, tk), lambda b, i: (b, 0, i, 0)) if tiled
                   else pl.BlockSpec((1, tq, seq), lambda b, i: (b, i, 0))),
        out_shape=jax.ShapeDtypeStruct((bsz, seq // tk, seq, tk) if tiled else (bsz, seq, seq), BF16),
        scratch_shapes=[pltpu.VMEM((seq, tq), I32)],
        compiler_params=_params("parallel", "parallel"))(iq, ik4, iw_t)


def _flash_kernel(*refs, tq, tk, use_mask, tiled):
    if use_mask:
        q_ref, k_ref, v_ref, mask_ref, o_ref, m_scr, acc_scr, mbuf, sem = refs
        nb_ref = None
    else:
        q_ref, k_ref, v_ref, nb_ref, o_ref, m_scr, acc_scr = refs
        mask_ref = None
    b = pl.program_id(0)
    i = pl.program_id(2)
    q = q_ref[...]
    head0 = lax.broadcasted_iota(I32, (tq, LANES), 1) < HEAD_DIM
    head0k = lax.broadcasted_iota(I32, (tk, LANES), 1) < HEAD_DIM
    qm = [jnp.where(head0, q, jnp.zeros_like(q)), jnp.where(head0, jnp.zeros_like(q), q)]
    m_scr[...] = jnp.full(m_scr.shape, NEG, F32)
    acc_scr[...] = jnp.zeros(acc_scr.shape, F32)
    n_col = tk // LANES
    n_sub = tq // tk
    n_off = i * n_sub

    def mask_dma(kt, slot, r0):
        return pltpu.make_async_copy(
            (mask_ref.at[b, kt, pl.ds(i * tq + r0, tq - r0), :] if tiled else
             mask_ref.at[b, pl.ds(i * tq + r0, tq - r0), pl.ds(pl.multiple_of(kt * tk, tk), tk)]),
            mbuf.at[slot, pl.ds(r0, tq - r0), :], sem.at[slot])

    def tile(kt, diag):
        r0 = 0 if diag is None else diag * tk
        nr = tq - r0
        k0 = pl.multiple_of(kt * tk, tk)
        kt_ = k_ref[pl.ds(k0, tk), :]
        vt = v_ref[pl.ds(k0, tk), :]
        one = jnp.ones_like(vt)
        vx = [jnp.where(head0k, vt, one), jnp.where(head0k, one, vt)]
        if use_mask:
            slot = kt & 1
            if diag is None:
                mask_dma(kt + 1, 1 - slot, 0).start()
            elif diag + 1 < n_sub:
                mask_dma(kt + 1, 1 - slot, (diag + 1) * tk).start()
            mask_dma(kt, slot, r0).wait()
            extra = mbuf[slot, pl.ds(r0, nr), :].astype(F32)
        ss = []
        for j in range(2):
            s = _dot_nt(qm[j][r0:, :], kt_)
            if use_mask:
                s = s + extra
            else:
                s = s + nb_ref[0, 0, j:j + 1, pl.ds(k0, tk)]
                if diag is not None:
                    rr = lax.broadcasted_iota(I32, (nr, tk), 0)
                    cc = lax.broadcasted_iota(I32, (nr, tk), 1)
                    s = jnp.where(cc <= rr, s, NEG)
            ss.append(s)
        ps = []
        for j in range(2):
            sc = [ss[j][:, c * LANES:(c + 1) * LANES] for c in range(n_col)]
            part = sc[0]
            for c in range(1, n_col):
                part = jnp.maximum(part, sc[c])
            m_old = m_scr[j, r0:, :]
            m_new = jnp.maximum(m_old, jnp.max(part, axis=1, keepdims=True))
            alpha = jnp.exp2(m_old - m_new)
            ps.append((alpha, jnp.concatenate([jnp.exp2(c_ - m_new) for c_ in sc], axis=1).astype(BF16)))
            m_scr[j, r0:, :] = m_new
        for j in range(2):
            alpha, p = ps[j]
            acc_scr[j, r0:, :] = alpha * acc_scr[j, r0:, :] + jnp.dot(p, vx[j], preferred_element_type=F32)

    def body(kt, carry):
        tile(kt, None)
        return carry

    if use_mask:
        mask_dma(0, 0, 0).start()
    lax.fori_loop(0, n_off, body, 0)
    for d in range(n_sub):
        tile(n_off + d, d)

    a0 = acc_scr[0]
    a1 = acc_scr[1]
    o0 = a0 / pltpu.roll(a0, HEAD_DIM, axis=1)
    o1 = a1 / pltpu.roll(a1, HEAD_DIM, axis=1)
    o_ref[...] = jnp.where(head0, o0, o1).astype(o_ref.dtype)


def _flash(q, k, v, bsz, seq, tq, tk, nb=None, mask=None):
    nq = seq // tq
    n_hp = q.shape[1] // LANES
    use_mask = mask is not None
    in_specs = [pl.BlockSpec((tq, LANES), lambda b, h, i: (b * nq + i, h)),
                pl.BlockSpec((seq, LANES), lambda b, h, i: (b, h)),
                pl.BlockSpec((seq, LANES), lambda b, h, i: (b, h))]
    scratch = [pltpu.VMEM((2, tq, LANES), F32), pltpu.VMEM((2, tq, LANES), F32)]
    if use_mask:
        in_specs.append(pl.BlockSpec(memory_space=pl.ANY))
        scratch += [pltpu.VMEM((2, tq, tk), BF16), pltpu.SemaphoreType.DMA((2,))]
        extra = mask
    else:
        in_specs.append(pl.BlockSpec((1, 1, 2, seq), lambda b, h, i: (b, h, 0, 0)))
        extra = nb
    tiled = use_mask and mask.ndim == 4
    assert not tiled or mask.shape[3] == tk
    kern = functools.partial(_flash_kernel, tq=tq, tk=tk, use_mask=use_mask, tiled=tiled)
    return pl.pallas_call(
        kern, grid=(bsz, n_hp, nq), in_specs=in_specs,
        out_specs=pl.BlockSpec((tq, LANES), lambda b, h, i: (b * nq + i, h)),
        out_shape=jax.ShapeDtypeStruct(q.shape, BF16),
        scratch_shapes=scratch,
        compiler_params=_params("parallel", "parallel", "parallel"))(q, k, v, extra)


def _hgrn_kernel(q_ref, zf_ref, v_ref, g_ref, lb_ref, ng_ref, o_ref, st_scr, oi_scr, *, tt):
    r = HGRN_SUB
    half = 256

    @pl.when(pl.program_id(1) == 0)
    def _():
        st_scr[...] = jnp.zeros(st_scr.shape, F32)

    q = q_ref[...]
    zf = zf_ref[...]
    v = v_ref[...]
    ls = jnp.minimum(zf, 0.0) - jnp.log1p(jnp.exp(-jnp.abs(zf)))
    a = lb_ref[0:1, :]
    y = lb_ref[1:2, :] + ls
    lf = jnp.maximum(a, y) + jnp.log1p(jnp.exp(-jnp.abs(a - y)))
    kk = lb_ref[2:3, :] * (1.0 / (1.0 + jnp.exp(zf)))

    rin = lax.broadcasted_iota(I32, (tt, 1), 0) & (r - 1)
    b = lf
    sh = 1
    while sh < r:
        b = b + jnp.where(rin >= sh, pltpu.roll(b, sh, axis=0), 0.0)
        sh *= 2

    rr = lax.broadcasted_iota(I32, (half, half), 0) // HEAD_DIM
    cc = lax.broadcasted_iota(I32, (half, half), 1) // HEAD_DIM
    same_head = rr == cc
    ones_bd = jnp.where(same_head, 1.0, 0.0).astype(BF16)
    bd_mask = jnp.where(same_head, 1.0, 0.0).astype(F32)

    o = jnp.zeros((tt, 2 * half), F32)
    for off in range(r):
        if off == 0:
            e = q * kk
            vs = v
        else:
            e = q * pltpu.roll(kk, off, axis=0) * jnp.exp(b - pltpu.roll(b, off, axis=0))
            e = jnp.where(rin >= off, e, 0.0)
            vs = pltpu.roll(v, off, axis=0)
        eb = e.astype(BF16)
        sc = jnp.concatenate(
            [jnp.dot(eb[:, 0:half], ones_bd, preferred_element_type=F32),
             jnp.dot(eb[:, half:], ones_bd, preferred_element_type=F32)], axis=1)
        o = o + sc * vs

    qe = (q * jnp.exp(b)).astype(BF16)
    vb = v.astype(BF16)
    for c in range(tt // r):
        r0 = c * r
        bl = b[r0 + r - 1:r0 + r, :]
        kd = (kk[r0:r0 + r, :] * jnp.exp(bl - b[r0:r0 + r, :])).astype(BF16)
        dec = jnp.exp(bl)
        for h2 in range(2):
            lo = h2 * half
            st = st_scr[h2]
            oi_scr[r0:r0 + r, lo:lo + half] = _dot_nt(qe[r0:r0 + r, lo:lo + half], st.astype(BF16))
            upd = _dot_tn(vb[r0:r0 + r, lo:lo + half], kd[:, lo:lo + half])
            st_scr[h2] = st * dec[:, lo:lo + half] + upd * bd_mask

    o = o + oi_scr[...]
    o2 = o * o
    o2_hi = o2.astype(BF16)
    o2_lo = (o2 - o2_hi.astype(F32)).astype(BF16)
    ms = jnp.concatenate(
        [jnp.dot(o2_hi[:, 0:half], ones_bd, preferred_element_type=F32)
         + jnp.dot(o2_lo[:, 0:half], ones_bd, preferred_element_type=F32),
         jnp.dot(o2_hi[:, half:], ones_bd, preferred_element_type=F32)
         + jnp.dot(o2_lo[:, half:], ones_bd, preferred_element_type=F32)],
        axis=1) * (1.0 / HEAD_DIM)
    on = o * lax.rsqrt(ms + RMS_EPS) * ng_ref[...] * _sigmoid(g_ref[...])
    o_ref[...] = on.astype(o_ref.dtype)


def _hgrn(zc, lbp, ng, bsz, seq, tt):
    n = zc.shape[0]
    nt = seq // tt
    col = lambda cidx: pl.BlockSpec((tt, 512), lambda b, j: (b * nt + j, cidx))
    kern = functools.partial(_hgrn_kernel, tt=tt)
    return pl.pallas_call(
        kern, grid=(bsz, nt),
        in_specs=[col(0), col(1), col(2), col(3),
                  pl.BlockSpec((3, 512), lambda b, j: (0, 0)),
                  pl.BlockSpec((1, 512), lambda b, j: (0, 0))],
        out_specs=pl.BlockSpec((tt, 512), lambda b, j: (b * nt + j, 0)),
        out_shape=jax.ShapeDtypeStruct((n, 512), BF16),
        scratch_shapes=[pltpu.VMEM((2, 256, 256), F32), pltpu.VMEM((tt, 512), F32)],
        compiler_params=_params("parallel", "arbitrary"))(zc, zc, zc, zc, lbp, ng)


def _layer_norm(y, g, b):
    mu = jnp.mean(y, axis=-1, keepdims=True)
    d = y - mu
    var = jnp.mean(d * d, axis=-1, keepdims=True)
    return d * lax.rsqrt(var + LN_EPS) * g + b


def _merge_kernel(oa_ref, ob_ref, oc_ref, gt_ref, x_ref, wpa_ref, wpb_ref, wpc_ref, wout_ref,
                  g_ref, b_ref, wr_ref, br_ref, x1_ref, xb_ref, ti_ref, gate_ref, *, alpha):
    d = x_ref.shape[1]
    pa = jnp.dot(oa_ref[...], wpa_ref[...], preferred_element_type=F32)
    pb = jnp.dot(ob_ref[...], wpb_ref[...], preferred_element_type=F32)
    pc = jnp.dot(oc_ref[...], wpc_ref[...], preferred_element_type=F32)
    merged = (_sigmoid(gt_ref[:, 0:d]) * pa + _sigmoid(gt_ref[:, d:2 * d]) * pb
              + _sigmoid(gt_ref[:, 2 * d:3 * d]) * pc)
    mix = jnp.dot(merged.astype(BF16), wout_ref[...], preferred_element_type=F32)
    x1 = _layer_norm(alpha * x_ref[...] + mix, g_ref[...], b_ref[...])
    x1_ref[...] = x1
    xb_ref[...] = x1.astype(BF16)

    x_hi = x1.astype(BF16)
    x_lo = (x1 - x_hi.astype(F32)).astype(BF16)
    logits = (jnp.dot(x_hi, wr_ref[0], preferred_element_type=F32)
              + jnp.dot(x_hi, wr_ref[1], preferred_element_type=F32)
              + jnp.dot(x_lo, wr_ref[0], preferred_element_type=F32)) + br_ref[...]
    lane = lax.broadcasted_iota(I32, logits.shape, 1)
    topi = jnp.zeros(logits.shape, I32)
    topv = jnp.full(logits.shape, NEG, F32)
    for k in range(TOP_K):
        m = jnp.max(logits, axis=1, keepdims=True)
        idx = jnp.min(jnp.where(logits == m, lane, LANES), axis=1, keepdims=True)
        topi = jnp.where(lane == k, idx, topi)
        topv = jnp.where(lane == k, m, topv)
        logits = jnp.where(lane == idx, -jnp.inf, logits)
    e = jnp.where(lane < TOP_K, jnp.exp(topv - jnp.max(topv, axis=1, keepdims=True)), 0.0)
    ti_ref[...] = topi
    gate_ref[...] = e / jnp.sum(e, axis=1, keepdims=True)


def _merge(oa, ob, oc, gt, x2, wpa, wpb, wpc, wout, g, b, wr, br, alpha, tm):
    n, d = x2.shape
    row = lambda w: pl.BlockSpec((tm, w), lambda i: (i, 0))
    full = lambda a: pl.BlockSpec(a.shape, lambda i: (0,) * a.ndim)
    kern = functools.partial(_merge_kernel, alpha=alpha)
    return pl.pallas_call(
        kern, grid=(n // tm,),
        in_specs=[row(512), row(512), row(512), row(3 * d), row(d),
                  full(wpa), full(wpb), full(wpc), full(wout), full(g), full(b), full(wr), full(br)],
        out_specs=[row(d), row(d), row(LANES), row(LANES)],
        out_shape=[jax.ShapeDtypeStruct((n, d), F32), jax.ShapeDtypeStruct((n, d), BF16),
                   jax.ShapeDtypeStruct((n, LANES), I32), jax.ShapeDtypeStruct((n, LANES), F32)],
        compiler_params=_params("parallel"))(oa, ob, oc, gt, x2, wpa, wpb, wpc, wout, g, b, wr, br)


def _moe_kernel(be_ref, nu_ref, xs_ref, wgu_ref, bgu_ref, wd_ref, bd_ref, y_ref, wgu_bf, wd_bf):
    i = pl.program_id(0)
    de = wd_ref.shape[1]

    @pl.when((i == 0) | (be_ref[i] != be_ref[jnp.maximum(i - 1, 0)]))
    def _():
        wgu_bf[...] = wgu_ref[0].astype(BF16)
        wd_bf[...] = wd_ref[0].astype(BF16)

    @pl.when(i < nu_ref[0])
    def _():
        gu = jnp.dot(xs_ref[...].astype(BF16), wgu_bf[...], preferred_element_type=F32) + bgu_ref[0]
        gate = jnp.minimum(gu[:, 0:de], SWIGLU_LIMIT)
        up = jnp.clip(gu[:, de:], -SWIGLU_LIMIT, SWIGLU_LIMIT)
        act = gate * _sigmoid(SWIGLU_ALPHA * gate) * (up + 1.0)
        y = jnp.dot(act.astype(BF16), wd_bf[...], preferred_element_type=F32) + bd_ref[0]
        y_ref[...] = y.astype(y_ref.dtype)

    @pl.when(i >= nu_ref[0])
    def _():
        y_ref[...] = jnp.zeros(y_ref.shape, y_ref.dtype)


def _moe_experts(layer, blk_e, n_used, xs, wgu, bgu, wd, bd):
    n_slots, d = xs.shape
    n_blocks = n_slots // MOE_BLOCK
    _, _, n2 = wgu.shape
    de = wd.shape[1]
    base = layer * N_EXPERTS
    grid_spec = pltpu.PrefetchScalarGridSpec(
        num_scalar_prefetch=2, grid=(n_blocks,),
        in_specs=[pl.BlockSpec((MOE_BLOCK, d), lambda i, be, nu: (i, 0)),
                  pl.BlockSpec((1, d, n2), lambda i, be, nu: (base + be[i], 0, 0)),
                  pl.BlockSpec((1, 1, n2), lambda i, be, nu: (base + be[i], 0, 0)),
                  pl.BlockSpec((1, de, d), lambda i, be, nu: (base + be[i], 0, 0)),
                  pl.BlockSpec((1, 1, d), lambda i, be, nu: (base + be[i], 0, 0))],
        out_specs=pl.BlockSpec((MOE_BLOCK, d), lambda i, be, nu: (i, 0)),
        scratch_shapes=[pltpu.VMEM((d, n2), BF16), pltpu.VMEM((de, d), BF16)])
    return pl.pallas_call(
        _moe_kernel, grid_spec=grid_spec,
        out_shape=jax.ShapeDtypeStruct((n_slots, d), BF16),
        compiler_params=_params("arbitrary"))(blk_e, n_used, xs, wgu, bgu, wd, bd)


def _combine_kernel(yg_ref, gate_ref, x1_ref, g_ref, b_ref, o_ref, *, alpha):
    gates = gate_ref[...]
    acc = alpha * x1_ref[...]
    for k in range(TOP_K):
        acc = acc + gates[:, k:k + 1] * yg_ref[k].astype(F32)
    o_ref[...] = _layer_norm(acc, g_ref[...], b_ref[...])


def _combine(yg, gates, x1, g, b, alpha, tm):
    n, d = x1.shape
    kern = functools.partial(_combine_kernel, alpha=alpha)
    return pl.pallas_call(
        kern, grid=(n // tm,),
        in_specs=[pl.BlockSpec((TOP_K, tm, d), lambda i: (0, i, 0)),
                  pl.BlockSpec((tm, LANES), lambda i: (i, 0)),
                  pl.BlockSpec((tm, d), lambda i: (i, 0)),
                  pl.BlockSpec((1, d), lambda i: (0, 0)),
                  pl.BlockSpec((1, d), lambda i: (0, 0))],
        out_specs=pl.BlockSpec((tm, d), lambda i: (i, 0)),
        out_shape=jax.ShapeDtypeStruct((n, d), F32),
        compiler_params=_params("parallel"))(yg, gates, x1, g, b)


def _rope_tables(seq, width):
    half = HEAD_DIM // 2
    inv = ROPE_THETA ** (-jnp.arange(half, dtype=F32) / half)
    ang = jnp.arange(seq, dtype=F32)[:, None] * inv[None, :]
    cos = jnp.concatenate([jnp.cos(ang), jnp.cos(ang)], axis=1)
    sin = jnp.concatenate([-jnp.sin(ang), jnp.sin(ang)], axis=1)
    reps = width // HEAD_DIM
    return jnp.tile(cos, (1, reps)), jnp.tile(sin, (1, reps))


def _split_w_in(w):
    seg = lambda a, b: w[:, _OFF[a]:_OFF[b]]
    w_dsa = seg(0, 3)
    small = jnp.concatenate([seg(5, 6), seg(9, 10), jnp.zeros((w.shape[0], LANES - 12), w.dtype)], axis=1)
    w_idx = jnp.concatenate([seg(3, 4)] + [seg(4, 5)] * IDX_HEADS + [small], axis=1)
    w_fox = seg(6, 9)
    w_hgrn = seg(10, 14)
    w_gate = seg(14, 15)
    return [t.astype(BF16) for t in (w_dsa, w_idx, w_fox, w_hgrn, w_gate)]


def _route(top_idx, n_tok):
    n_assign = n_tok * TOP_K
    flat_e = top_idx.reshape(-1)
    experts = jnp.arange(N_EXPERTS, dtype=I32)
    onehot = (flat_e[None, :] == experts[:, None]).astype(F32)
    oh3 = onehot.reshape(N_EXPERTS, n_assign // LANES, LANES)
    upper = (jnp.arange(LANES)[:, None] <= jnp.arange(LANES)[None, :]).astype(F32)
    within = jnp.einsum('ebl,lm->ebm', oh3, upper)
    btot = within[:, :, LANES - 1]
    csum = (within + (jnp.cumsum(btot, axis=1) - btot)[:, :, None]).reshape(N_EXPERTS, n_assign)
    counts = jnp.sum(btot, axis=1).astype(I32)
    rank = (jnp.sum(csum * onehot, axis=0) - 1.0).astype(I32)
    padded = (counts + MOE_BLOCK - 1) // MOE_BLOCK * MOE_BLOCK
    pad_end = jnp.cumsum(padded)
    pad_start = pad_end - padded
    start = jnp.cumsum(counts) - counts
    slot_of = pad_start[flat_e] + rank
    n_blocks = -(-(n_assign + N_EXPERTS * (MOE_BLOCK - 1)) // MOE_BLOCK)
    blk_start = jnp.arange(n_blocks, dtype=I32) * MOE_BLOCK
    blk_e = jnp.sum((pad_end[None, :] <= blk_start[:, None]).astype(I32), axis=1)
    blk_e = jnp.minimum(blk_e, N_EXPERTS - 1).astype(I32)
    n_used = (pad_end[-1] // MOE_BLOCK).astype(I32).reshape(1)
    order = jnp.argsort(flat_e).astype(I32)
    slot = jnp.arange(n_blocks * MOE_BLOCK, dtype=I32)
    slot_e = jnp.repeat(blk_e, MOE_BLOCK)
    r = slot - pad_start[slot_e]
    src = jnp.clip(start[slot_e] + r, 0, n_assign - 1)
    slot_tok = jnp.where(r < counts[slot_e], order[src] // TOP_K, slot % n_tok)
    return slot_tok, blk_e, n_used, slot_of.reshape(n_tok, TOP_K).T


def _layer(layer, x2, bsz, seq, alpha, w_in, b_fox, lb, ng, wpa, wpb, wpc, wout, g1, b1,
           wr, br, wgu, bgu, wd, bd, g2, b2, cos, sin):
    n, d = x2.shape
    tm = min(PROJ_TM, seq)
    tmq = min(PROJ_QKV_TM, seq)
    w_dsa, w_idx, w_fox, w_hgrn, w_gate = _split_w_in(w_in)
    nst = seq // tmq
    tab = lambda arr: (arr, pl.BlockSpec((tmq, 512), lambda i: (i % nst, 0)))

    aq, ak, av = _proj_call(_proj_dsa_kernel, x2, w_dsa, [tab(cos), tab(sin)],
                            [(512, BF16)] * 3, tmq)
    iq, ik4, small_t = _proj_call(_proj_idx_kernel, x2, w_idx, [tab(cos), tab(sin)],
                                [(256, BF16), (256, BF16), (LANES, F32, True)], tmq)
    fq, fk, fv = _proj_call(_proj_fox_kernel, x2, w_fox, [], [(512, BF16)] * 3, tmq)
    (zc,) = _proj_call(_proj_plain_kernel, x2, w_hgrn, [], [(2048, F32)], tm)
    (gt,) = _proj_call(_proj_plain_kernel, x2, w_gate, [], [(3 * d, F32)], tm)

    tq_att, tk_att = min(FLASH_TQ, seq), min(FLASH_TK, seq)
    mask = _dsa_index(iq, ik4, small_t, bsz, seq, min(256, seq), tk_att)
    o_a = _flash(aq, ak, av, bsz, seq, tq_att, tk_att, mask=mask)

    nb = _fox_bias(small_t, b_fox, bsz, seq).reshape(bsz, 4, 2, seq)
    o_b = _flash(fq, fk, fv, bsz, seq, tq_att, tk_att, nb=nb)

    lbp = jnp.stack([jnp.log(lb), jnp.log1p(-lb), 1.0 - lb]).astype(F32)
    o_c = _hgrn(zc, lbp, ng.reshape(1, -1).astype(F32), bsz, seq, min(HGRN_TILE, seq))

    wr_f = jnp.zeros((d, LANES), F32).at[:, :N_EXPERTS].set(wr.astype(F32))
    wr_hi = wr_f.astype(BF16)
    wr_p = jnp.stack([wr_hi, (wr_f - wr_hi.astype(F32)).astype(BF16)])
    br_p = jnp.full((1, LANES), NEG, F32).at[0, :N_EXPERTS].set(br.astype(F32))
    x1, x1b, topi, gates = _merge(
        o_a, o_b, o_c, gt, x2, wpa.astype(BF16), wpb.astype(BF16), wpc.astype(BF16),
        wout.astype(BF16), g1.reshape(1, d), b1.reshape(1, d), wr_p, br_p, alpha, min(MERGE_TM, n))

    slot_tok, blk_e, n_used, slot_of = _route(topi[:, :TOP_K], n)
    xs = x1b[slot_tok]
    y_slots = _moe_experts(layer, blk_e, n_used, xs, wgu, bgu, wd, bd)
    yg = y_slots[slot_of]
    return _combine(yg, gates, x1, g2.reshape(1, d), b2.reshape(1, d), alpha, min(COMBINE_TM, n))


def kernel(x, w_in, b_fox_f, hgrn_lb_logits, hgrn_norm_g, w_branch_a, w_branch_b, w_branch_c, w_out, ln1_g, ln1_b, w_router, b_router, w_gu, b_gu, w_down, b_down, ln2_g, ln2_b):
    bsz, seq, d = x.shape
    depth = w_in.shape[0]
    alpha = (2 * depth) ** 0.25
    p = jax.nn.softmax(hgrn_lb_logits.astype(F32), axis=0)
    lbs = jnp.cumsum(p, axis=0)
    lbs = lbs - lbs[0]
    cos, sin = _rope_tables(seq, 512)
    x2 = x.reshape(bsz * seq, d)
    n_e = depth * w_gu.shape[1]
    wgu = w_gu.reshape(n_e, d, w_gu.shape[3])
    bgu = b_gu.reshape(n_e, 1, b_gu.shape[2])
    wd = w_down.reshape(n_e, w_down.shape[2], d)
    bd = b_down.reshape(n_e, 1, d)
    for l in range(depth):
        x2 = _layer(l, x2, bsz, seq, alpha, w_in[l], b_fox_f[l], lbs[l], hgrn_norm_g[l],
                    w_branch_a[l], w_branch_b[l], w_branch_c[l], w_out[l], ln1_g[l], ln1_b[l],
                    w_router[l], b_router[l], wgu, bgu, wd, bd, ln2_g[l], ln2_b[l], cos, sin)
    return x2.reshape(bsz, seq, d)
```

```python
import functools
import math

import jax
import jax.numpy as jnp
import numpy as np
from jax import lax
from jax.experimental import pallas as pl
from jax.experimental.pallas import tpu as pltpu

F32 = jnp.float32
BF16 = jnp.bfloat16
I32 = jnp.int32

CHUNK = 64
HEAD_DIM = 64
ROPE_THETA = 10000.0
DSA_TOPK_MAX = 256
IDX_HEADS = 4
N_EXPERTS = 32
TOP_K = 4
SWIGLU_LIMIT = 7.0
SWIGLU_ALPHA = 1.702
MOE_BLOCK = 512
LN_EPS = 1e-5
RMS_EPS = 1e-6

LOG2E = 1.4426950408889634
NEG = -1e30
INT_MIN = -2147483648
INT_MAX = 2147483647
LANES = 128
DSA_PROBES_PER_CHECK = 3
FLASH_TQ = 2048
FLASH_TK = 512
PROJ_TM = 512
PROJ_QKV_TM = 1024
MERGE_TM = 512
COMBINE_TM = 1024
HGRN_TILE = 128
HGRN_SUB = 16
VMEM_LIMIT = 56 * 1024 * 1024

IN_SPLITS = (512, 512, 512, 256, 64, 4, 512, 512, 512, 8, 512, 512, 512, 512, 3072)
_OFF = tuple(int(v) for v in np.cumsum((0,) + IN_SPLITS))


def _params(*sem):
    return pltpu.CompilerParams(dimension_semantics=sem, vmem_limit_bytes=VMEM_LIMIT)


def _dot_nt(a, b):
    return lax.dot_general(a, b, (((1,), (1,)), ((), ())), preferred_element_type=F32)


def _dot_tn(a, b):
    return lax.dot_general(a, b, (((0,), (0,)), ((), ())), preferred_element_type=F32)


def _sigmoid(t):
    return 1.0 / (1.0 + jnp.exp(-t))


def _rope_tile(t, cos, sin):
    w = t.shape[1]
    lane_d = lax.broadcasted_iota(I32, t.shape, 1) & (HEAD_DIM - 1)
    partner = jnp.where(lane_d < HEAD_DIM // 2,
                        pltpu.roll(t, w - HEAD_DIM // 2, axis=1),
                        pltpu.roll(t, HEAD_DIM // 2, axis=1))
    return t * cos + partner * sin


def _proj_dsa_kernel(x_ref, w_ref, cos_ref, sin_ref, q_ref, k_ref, v_ref):
    z = jnp.dot(x_ref[...].astype(BF16), w_ref[...], preferred_element_type=F32)
    cos = cos_ref[...]
    sin = sin_ref[...]
    q_ref[...] = (_rope_tile(z[:, 0:512], cos, sin) * (HEAD_DIM ** -0.5 * LOG2E)).astype(BF16)
    k_ref[...] = _rope_tile(z[:, 512:1024], cos, sin).astype(BF16)
    v_ref[...] = z[:, 1024:1536].astype(BF16)


def _proj_idx_kernel(x_ref, w_ref, cos_ref, sin_ref, iq_ref, ik_ref, sm_ref):
    z = jnp.dot(x_ref[...].astype(BF16), w_ref[...], preferred_element_type=F32)
    cos = cos_ref[...][:, 0:256]
    sin = sin_ref[...][:, 0:256]
    iq_ref[...] = (_rope_tile(z[:, 0:256], cos, sin) * (HEAD_DIM ** -0.5)).astype(BF16)
    ik_ref[...] = _rope_tile(z[:, 256:512], cos, sin).astype(BF16)
    sm_ref[...] = z[:, 512:640].T


def _proj_fox_kernel(x_ref, w_ref, q_ref, k_ref, v_ref):
    z = jnp.dot(x_ref[...].astype(BF16), w_ref[...], preferred_element_type=F32)
    q_ref[...] = (z[:, 0:512] * (HEAD_DIM ** -0.5 * LOG2E)).astype(BF16)
    k_ref[...] = z[:, 512:1024].astype(BF16)
    v_ref[...] = z[:, 1024:1536].astype(BF16)


def _proj_plain_kernel(x_ref, w_ref, o_ref):
    o_ref[...] = jnp.dot(x_ref[...].astype(BF16), w_ref[...], preferred_element_type=F32)


def _proj_call(kernel_fn, x2, w, extra, outs, tm):
    n, d = x2.shape
    nout = w.shape[1]
    in_specs = [pl.BlockSpec((tm, d), lambda i: (i, 0)),
                pl.BlockSpec((d, nout), lambda i: (0, 0))]
    args = [x2, w]
    for arr, spec in extra:
        in_specs.append(spec)
        args.append(arr)
    out_shape = [jax.ShapeDtypeStruct((wdt, n) if t else (n, wdt), dt) for wdt, dt, *t in outs]
    out_specs = [pl.BlockSpec((wdt, tm), lambda i: (0, i)) if t else pl.BlockSpec((tm, wdt), lambda i: (i, 0))
                 for wdt, _, *t in outs]
    return pl.pallas_call(
        kernel_fn, grid=(n // tm,), in_specs=in_specs, out_specs=out_specs,
        out_shape=out_shape, compiler_params=_params("parallel"))(*args)


def _fox_bias_kernel(ff_ref, b_ref, nb_ref):
    s = ff_ref.shape[1]
    z = ff_ref[4:12, :] + b_ref[...]
    lf = jnp.minimum(z, 0.0) - jnp.log1p(jnp.exp(-jnp.abs(z)))
    r = lax.broadcasted_iota(I32, (LANES, LANES), 0)
    c = lax.broadcasted_iota(I32, (LANES, LANES), 1)
    upper = jnp.where(r <= c, 1.0, 0.0).astype(F32)
    carry = jnp.zeros((lf.shape[0], 1), F32)
    for t in range(s // LANES):
        blk = lf[:, t * LANES:(t + 1) * LANES]
        cs = jnp.dot(blk, upper, preferred_element_type=F32,
                     precision=lax.Precision.HIGHEST) + carry
        nb_ref[0, :, t * LANES:(t + 1) * LANES] = cs * (-LOG2E)
        carry = cs[:, LANES - 1:LANES]


def _fox_bias(small_t, b_fox, bsz, s):
    h = b_fox.shape[0]
    return pl.pallas_call(
        _fox_bias_kernel, grid=(bsz,),
        in_specs=[pl.BlockSpec((16, s), lambda b: (0, b)),
                  pl.BlockSpec((h, 1), lambda b: (0, 0))],
        out_specs=pl.BlockSpec((1, h, s), lambda b: (b, 0, 0)),
        out_shape=jax.ShapeDtypeStruct((bsz, h, s), F32),
        compiler_params=_params("parallel"))(small_t, b_fox.reshape(h, 1).astype(F32))


def _dsa_index_kernel(iq_ref, ik_ref, iwt_ref, bias_ref, keys_scr, *, tq, tk, seq, ksel):
    i = pl.program_id(1)
    n_kt = ((i + 1) * tq + tk - 1) // tk
    q_pos = i * tq + lax.broadcasted_iota(I32, (1, tq), 1)
    adm_end = (q_pos // CHUNK + 1) * CHUNK

    iq = iq_ref[...]
    lane = lax.broadcasted_iota(I32, iq.shape, 1)
    iqm = [jnp.where((lane >= HEAD_DIM * h) & (lane < HEAD_DIM * (h + 1)), iq, jnp.zeros_like(iq))
           for h in range(IDX_HEADS)]
    wts = [iwt_ref[h:h + 1, :] * (IDX_HEADS ** -0.5) for h in range(IDX_HEADS)]

    def key_idx(k0):
        return k0 + lax.broadcasted_iota(I32, (tk, 1), 0)

    def to_key(v):
        bits = pltpu.bitcast(v, I32)
        sign = bits >> 31
        return (bits ^ (sign & 0x7FFFFFFF)) - sign

    def from_key(k):
        sign = k >> 31
        return pltpu.bitcast((k + sign) ^ (sign & 0x7FFFFFFF), F32)

    def score_tile(kt, carry):
        kmax, kmin = carry
        k0 = pl.multiple_of(kt * tk, tk)
        ikt = ik_ref[pl.ds(k0, tk), :]
        sc = jnp.zeros((tk, tq), F32)
        for h in range(IDX_HEADS):
            sc = sc + jnp.maximum(_dot_nt(ikt, iqm[h]), 0.0) * wts[h]
        key = to_key(sc)
        adm = key_idx(k0) < adm_end
        k_lo = jnp.where(adm, key, INT_MIN)
        k_hi = jnp.where(adm, key, INT_MAX)
        keys_scr[pl.ds(k0, tk), :] = k_lo
        for g in range(tk // 8):
            kmax = jnp.maximum(kmax, k_lo[g * 8:(g + 1) * 8, :])
            kmin = jnp.minimum(kmin, k_hi[g * 8:(g + 1) * 8, :])
        return kmax, kmin

    kmax, kmin = lax.fori_loop(0, n_kt, score_tile,
                               (jnp.full((8, tq), INT_MIN, I32), jnp.full((8, tq), INT_MAX, I32)))
    kmax = jnp.max(kmax, axis=0, keepdims=True)
    kmin = jnp.min(kmin, axis=0, keepdims=True)

    kf = float(ksel)
    n_acc = 4
    grp = 8 * n_acc

    def count(pred):
        def body(kt, accs):
            k0 = pl.multiple_of(kt * tk, tk)
            accs = list(accs)
            for g in range(tk // grp):
                kk = keys_scr[pl.ds(k0 + g * grp, grp), :]
                hit = jnp.where(pred(kk, k0 + g * grp), 1.0, 0.0)
                for a in range(n_acc):
                    accs[a] = accs[a] + hit[a * 8:(a + 1) * 8, :]
            return tuple(accs)
        accs = lax.fori_loop(0, n_kt, body, tuple(jnp.zeros((8, tq), F32) for _ in range(n_acc)))
        tot = accs[0]
        for a in range(1, n_acc):
            tot = tot + accs[a]
        return jnp.sum(tot, axis=0, keepdims=True)

    def count_ge(cand):
        return count(lambda kk, k0: kk >= cand)

    n_adm = adm_end.astype(F32)
    zero = jnp.zeros((1, tq), F32)
    one = jnp.ones((1, tq), F32)
    log_k = math.log(kf)
    done0 = jnp.where(n_adm <= kf, 1.0, 0.0)
    init = (jnp.int32(0), kmin, kmax + 1, n_adm, zero, jnp.full((1, tq), INT_MIN, I32), done0, zero,
            one, one, zero)

    def cond(st):
        return (st[0] < 264) & (jnp.min(st[6]) == 0.0)

    def step(st):
        it, lo, hi, c_lo, c_hi, thr, done, tie, w_lo, w_hi, last = st
        lo_v = from_key(lo)
        hi_v = from_key(hi)
        f_lo = (jnp.log(c_lo + 0.5) - log_k) * w_lo
        f_hi = (log_k - jnp.log(c_hi + 0.5)) * w_hi
        cand = to_key(lo_v + (hi_v - lo_v) * (f_lo / (f_lo + f_hi)))
        cand = jnp.where(((it - 3) & 7) == 7, (lo >> 1) + (hi >> 1) + (lo & hi & 1), cand)
        cand = jnp.where(it == 0, kmax, cand)
        cand = jnp.where(it == 1, 0, cand)
        cand = jnp.where(it == 2, 1, cand)
        cand = jnp.minimum(jnp.maximum(cand, lo + 1), hi - 1)
        c = count_ge(cand)
        active = done == 0.0
        up = active & (c >= kf)
        down = active & (c < kf)
        lo = jnp.where(up, cand, lo)
        c_lo = jnp.where(up, c, c_lo)
        hi = jnp.where(down, cand, hi)
        c_hi = jnp.where(down, c, c_hi)
        w_hi = jnp.where(up & (last > 0.0), 0.5 * w_hi, jnp.where(down, 1.0, w_hi))
        w_lo = jnp.where(down & (last < 0.0), 0.5 * w_lo, jnp.where(up, 1.0, w_lo))
        last = jnp.where(up, 1.0, jnp.where(down, -1.0, last))
        hit = active & (c == kf)
        conv = active & jnp.logical_not(hit) & (hi - 1 <= lo)
        thr = jnp.where(hit, cand - 1, jnp.where(conv, lo, thr))
        tie = jnp.where(conv, 1.0, tie)
        done = jnp.where(hit | conv, 1.0, done)
        return it + 1, lo, hi, c_lo, c_hi, thr, done, tie, w_lo, w_hi, last

    def steps(st):
        for _ in range(DSA_PROBES_PER_CHECK):
            st = step(st)
        return st

    st = lax.while_loop(cond, steps, init)
    c_hi, thr, tie = st[4], st[5], st[7]

    def store_tile(k0, val):
        bias_ref[0, :, pl.ds(k0, tk)] = val
    any_tie = jnp.max(tie) > 0.0

    @pl.when(jnp.logical_not(any_tie))
    def _():
        def write_tile(kt, carry):
            k0 = pl.multiple_of(kt * tk, tk)
            sel = keys_scr[pl.ds(k0, tk), :] > thr
            store_tile(k0, jnp.where(sel, 0.0, NEG).T.astype(BF16))
            return carry

        lax.fori_loop(0, n_kt, write_tile, 0)

    @pl.when(any_tie)
    def _():
        need = jnp.where(tie > 0.0, kf - c_hi, 0.0)
        lower = jnp.where(lax.broadcasted_iota(I32, (tk, tk), 0) >= lax.broadcasted_iota(I32, (tk, tk), 1),
                          1.0, 0.0).astype(BF16)

        def write_tile(kt, seen):
            k0 = pl.multiple_of(kt * tk, tk)
            kk = keys_scr[pl.ds(k0, tk), :]
            eq = kk == thr
            rank = jnp.dot(lower, jnp.where(eq, 1.0, 0.0).astype(BF16), preferred_element_type=F32) + seen
            sel = (kk > thr) | (eq & (rank <= need))
            store_tile(k0, jnp.where(sel, 0.0, NEG).T.astype(BF16))
            return rank[tk - 1:tk, :]

        lax.fori_loop(0, n_kt, write_tile, jnp.zeros((1, tq), F32))

    def fill_tile(kt, carry):
        k0 = pl.multiple_of(kt * tk, tk)
        store_tile(k0, jnp.full((tq, tk), NEG, BF16))
        return carry

    lax.fori_loop(n_kt, seq // tk, fill_tile, 0)


def _dsa_index(iq, ik4, iw_t, bsz, seq, tq, tk):
    ksel = min(DSA_TOPK_MAX, seq // 4)
    nq = seq // tq
    kern = functools.partial(_dsa_index_kernel, tq=tq, tk=tk, seq=seq, ksel=ksel)
    return pl.pallas_call(
        kern, grid=(bsz, nq),
        in_specs=[pl.BlockSpec((tq, 256), lambda b, i: (b * nq + i, 0)),
                  pl.BlockSpec((seq, 256), lambda b, i: (b, 0)),
                  pl.BlockSpec((8, tq), lambda b, i: (0, b * nq + i))],
        out_specs=pl.BlockSpec((1, tq, seq), lambda b, i: (b, i, 0)),
        out_shape=jax.ShapeDtypeStruct((bsz, seq, seq), BF16),
        scratch_shapes=[pltpu.VMEM((seq, tq), I32)],
        compiler_params=_params("parallel", "parallel"))(iq, ik4, iw_t)


def _flash_kernel(*refs, tq, tk, use_mask):
    if use_mask:
        q_ref, k_ref, v_ref, mask_ref, o_ref, m_scr, acc_scr, mbuf, sem = refs
        nb_ref = None
    else:
        q_ref, k_ref, v_ref, nb_ref, o_ref, m_scr, acc_scr = refs
        mask_ref = None
    b = pl.program_id(0)
    i = pl.program_id(2)
    q = q_ref[...]
    head0 = lax.broadcasted_iota(I32, (tq, LANES), 1) < HEAD_DIM
    head0k = lax.broadcasted_iota(I32, (tk, LANES), 1) < HEAD_DIM
    qm = [jnp.where(head0, q, jnp.zeros_like(q)), jnp.where(head0, jnp.zeros_like(q), q)]
    m_scr[...] = jnp.full(m_scr.shape, NEG, F32)
    acc_scr[...] = jnp.zeros(acc_scr.shape, F32)
    n_col = tk // LANES
    n_sub = tq // tk
    n_off = i * n_sub

    def mask_dma(kt, slot, r0):
        return pltpu.make_async_copy(
            mask_ref.at[b, pl.ds(i * tq + r0, tq - r0), pl.ds(pl.multiple_of(kt * tk, tk), tk)],
            mbuf.at[slot, pl.ds(r0, tq - r0), :], sem.at[slot])

    def tile(kt, diag):
        r0 = 0 if diag is None else diag * tk
        nr = tq - r0
        k0 = pl.multiple_of(kt * tk, tk)
        kt_ = k_ref[pl.ds(k0, tk), :]
        vt = v_ref[pl.ds(k0, tk), :]
        one = jnp.ones_like(vt)
        vx = [jnp.where(head0k, vt, one), jnp.where(head0k, one, vt)]
        if use_mask:
            slot = kt & 1
            if diag is None:
                mask_dma(kt + 1, 1 - slot, 0).start()
            elif diag + 1 < n_sub:
                mask_dma(kt + 1, 1 - slot, (diag + 1) * tk).start()
            mask_dma(kt, slot, r0).wait()
            extra = mbuf[slot, pl.ds(r0, nr), :].astype(F32)
        ss = []
        for j in range(2):
            s = _dot_nt(qm[j][r0:, :], kt_)
            if use_mask:
                s = s + extra
            else:
                s = s + nb_ref[0, 0, j:j + 1, pl.ds(k0, tk)]
                if diag is not None:
                    rr = lax.broadcasted_iota(I32, (nr, tk), 0)
                    cc = lax.broadcasted_iota(I32, (nr, tk), 1)
                    s = jnp.where(cc <= rr, s, NEG)
            ss.append(s)
        ps = []
        for j in range(2):
            sc = [ss[j][:, c * LANES:(c + 1) * LANES] for c in range(n_col)]
            part = sc[0]
            for c in range(1, n_col):
                part = jnp.maximum(part, sc[c])
            m_old = m_scr[j, r0:, :]
            m_new = jnp.maximum(m_old, jnp.max(part, axis=1, keepdims=True))
            alpha = jnp.exp2(m_old - m_new)
            ps.append((alpha, jnp.concatenate([jnp.exp2(c_ - m_new) for c_ in sc], axis=1).astype(BF16)))
            m_scr[j, r0:, :] = m_new
        for j in range(2):
            alpha, p = ps[j]
            acc_scr[j, r0:, :] = alpha * acc_scr[j, r0:, :] + jnp.dot(p, vx[j], preferred_element_type=F32)

    def body(kt, carry):
        tile(kt, None)
        return carry

    if use_mask:
        mask_dma(0, 0, 0).start()
    lax.fori_loop(0, n_off, body, 0)
    for d in range(n_sub):
        tile(n_off + d, d)

    a0 = acc_scr[0]
    a1 = acc_scr[1]
    o0 = a0 / pltpu.roll(a0, HEAD_DIM, axis=1)
    o1 = a1 / pltpu.roll(a1, HEAD_DIM, axis=1)
    o_ref[...] = jnp.where(head0, o0, o1).astype(o_ref.dtype)


def _flash(q, k, v, bsz, seq, tq, tk, nb=None, mask=None):
    nq = seq // tq
    n_hp = q.shape[1] // LANES
    use_mask = mask is not None
    in_specs = [pl.BlockSpec((tq, LANES), lambda b, h, i: (b * nq + i, h)),
                pl.BlockSpec((seq, LANES), lambda b, h, i: (b, h)),
                pl.BlockSpec((seq, LANES), lambda b, h, i: (b, h))]
    scratch = [pltpu.VMEM((2, tq, LANES), F32), pltpu.VMEM((2, tq, LANES), F32)]
    if use_mask:
        in_specs.append(pl.BlockSpec(memory_space=pl.ANY))
        scratch += [pltpu.VMEM((2, tq, tk), BF16), pltpu.SemaphoreType.DMA((2,))]
        extra = mask
    else:
        in_specs.append(pl.BlockSpec((1, 1, 2, seq), lambda b, h, i: (b, h, 0, 0)))
        extra = nb
    kern = functools.partial(_flash_kernel, tq=tq, tk=tk, use_mask=use_mask)
    return pl.pallas_call(
        kern, grid=(bsz, n_hp, nq), in_specs=in_specs,
        out_specs=pl.BlockSpec((tq, LANES), lambda b, h, i: (b * nq + i, h)),
        out_shape=jax.ShapeDtypeStruct(q.shape, BF16),
        scratch_shapes=scratch,
        compiler_params=_params("parallel", "parallel", "parallel"))(q, k, v, extra)


def _hgrn_kernel(q_ref, zf_ref, v_ref, g_ref, lb_ref, ng_ref, o_ref, st_scr, oi_scr, *, tt):
    r = HGRN_SUB
    half = 256

    @pl.when(pl.program_id(1) == 0)
    def _():
        st_scr[...] = jnp.zeros(st_scr.shape, F32)

    q = q_ref[...]
    zf = zf_ref[...]
    v = v_ref[...]
    ls = jnp.minimum(zf, 0.0) - jnp.log1p(jnp.exp(-jnp.abs(zf)))
    a = lb_ref[0:1, :]
    y = lb_ref[1:2, :] + ls
    lf = jnp.maximum(a, y) + jnp.log1p(jnp.exp(-jnp.abs(a - y)))
    kk = lb_ref[2:3, :] * (1.0 / (1.0 + jnp.exp(zf)))

    rin = lax.broadcasted_iota(I32, (tt, 1), 0) & (r - 1)
    b = lf
    sh = 1
    while sh < r:
        b = b + jnp.where(rin >= sh, pltpu.roll(b, sh, axis=0), 0.0)
        sh *= 2

    rr = lax.broadcasted_iota(I32, (half, half), 0) // HEAD_DIM
    cc = lax.broadcasted_iota(I32, (half, half), 1) // HEAD_DIM
    same_head = rr == cc
    ones_bd = jnp.where(same_head, 1.0, 0.0).astype(BF16)
    bd_mask = jnp.where(same_head, 1.0, 0.0).astype(F32)

    o = jnp.zeros((tt, 2 * half), F32)
    for off in range(r):
        if off == 0:
            e = q * kk
            vs = v
        else:
            e = q * pltpu.roll(kk, off, axis=0) * jnp.exp(b - pltpu.roll(b, off, axis=0))
            e = jnp.where(rin >= off, e, 0.0)
            vs = pltpu.roll(v, off, axis=0)
        eb = e.astype(BF16)
        sc = jnp.concatenate(
            [jnp.dot(eb[:, 0:half], ones_bd, preferred_element_type=F32),
             jnp.dot(eb[:, half:], ones_bd, preferred_element_type=F32)], axis=1)
        o = o + sc * vs

    qe = (q * jnp.exp(b)).astype(BF16)
    vb = v.astype(BF16)
    for c in range(tt // r):
        r0 = c * r
        bl = b[r0 + r - 1:r0 + r, :]
        kd = (kk[r0:r0 + r, :] * jnp.exp(bl - b[r0:r0 + r, :])).astype(BF16)
        dec = jnp.exp(bl)
        for h2 in range(2):
            lo = h2 * half
            st = st_scr[h2]
            oi_scr[r0:r0 + r, lo:lo + half] = _dot_nt(qe[r0:r0 + r, lo:lo + half], st.astype(BF16))
            upd = _dot_tn(vb[r0:r0 + r, lo:lo + half], kd[:, lo:lo + half])
            st_scr[h2] = st * dec[:, lo:lo + half] + upd * bd_mask

    o = o + oi_scr[...]
    o2 = o * o
    o2_hi = o2.astype(BF16)
    o2_lo = (o2 - o2_hi.astype(F32)).astype(BF16)
    ms = jnp.concatenate(
        [jnp.dot(o2_hi[:, 0:half], ones_bd, preferred_element_type=F32)
         + jnp.dot(o2_lo[:, 0:half], ones_bd, preferred_element_type=F32),
         jnp.dot(o2_hi[:, half:], ones_bd, preferred_element_type=F32)
         + jnp.dot(o2_lo[:, half:], ones_bd, preferred_element_type=F32)],
        axis=1) * (1.0 / HEAD_DIM)
    on = o * lax.rsqrt(ms + RMS_EPS) * ng_ref[...] * _sigmoid(g_ref[...])
    o_ref[...] = on.astype(o_ref.dtype)


def _hgrn(zc, lbp, ng, bsz, seq, tt):
    n = zc.shape[0]
    nt = seq // tt
    col = lambda cidx: pl.BlockSpec((tt, 512), lambda b, j: (b * nt + j, cidx))
    kern = functools.partial(_hgrn_kernel, tt=tt)
    return pl.pallas_call(
        kern, grid=(bsz, nt),
        in_specs=[col(0), col(1), col(2), col(3),
                  pl.BlockSpec((3, 512), lambda b, j: (0, 0)),
                  pl.BlockSpec((1, 512), lambda b, j: (0, 0))],
        out_specs=pl.BlockSpec((tt, 512), lambda b, j: (b * nt + j, 0)),
        out_shape=jax.ShapeDtypeStruct((n, 512), BF16),
        scratch_shapes=[pltpu.VMEM((2, 256, 256), F32), pltpu.VMEM((tt, 512), F32)],
        compiler_params=_params("parallel", "arbitrary"))(zc, zc, zc, zc, lbp, ng)


def _layer_norm(y, g, b):
    mu = jnp.mean(y, axis=-1, keepdims=True)
    d = y - mu
    var = jnp.mean(d * d, axis=-1, keepdims=True)
    return d * lax.rsqrt(var + LN_EPS) * g + b


def _merge_kernel(oa_ref, ob_ref, oc_ref, gt_ref, x_ref, wpa_ref, wpb_ref, wpc_ref, wout_ref,
                  g_ref, b_ref, wr_ref, br_ref, x1_ref, xb_ref, ti_ref, gate_ref, *, alpha):
    d = x_ref.shape[1]
    pa = jnp.dot(oa_ref[...], wpa_ref[...], preferred_element_type=F32)
    pb = jnp.dot(ob_ref[...], wpb_ref[...], preferred_element_type=F32)
    pc = jnp.dot(oc_ref[...], wpc_ref[...], preferred_element_type=F32)
    merged = (_sigmoid(gt_ref[:, 0:d]) * pa + _sigmoid(gt_ref[:, d:2 * d]) * pb
              + _sigmoid(gt_ref[:, 2 * d:3 * d]) * pc)
    mix = jnp.dot(merged.astype(BF16), wout_ref[...], preferred_element_type=F32)
    x1 = _layer_norm(alpha * x_ref[...] + mix, g_ref[...], b_ref[...])
    x1_ref[...] = x1
    xb_ref[...] = x1.astype(BF16)

    x_hi = x1.astype(BF16)
    x_lo = (x1 - x_hi.astype(F32)).astype(BF16)
    logits = (jnp.dot(x_hi, wr_ref[0], preferred_element_type=F32)
              + jnp.dot(x_hi, wr_ref[1], preferred_element_type=F32)
              + jnp.dot(x_lo, wr_ref[0], preferred_element_type=F32)) + br_ref[...]
    lane = lax.broadcasted_iota(I32, logits.shape, 1)
    topi = jnp.zeros(logits.shape, I32)
    topv = jnp.full(logits.shape, NEG, F32)
    for k in range(TOP_K):
        m = jnp.max(logits, axis=1, keepdims=True)
        idx = jnp.min(jnp.where(logits == m, lane, LANES), axis=1, keepdims=True)
        topi = jnp.where(lane == k, idx, topi)
        topv = jnp.where(lane == k, m, topv)
        logits = jnp.where(lane == idx, -jnp.inf, logits)
    e = jnp.where(lane < TOP_K, jnp.exp(topv - jnp.max(topv, axis=1, keepdims=True)), 0.0)
    ti_ref[...] = topi
    gate_ref[...] = e / jnp.sum(e, axis=1, keepdims=True)


def _merge(oa, ob, oc, gt, x2, wpa, wpb, wpc, wout, g, b, wr, br, alpha, tm):
    n, d = x2.shape
    row = lambda w: pl.BlockSpec((tm, w), lambda i: (i, 0))
    full = lambda a: pl.BlockSpec(a.shape, lambda i: (0,) * a.ndim)
    kern = functools.partial(_merge_kernel, alpha=alpha)
    return pl.pallas_call(
        kern, grid=(n // tm,),
        in_specs=[row(512), row(512), row(512), row(3 * d), row(d),
                  full(wpa), full(wpb), full(wpc), full(wout), full(g), full(b), full(wr), full(br)],
        out_specs=[row(d), row(d), row(LANES), row(LANES)],
        out_shape=[jax.ShapeDtypeStruct((n, d), F32), jax.ShapeDtypeStruct((n, d), BF16),
                   jax.ShapeDtypeStruct((n, LANES), I32), jax.ShapeDtypeStruct((n, LANES), F32)],
        compiler_params=_params("parallel"))(oa, ob, oc, gt, x2, wpa, wpb, wpc, wout, g, b, wr, br)


def _moe_kernel(be_ref, nu_ref, xs_ref, wgu_ref, bgu_ref, wd_ref, bd_ref, y_ref, wgu_bf, wd_bf):
    i = pl.program_id(0)
    de = wd_ref.shape[1]

    @pl.when((i == 0) | (be_ref[i] != be_ref[jnp.maximum(i - 1, 0)]))
    def _():
        wgu_bf[...] = wgu_ref[0].astype(BF16)
        wd_bf[...] = wd_ref[0].astype(BF16)

    @pl.when(i < nu_ref[0])
    def _():
        gu = jnp.dot(xs_ref[...].astype(BF16), wgu_bf[...], preferred_element_type=F32) + bgu_ref[0]
        gate = jnp.minimum(gu[:, 0:de], SWIGLU_LIMIT)
        up = jnp.clip(gu[:, de:], -SWIGLU_LIMIT, SWIGLU_LIMIT)
        act = gate * _sigmoid(SWIGLU_ALPHA * gate) * (up + 1.0)
        y = jnp.dot(act.astype(BF16), wd_bf[...], preferred_element_type=F32) + bd_ref[0]
        y_ref[...] = y.astype(y_ref.dtype)

    @pl.when(i >= nu_ref[0])
    def _():
        y_ref[...] = jnp.zeros(y_ref.shape, y_ref.dtype)


def _moe_experts(layer, blk_e, n_used, xs, wgu, bgu, wd, bd):
    n_slots, d = xs.shape
    n_blocks = n_slots // MOE_BLOCK
    _, _, n2 = wgu.shape
    de = wd.shape[1]
    base = layer * N_EXPERTS
    grid_spec = pltpu.PrefetchScalarGridSpec(
        num_scalar_prefetch=2, grid=(n_blocks,),
        in_specs=[pl.BlockSpec((MOE_BLOCK, d), lambda i, be, nu: (i, 0)),
                  pl.BlockSpec((1, d, n2), lambda i, be, nu: (base + be[i], 0, 0)),
                  pl.BlockSpec((1, 1, n2), lambda i, be, nu: (base + be[i], 0, 0)),
                  pl.BlockSpec((1, de, d), lambda i, be, nu: (base + be[i], 0, 0)),
                  pl.BlockSpec((1, 1, d), lambda i, be, nu: (base + be[i], 0, 0))],
        out_specs=pl.BlockSpec((MOE_BLOCK, d), lambda i, be, nu: (i, 0)),
        scratch_shapes=[pltpu.VMEM((d, n2), BF16), pltpu.VMEM((de, d), BF16)])
    return pl.pallas_call(
        _moe_kernel, grid_spec=grid_spec,
        out_shape=jax.ShapeDtypeStruct((n_slots, d), BF16),
        compiler_params=_params("arbitrary"))(blk_e, n_used, xs, wgu, bgu, wd, bd)


def _combine_kernel(yg_ref, gate_ref, x1_ref, g_ref, b_ref, o_ref, *, alpha):
    gates = gate_ref[...]
    acc = alpha * x1_ref[...]
    for k in range(TOP_K):
        acc = acc + gates[:, k:k + 1] * yg_ref[k].astype(F32)
    o_ref[...] = _layer_norm(acc, g_ref[...], b_ref[...])


def _combine(yg, gates, x1, g, b, alpha, tm):
    n, d = x1.shape
    kern = functools.partial(_combine_kernel, alpha=alpha)
    return pl.pallas_call(
        kern, grid=(n // tm,),
        in_specs=[pl.BlockSpec((TOP_K, tm, d), lambda i: (0, i, 0)),
                  pl.BlockSpec((tm, LANES), lambda i: (i, 0)),
                  pl.BlockSpec((tm, d), lambda i: (i, 0)),
                  pl.BlockSpec((1, d), lambda i: (0, 0)),
                  pl.BlockSpec((1, d), lambda i: (0, 0))],
        out_specs=pl.BlockSpec((tm, d), lambda i: (i, 0)),
        out_shape=jax.ShapeDtypeStruct((n, d), F32),
        compiler_params=_params("parallel"))(yg, gates, x1, g, b)


def _rope_tables(seq, width):
    half = HEAD_DIM // 2
    inv = ROPE_THETA ** (-jnp.arange(half, dtype=F32) / half)
    ang = jnp.arange(seq, dtype=F32)[:, None] * inv[None, :]
    cos = jnp.concatenate([jnp.cos(ang), jnp.cos(ang)], axis=1)
    sin = jnp.concatenate([-jnp.sin(ang), jnp.sin(ang)], axis=1)
    reps = width // HEAD_DIM
    return jnp.tile(cos, (1, reps)), jnp.tile(sin, (1, reps))


def _split_w_in(w):
    seg = lambda a, b: w[:, _OFF[a]:_OFF[b]]
    w_dsa = seg(0, 3)
    small = jnp.concatenate([seg(5, 6), seg(9, 10), jnp.zeros((w.shape[0], LANES - 12), w.dtype)], axis=1)
    w_idx = jnp.concatenate([seg(3, 4)] + [seg(4, 5)] * IDX_HEADS + [small], axis=1)
    w_fox = seg(6, 9)
    w_hgrn = seg(10, 14)
    w_gate = seg(14, 15)
    return [t.astype(BF16) for t in (w_dsa, w_idx, w_fox, w_hgrn, w_gate)]


def _route(top_idx, n_tok):
    n_assign = n_tok * TOP_K
    flat_e = top_idx.reshape(-1)
    experts = jnp.arange(N_EXPERTS, dtype=I32)
    onehot = (flat_e[None, :] == experts[:, None]).astype(F32)
    oh3 = onehot.reshape(N_EXPERTS, n_assign // LANES, LANES)
    upper = (jnp.arange(LANES)[:, None] <= jnp.arange(LANES)[None, :]).astype(F32)
    within = jnp.einsum('ebl,lm->ebm', oh3, upper)
    btot = within[:, :, LANES - 1]
    csum = (within + (jnp.cumsum(btot, axis=1) - btot)[:, :, None]).reshape(N_EXPERTS, n_assign)
    counts = jnp.sum(btot, axis=1).astype(I32)
    rank = (jnp.sum(csum * onehot, axis=0) - 1.0).astype(I32)
    padded = (counts + MOE_BLOCK - 1) // MOE_BLOCK * MOE_BLOCK
    pad_end = jnp.cumsum(padded)
    pad_start = pad_end - padded
    start = jnp.cumsum(counts) - counts
    slot_of = pad_start[flat_e] + rank
    n_blocks = -(-(n_assign + N_EXPERTS * (MOE_BLOCK - 1)) // MOE_BLOCK)
    blk_start = jnp.arange(n_blocks, dtype=I32) * MOE_BLOCK
    blk_e = jnp.sum((pad_end[None, :] <= blk_start[:, None]).astype(I32), axis=1)
    blk_e = jnp.minimum(blk_e, N_EXPERTS - 1).astype(I32)
    n_used = (pad_end[-1] // MOE_BLOCK).astype(I32).reshape(1)
    order = jnp.argsort(flat_e).astype(I32)
    slot = jnp.arange(n_blocks * MOE_BLOCK, dtype=I32)
    slot_e = jnp.repeat(blk_e, MOE_BLOCK)
    r = slot - pad_start[slot_e]
    src = jnp.clip(start[slot_e] + r, 0, n_assign - 1)
    slot_tok = jnp.where(r < counts[slot_e], order[src] // TOP_K, slot % n_tok)
    return slot_tok, blk_e, n_used, slot_of.reshape(n_tok, TOP_K).T


def _layer(layer, x2, bsz, seq, alpha, w_in, b_fox, lb, ng, wpa, wpb, wpc, wout, g1, b1,
           wr, br, wgu, bgu, wd, bd, g2, b2, cos, sin):
    n, d = x2.shape
    tm = min(PROJ_TM, seq)
    tmq = min(PROJ_QKV_TM, seq)
    w_dsa, w_idx, w_fox, w_hgrn, w_gate = _split_w_in(w_in)
    nst = seq // tmq
    tab = lambda arr: (arr, pl.BlockSpec((tmq, 512), lambda i: (i % nst, 0)))

    aq, ak, av = _proj_call(_proj_dsa_kernel, x2, w_dsa, [tab(cos), tab(sin)],
                            [(512, BF16)] * 3, tmq)
    iq, ik4, small_t = _proj_call(_proj_idx_kernel, x2, w_idx, [tab(cos), tab(sin)],
                                [(256, BF16), (256, BF16), (LANES, F32, True)], tmq)
    fq, fk, fv = _proj_call(_proj_fox_kernel, x2, w_fox, [], [(512, BF16)] * 3, tmq)
    (zc,) = _proj_call(_proj_plain_kernel, x2, w_hgrn, [], [(2048, F32)], tm)
    (gt,) = _proj_call(_proj_plain_kernel, x2, w_gate, [], [(3 * d, F32)], tm)

    tq_att, tk_att = min(FLASH_TQ, seq), min(FLASH_TK, seq)
    mask = _dsa_index(iq, ik4, small_t, bsz, seq, min(256, seq), tk_att)
    o_a = _flash(aq, ak, av, bsz, seq, tq_att, tk_att, mask=mask)

    nb = _fox_bias(small_t, b_fox, bsz, seq).reshape(bsz, 4, 2, seq)
    o_b = _flash(fq, fk, fv, bsz, seq, tq_att, tk_att, nb=nb)

    lbp = jnp.stack([jnp.log(lb), jnp.log1p(-lb), 1.0 - lb]).astype(F32)
    o_c = _hgrn(zc, lbp, ng.reshape(1, -1).astype(F32), bsz, seq, min(HGRN_TILE, seq))

    wr_f = jnp.zeros((d, LANES), F32).at[:, :N_EXPERTS].set(wr.astype(F32))
    wr_hi = wr_f.astype(BF16)
    wr_p = jnp.stack([wr_hi, (wr_f - wr_hi.astype(F32)).astype(BF16)])
    br_p = jnp.full((1, LANES), NEG, F32).at[0, :N_EXPERTS].set(br.astype(F32))
    x1, x1b, topi, gates = _merge(
        o_a, o_b, o_c, gt, x2, wpa.astype(BF16), wpb.astype(BF16), wpc.astype(BF16),
        wout.astype(BF16), g1.reshape(1, d), b1.reshape(1, d), wr_p, br_p, alpha, min(MERGE_TM, n))

    slot_tok, blk_e, n_used, slot_of = _route(topi[:, :TOP_K], n)
    xs = x1b[slot_tok]
    y_slots = _moe_experts(layer, blk_e, n_used, xs, wgu, bgu, wd, bd)
    yg = y_slots[slot_of]
    return _combine(yg, gates, x1, g2.reshape(1, d), b2.reshape(1, d), alpha, min(COMBINE_TM, n))


def kernel(x, w_in, b_fox_f, hgrn_lb_logits, hgrn_norm_g, w_branch_a, w_branch_b, w_branch_c, w_out, ln1_g, ln1_b, w_router, b_router, w_gu, b_gu, w_down, b_down, ln2_g, ln2_b):
    bsz, seq, d = x.shape
    depth = w_in.shape[0]
    alpha = (2 * depth) ** 0.25
    p = jax.nn.softmax(hgrn_lb_logits.astype(F32), axis=0)
    lbs = jnp.cumsum(p, axis=0)
    lbs = lbs - lbs[0]
    cos, sin = _rope_tables(seq, 512)
    x2 = x.reshape(bsz * seq, d)
    n_e = depth * w_gu.shape[1]
    wgu = w_gu.reshape(n_e, d, w_gu.shape[3])
    bgu = b_gu.reshape(n_e, 1, b_gu.shape[2])
    wd = w_down.reshape(n_e, w_down.shape[2], d)
    bd = b_down.reshape(n_e, 1, d)
    for l in range(depth):
        x2 = _layer(l, x2, bsz, seq, alpha, w_in[l], b_fox_f[l], lbs[l], hgrn_norm_g[l],
                    w_branch_a[l], w_branch_b[l], w_branch_c[l], w_out[l], ln1_g[l], ln1_b[l],
                    w_router[l], b_router[l], wgu, bgu, wd, bd, ln2_g[l], ln2_b[l], cos, sin)
    return x2.reshape(bsz, seq, d)
```
